```python
import jax, jax.numpy as jnp
from jax import lax
import numpy as np

D_MODEL = 1024
BATCH = 2
SEQ = 16384
DEPTH = 4

N_MIXERS = 3
PLE_DIM = 256
D_FF = 2816
EPS = 1e-6
BLOCK = 128

GDN_HEADS = 8
GDN_DK = 128
GDN_DV = 128
GDN_CONV = 4
GDN_CHUNK = 64
GDN_QK = GDN_HEADS * GDN_DK
GDN_V = GDN_HEADS * GDN_DV
GDN_CONV_DIM = 2 * GDN_QK + GDN_V
GDN_IN = GDN_CONV_DIM + GDN_V + 2 * GDN_HEADS

SB_HEADS = 16
SB_HD = 64

MLA_HEADS = 8
MLA_Q_RANK = 384
MLA_KV_RANK = 256
MLA_NOPE = 128
MLA_ROPE = 64
MLA_V = 128
ROPE_THETA = 10000.0

N_A = len(range(0, DEPTH, N_MIXERS))
N_B = len(range(1, DEPTH, N_MIXERS))
N_C = len(range(2, DEPTH, N_MIXERS))

kernel_name = "hybrid_gdn_stickbreak_mla_macaron"

F32 = jnp.float32


def rmsnorm(x, g):
    xf = x.astype(F32)
    y = xf * lax.rsqrt(jnp.mean(xf * xf, axis=-1, keepdims=True) + EPS)
    return (y * g.astype(F32)).astype(x.dtype)


def l2norm(x):
    xf = x.astype(F32)
    return xf * lax.rsqrt(jnp.sum(xf * xf, axis=-1, keepdims=True) + EPS)


def swiglu(h, w_gate, w_up, w_down):
    return (jax.nn.silu(h @ w_gate) * (h @ w_up)) @ w_down


def causal_dwconv(x, w):
    c = x.shape[-1]
    return lax.conv_general_dilated(
        x, w[:, None, :].astype(x.dtype), window_strides=(1,),
        padding=[(w.shape[0] - 1, 0)], dimension_numbers=('NWC', 'WIO', 'NWC'),
        feature_group_count=c)


def chunk_gated_delta(q, k, v, beta, g):
    B, L, H, dk = q.shape
    dv = v.shape[-1]
    C = GDN_CHUNK
    N = L // C

    def chunks(t):
        return t.astype(F32).reshape(B, N, C, H, *t.shape[3:]).swapaxes(2, 3)

    q, k, v, beta, g = map(chunks, (q, k, v, beta, g))
    g = jnp.cumsum(g, axis=-1)
    tril_incl = jnp.tril(jnp.ones((C, C), bool))
    tril_strict = jnp.tril(jnp.ones((C, C), bool), -1)
    diff = g[..., :, None] - g[..., None, :]
    decay = jnp.where(tril_incl, jnp.exp(jnp.where(tril_incl, diff, 0.0)), 0.0)

    kb = k * beta[..., None]
    lmat = jnp.where(tril_strict, jnp.einsum('bnhtd,bnhsd->bnhts', kb, k) * decay, 0.0)
    rhs = jnp.concatenate([v * beta[..., None], kb * jnp.exp(g)[..., None]], axis=-1)
    sol = lax.linalg.triangular_solve(lmat + jnp.eye(C, dtype=F32), rhs,
                                      left_side=True, lower=True, unit_diagonal=True)
    u, w = sol[..., :dv], sol[..., dv:]

    attn = jnp.where(tril_incl, jnp.einsum('bnhtd,bnhsd->bnhts', q, k) * decay, 0.0)
    q_dec = q * jnp.exp(g)[..., None]
    k_dec = k * jnp.exp(g[..., -1:] - g)[..., None]
    g_last = jnp.exp(g[..., -1])

    def step(S, xs):
        u_c, w_c, attn_c, qd_c, kd_c, gl_c = xs
        v_new = u_c - jnp.einsum('bhcd,bhde->bhce', w_c, S)
        o_c = (jnp.einsum('bhcd,bhde->bhce', qd_c, S)
               + jnp.einsum('bhts,bhse->bhte', attn_c, v_new))
        S = S * gl_c[..., None, None] + jnp.einsum('bhcd,bhce->bhde', kd_c, v_new)
        return S, o_c

    xs = tuple(t.swapaxes(0, 1) for t in (u, w, attn, q_dec, k_dec, g_last))
    S0 = jnp.zeros((B, H, dk, dv), F32)
    _, o = lax.scan(step, S0, xs)
    return o.transpose(1, 0, 3, 2, 4).reshape(B, L, H, dv)


def gated_deltanet(h, w_in, conv_w, a_log, dt_bias, out_norm, w_out):
    B, L, _ = h.shape
    H = GDN_HEADS
    proj = h @ w_in
    qkv, z, b_raw, a_raw = jnp.split(
        proj, [GDN_CONV_DIM, GDN_CONV_DIM + GDN_V, GDN_CONV_DIM + GDN_V + H], axis=-1)
    qkv = jax.nn.silu(causal_dwconv(qkv, conv_w))
    q, k, v = jnp.split(qkv, [GDN_QK, 2 * GDN_QK], axis=-1)
    q = l2norm(q.reshape(B, L, H, GDN_DK)) * GDN_DK ** -0.5
    k = l2norm(k.reshape(B, L, H, GDN_DK))
    v = v.reshape(B, L, H, GDN_DV)
    beta = jax.nn.sigmoid(b_raw.astype(F32))
    g = -jnp.exp(a_log.astype(F32)) * jax.nn.softplus(a_raw.astype(F32) + dt_bias.astype(F32))
    o = chunk_gated_delta(q, k, v, beta, g).astype(h.dtype)
    o = rmsnorm(o, out_norm) * jax.nn.silu(z.reshape(B, L, H, GDN_DV))
    return o.reshape(B, L, GDN_V) @ w_out


def stick_breaking(h, w_qkv, w_out):
    B, L, _ = h.shape
    H, d = SB_HEADS, SB_HD
    nb = L // BLOCK
    q, k, v = jnp.split(h @ w_qkv, 3, axis=-1)

    def blocks(t):
        return t.reshape(B, nb, BLOCK, H, d).transpose(1, 0, 3, 2, 4)

    q = blocks(q) * SB_HD ** -0.5
    k, v = blocks(k), blocks(v)
    idx = jnp.arange(BLOCK)
    strict = idx[:, None] > idx[None, :]

    def query_block(args):
        qb, q_blk = args
        q_blk = q_blk.astype(F32)

        def body(i, carry):
            acc, out = carry
            kb = qb - i
            k_blk = lax.dynamic_index_in_dim(k, kb, 0, keepdims=False).astype(F32)
            v_blk = lax.dynamic_index_in_dim(v, kb, 0, keepdims=False).astype(F32)
            z = jnp.einsum('bhtd,bhsd->bhts', q_blk, k_blk)
            valid = jnp.logical_or(kb < qb, strict)
            log_stay = jnp.where(valid, jax.nn.log_sigmoid(-z), 0.0)
            suffix = lax.cumsum(log_stay, axis=3, reverse=True) - log_stay
            log_a = jax.nn.log_sigmoid(z) + suffix + acc[..., None]
            a = jnp.where(valid, jnp.exp(log_a), 0.0)
            out = out + jnp.einsum('bhts,bhse->bhte', a, v_blk)
            acc = acc + jnp.sum(log_stay, axis=3)
            return acc, out

        init = (jnp.zeros((B, H, BLOCK), F32), jnp.zeros((B, H, BLOCK, d), F32))
        _, out = lax.fori_loop(0, qb + 1, body, init)
        return out.astype(h.dtype)

    o = lax.map(query_block, (jnp.arange(nb), q))
    o = o.transpose(1, 0, 3, 2, 4).reshape(B, L, H * d)
    return o @ w_out


def rope_cos_sin(positions):
    half = MLA_ROPE // 2
    inv_freq = ROPE_THETA ** (-jnp.arange(half, dtype=F32) / half)
    ang = positions.astype(F32)[:, :, None, None] * inv_freq
    return jnp.cos(ang), jnp.sin(ang)


def apply_rope(x, cos, sin):
    half = MLA_ROPE // 2
    xf = x.astype(F32)
    x1, x2 = xf[..., :half], xf[..., half:]
    return jnp.concatenate([x1 * cos - x2 * sin, x2 * cos + x1 * sin], axis=-1).astype(x.dtype)


def causal_softmax_attention(q, k, v):
    B, L, H, _ = q.shape
    dv = v.shape[-1]
    nb = L // BLOCK

    def blocks(t):
        return t.reshape(B, nb, BLOCK, H, t.shape[-1]).transpose(1, 0, 3, 2, 4)

    qb_all, kb_all, vb_all = blocks(q), blocks(k), blocks(v)
    idx = jnp.arange(BLOCK)
    causal = idx[:, None] >= idx[None, :]

    def query_block(args):
        qb, q_blk = args
        q_blk = q_blk.astype(F32)

        def body(kb, carry):
            m, l, acc = carry
            k_blk = lax.dynamic_index_in_dim(kb_all, kb, 0, keepdims=False).astype(F32)
            v_blk = lax.dynamic_index_in_dim(vb_all, kb, 0, keepdims=False).astype(F32)
            s = jnp.einsum('bhtd,bhsd->bhts', q_blk, k_blk)
            s = jnp.where(jnp.logical_or(kb < qb, causal), s, -jnp.inf)
            m_new = jnp.maximum(m, jnp.max(s, axis=-1))
            pr = jnp.exp(s - m_new[..., None])
            corr = jnp.exp(m - m_new)
            l = l * corr + jnp.sum(pr, axis=-1)
            acc = acc * corr[..., None] + jnp.einsum('bhts,bhse->bhte', pr, v_blk)
            return m_new, l, acc

        init = (jnp.full((B, H, BLOCK), -jnp.inf, F32), jnp.zeros((B, H, BLOCK), F32),
                jnp.zeros((B, H, BLOCK, dv), F32))
        _, l, acc = lax.fori_loop(0, qb + 1, body, init)
        return (acc / l[..., None]).astype(q.dtype)

    o = lax.map(query_block, (jnp.arange(nb), qb_all))
    return o.transpose(1, 0, 3, 2, 4).reshape(B, L, H, dv)


def mla(h, positions, w_in, q_norm, w_uq, kv_norm, w_ukv, w_out):
    B, L, _ = h.shape
    H = MLA_HEADS
    c_q, c_kv, k_rope = jnp.split(h @ w_in, [MLA_Q_RANK, MLA_Q_RANK + MLA_KV_RANK], axis=-1)
    q = (rmsnorm(c_q, q_norm) @ w_uq).reshape(B, L, H, MLA_NOPE + MLA_ROPE)
    kv = (rmsnorm(c_kv, kv_norm) @ w_ukv).reshape(B, L, H, MLA_NOPE + MLA_V)
    q_nope, q_rope = q[..., :MLA_NOPE], q[..., MLA_NOPE:]
    k_nope, v = kv[..., :MLA_NOPE], kv[..., MLA_NOPE:]
    cos, sin = rope_cos_sin(positions)
    q_rope = apply_rope(q_rope, cos, sin)
    k_rope = apply_rope(k_rope[:, :, None, :], cos, sin)
    q = jnp.concatenate([q_nope, q_rope], axis=-1) * (MLA_NOPE + MLA_ROPE) ** -0.5
    k = jnp.concatenate([k_nope, jnp.broadcast_to(k_rope, (B, L, H, MLA_ROPE))], axis=-1)
    o = causal_softmax_attention(q, k, v)
    return o.reshape(B, L, H * MLA_V) @ w_out


def setup_inputs(seed: int = 0) -> dict:
    key = jax.random.key(seed)
    ks = iter(jax.random.split(key, 48))

    def nrm(shape, fan_in):
        return jax.random.normal(next(ks), shape, F32) * fan_in ** -0.5

    def gain(shape):
        return 1.0 + 0.02 * jax.random.normal(next(ks), shape, F32)

    D, Fd = D_MODEL, D_FF
    x = jax.random.normal(next(ks), (BATCH, SEQ, D), F32)
    p = jax.random.normal(next(ks), (DEPTH, BATCH, SEQ, PLE_DIM), F32)
    offset = jax.random.randint(next(ks), (BATCH, 1), 0, 4096, dtype=jnp.int32)
    positions = offset + jnp.arange(SEQ, dtype=jnp.int32)[None, :]

    dt = jnp.exp(jax.random.uniform(next(ks), (N_A, GDN_HEADS), F32,
                                    minval=np.log(1e-3), maxval=np.log(1e-1)))
    return {
        "x": x, "p": p, "positions": positions,
        "ffn1_norm": gain((DEPTH, D)),
        "ffn1_w_gate": nrm((DEPTH, D, Fd), D),
        "ffn1_w_up": nrm((DEPTH, D, Fd), D),
        "ffn1_w_down": nrm((DEPTH, Fd, D), Fd),
        "mix_norm": gain((DEPTH, D)),
        "ffn2_norm": gain((DEPTH, D)),
        "ffn2_w_gate": nrm((DEPTH, D, Fd), D),
        "ffn2_w_up": nrm((DEPTH, D, Fd), D),
        "ffn2_w_down": nrm((DEPTH, Fd, D), Fd),
        "ple_norm": gain((DEPTH, D)),
        "ple_w_gate": nrm((DEPTH, D, D), D),
        "ple_w_proj": nrm((DEPTH, PLE_DIM, D), PLE_DIM),
        "gdn_w_in": nrm((N_A, D, GDN_IN), D),
        "gdn_conv_w": nrm((N_A, GDN_CONV, GDN_CONV_DIM), GDN_CONV),
        "gdn_a_log": jnp.log(jax.random.uniform(next(ks), (N_A, GDN_HEADS), F32, minval=1.0, maxval=16.0)),
        "gdn_dt_bias": jnp.log(jnp.expm1(dt)),
        "gdn_out_norm": gain((N_A, GDN_DV)),
        "gdn_w_out": nrm((N_A, GDN_V, D), GDN_V),
        "sb_w_qkv": nrm((N_B, D, 3 * SB_HEADS * SB_HD), D),
        "sb_w_out": nrm((N_B, SB_HEADS * SB_HD, D), SB_HEADS * SB_HD),
        "mla_w_in": nrm((N_C, D, MLA_Q_RANK + MLA_KV_RANK + MLA_ROPE), D),
        "mla_q_norm": gain((N_C, MLA_Q_RANK)),
        "mla_w_uq": nrm((N_C, MLA_Q_RANK, MLA_HEADS * (MLA_NOPE + MLA_ROPE)), MLA_Q_RANK),
        "mla_kv_norm": gain((N_C, MLA_KV_RANK)),
        "mla_w_ukv": nrm((N_C, MLA_KV_RANK, MLA_HEADS * (MLA_NOPE + MLA_V)), MLA_KV_RANK),
        "mla_w_out": nrm((N_C, MLA_HEADS * MLA_V, D), MLA_HEADS * MLA_V),
        "final_norm": gain((D,)),
    }


def reference(x, p, positions,
              ffn1_norm, ffn1_w_gate, ffn1_w_up, ffn1_w_down,
              mix_norm,
              ffn2_norm, ffn2_w_gate, ffn2_w_up, ffn2_w_down,
              ple_norm, ple_w_gate, ple_w_proj,
              gdn_w_in, gdn_conv_w, gdn_a_log, gdn_dt_bias, gdn_out_norm, gdn_w_out,
              sb_w_qkv, sb_w_out,
              mla_w_in, mla_q_norm, mla_w_uq, mla_kv_norm, mla_w_ukv, mla_w_out,
              final_norm):
    for i in range(DEPTH):
        mixer, slot = i % N_MIXERS, i // N_MIXERS
        x = x + 0.5 * swiglu(rmsnorm(x, ffn1_norm[i]), ffn1_w_gate[i], ffn1_w_up[i], ffn1_w_down[i])
        hn = rmsnorm(x, mix_norm[i])
        if mixer == 0:
            y = gated_deltanet(hn, gdn_w_in[slot], gdn_conv_w[slot], gdn_a_log[slot],
                               gdn_dt_bias[slot], gdn_out_norm[slot], gdn_w_out[slot])
        elif mixer == 1:
            y = stick_breaking(hn, sb_w_qkv[slot], sb_w_out[slot])
        else:
            y = mla(hn, positions, mla_w_in[slot], mla_q_norm[slot], mla_w_uq[slot],
                    mla_kv_norm[slot], mla_w_ukv[slot], mla_w_out[slot])
        x = x + y
        x = x + 0.5 * swiglu(rmsnorm(x, ffn2_norm[i]), ffn2_w_gate[i], ffn2_w_up[i], ffn2_w_down[i])
        gate = jax.nn.sigmoid(rmsnorm(x, ple_norm[i]) @ ple_w_gate[i])
        x = x + gate * (p[i] @ ple_w_proj[i])
    return rmsnorm(x, final_norm)
```

```python
import functools

import numpy as np
import jax
import jax.numpy as jnp
from jax import lax
from jax.experimental import pallas as pl
from jax.experimental.pallas import tpu as pltpu

F32 = jnp.float32
BF16 = jnp.bfloat16

D_MODEL = 1024
DEPTH = 4
N_MIXERS = 3
PLE_DIM = 256
D_FF = 2816
EPS = 1e-6

GDN_HEADS = 8
GDN_DK = 128
GDN_DV = 128
GDN_CONV = 4
GDN_CHUNK = 64
GDN_QK = GDN_HEADS * GDN_DK
GDN_V = GDN_HEADS * GDN_DV
GDN_CONV_DIM = 2 * GDN_QK + GDN_V

SB_HEADS = 16
SB_HD = 64

MLA_HEADS = 8
MLA_Q_RANK = 384
MLA_KV_RANK = 256
MLA_NOPE = 128
MLA_ROPE = 64
MLA_V = 128
ROPE_THETA = 10000.0

LANES = 128
VMEM_LIMIT = 56 * 1024 * 1024

FFN_TM = 512
FFN_TF = 1408
TOK_TM = 512
SB_TQ = 512
SB_TK = 128
MLA_TQ = 512
MLA_TK = 512


def _cparams(*sem):
    return pltpu.CompilerParams(dimension_semantics=sem, vmem_limit_bytes=VMEM_LIMIT)


def _dot(a, b):
    return jnp.dot(a, b, preferred_element_type=F32)


def _dot_nt(a, b):
    return lax.dot_general(a, b, (((1,), (1,)), ((), ())), preferred_element_type=F32)


def _dot_tn(a, b):
    return lax.dot_general(a, b, (((0,), (0,)), ((), ())), preferred_element_type=F32)


def _rms(xf, g):
    return xf * lax.rsqrt(jnp.mean(xf * xf, axis=-1, keepdims=True) + EPS) * g


def _sigmoid(x):
    return 1.0 / (1.0 + jnp.exp(-x))


def _softplus(x):
    return jnp.maximum(x, 0.0) + jnp.log(1.0 + jnp.exp(-jnp.abs(x)))


def _split_dot(x, m_bf16, terms):
    out = None
    r = x
    for _ in range(terms):
        p = r.astype(BF16)
        d = _dot(p, m_bf16)
        out = d if out is None else out + d
        r = r - p.astype(F32)
    return out


def _ffn_kernel(x_ref, g_ref, wg_ref, wu_ref, wd_ref, o_ref, h_ref, acc_ref):
    f = pl.program_id(1)

    @pl.when(f == 0)
    def _():
        h_ref[...] = _rms(x_ref[...], g_ref[...]).astype(BF16)
        acc_ref[...] = jnp.zeros_like(acc_ref)

    h = h_ref[...]
    gate = _dot(h, wg_ref[...])
    up = _dot(h, wu_ref[...])
    act = (gate * _sigmoid(gate) * up).astype(BF16)
    acc_ref[...] += _dot(act, wd_ref[...])

    @pl.when(f == pl.num_programs(1) - 1)
    def _():
        o_ref[...] = x_ref[...] + 0.5 * acc_ref[...]


def _ffn(x, g, wg, wu, wd):
    t, d = x.shape
    tm, tf = min(FFN_TM, t), FFN_TF
    nf = D_FF // tf
    return pl.pallas_call(
        _ffn_kernel,
        out_shape=jax.ShapeDtypeStruct((t, d), F32),
        grid=(t // tm, nf),
        in_specs=[
            pl.BlockSpec((tm, d), lambda i, f: (i, 0)),
            pl.BlockSpec((1, d), lambda i, f: (0, 0)),
            pl.BlockSpec((d, tf), lambda i, f: (0, f)),
            pl.BlockSpec((d, tf), lambda i, f: (0, f)),
            pl.BlockSpec((tf, d), lambda i, f: (f, 0)),
        ],
        out_specs=pl.BlockSpec((tm, d), lambda i, f: (i, 0)),
        scratch_shapes=[pltpu.VMEM((tm, d), BF16), pltpu.VMEM((tm, d), F32)],
        compiler_params=_cparams("parallel", "arbitrary"),
        name="ffn",
    )(x, g, wg, wu, wd)


def _ple_kernel(x_ref, g_ref, wgate_ref, p_ref, wp_ref, fin_ref, o_ref, *, final):
    x = x_ref[...]
    h = _rms(x, g_ref[...]).astype(BF16)
    gate = _sigmoid(_dot(h, wgate_ref[...]))
    y = x + gate * _dot(p_ref[...].astype(BF16), wp_ref[...])
    if final:
        y = _rms(y, fin_ref[...])
    o_ref[...] = y


def _ple(x, g, wgate, p, wp, fin, final):
    t, d = x.shape
    tm = min(TOK_TM, t)
    return pl.pallas_call(
        functools.partial(_ple_kernel, final=final),
        out_shape=jax.ShapeDtypeStruct((t, d), F32),
        grid=(t // tm,),
        in_specs=[
            pl.BlockSpec((tm, d), lambda i: (i, 0)),
            pl.BlockSpec((1, d), lambda i: (0, 0)),
            pl.BlockSpec((d, d), lambda i: (0, 0)),
            pl.BlockSpec((tm, PLE_DIM), lambda i: (i, 0)),
            pl.BlockSpec((PLE_DIM, d), lambda i: (0, 0)),
            pl.BlockSpec((1, d), lambda i: (0, 0)),
        ],
        out_specs=pl.BlockSpec((tm, d), lambda i: (i, 0)),
        compiler_params=_cparams("parallel"),
        name="ple",
    )(x, g, wgate, p, wp, fin)


def _norm_matmul_kernel(x_ref, g_ref, w_ref, o_ref, h_ref):
    @pl.when(pl.program_id(1) == 0)
    def _():
        h_ref[...] = _rms(x_ref[...], g_ref[...]).astype(BF16)

    o_ref[...] = _dot(h_ref[...], w_ref[...]).astype(o_ref.dtype)


def _norm_matmul(x, g, w, tn, out_dtype):
    t, d = x.shape
    n = w.shape[1]
    tm = min(TOK_TM, t)
    return pl.pallas_call(
        _norm_matmul_kernel,
        out_shape=jax.ShapeDtypeStruct((t, n), out_dtype),
        grid=(t // tm, n // tn),
        in_specs=[
            pl.BlockSpec((tm, d), lambda i, j: (i, 0)),
            pl.BlockSpec((1, d), lambda i, j: (0, 0)),
            pl.BlockSpec((d, tn), lambda i, j: (0, j)),
        ],
        out_specs=pl.BlockSpec((tm, tn), lambda i, j: (i, j)),
        scratch_shapes=[pltpu.VMEM((tm, d), BF16)],
        compiler_params=_cparams("parallel", "arbitrary"),
        name="norm_matmul",
    )(x, g, w)


def _matmul_res_kernel(a_ref, w_ref, x_ref, o_ref):
    o_ref[...] = x_ref[...] + _dot(a_ref[...].astype(BF16), w_ref[...])


def _matmul_res(a, w, x):
    t, k = a.shape
    d = w.shape[1]
    tm = min(TOK_TM, t)
    return pl.pallas_call(
        _matmul_res_kernel,
        out_shape=jax.ShapeDtypeStruct((t, d), F32),
        grid=(t // tm,),
        in_specs=[
            pl.BlockSpec((tm, k), lambda i: (i, 0)),
            pl.BlockSpec((k, d), lambda i: (0, 0)),
            pl.BlockSpec((tm, d), lambda i: (i, 0)),
        ],
        out_specs=pl.BlockSpec((tm, d), lambda i: (i, 0)),
        compiler_params=_cparams("parallel"),
        name="matmul_res",
    )(a, w, x)


def _sb_kernel(q_ref, k_ref, v_ref, u_ref, o_ref, acc_o, *, tq, tk):
    qi = pl.program_id(2)
    q = q_ref[0]
    u2 = u_ref[...]
    lane = lax.broadcasted_iota(jnp.int32, (tk, LANES), 1)
    first = lane < SB_HD
    row_minus_col = (lax.broadcasted_iota(jnp.int32, (tq, 2 * tk), 0)
                     - (lax.broadcasted_iota(jnp.int32, (tq, 2 * tk), 1) & (tk - 1)))
    acc_o[...] = jnp.zeros_like(acc_o)

    def blockdiag(t):
        zero = jnp.zeros_like(t)
        return jnp.concatenate([jnp.where(first, t, zero), jnp.where(first, zero, t)], axis=0)

    def tile(kb, carry, thr):
        acc_a, acc_b = carry
        start = pl.multiple_of(kb * tk, tk)
        kk = blockdiag(k_ref[0, pl.ds(start, tk), :])
        vv = blockdiag(v_ref[0, pl.ds(start, tk), :])
        z = _dot_nt(q, kk)
        sp = _softplus(z)
        log_beta = z - sp
        log_stay = -sp
        if thr is not None:
            valid = row_minus_col > thr
            log_stay = jnp.where(valid, log_stay, 0.0)
        suffix = _split_dot(log_stay, u2, 2)
        log_a = log_beta + suffix + jnp.concatenate(
            [jnp.broadcast_to(acc_a, (tq, tk)), jnp.broadcast_to(acc_b, (tq, tk))], axis=1)
        a = jnp.exp(log_a)
        if thr is not None:
            a = jnp.where(valid, a, 0.0)
        acc_o[...] += _dot(a.astype(BF16), vv)
        acc_a = acc_a + jnp.sum(log_stay[:, :tk], axis=1, keepdims=True)
        acc_b = acc_b + jnp.sum(log_stay[:, tk:], axis=1, keepdims=True)
        return acc_a, acc_b

    carry = (jnp.zeros((tq, 1), F32), jnp.zeros((tq, 1), F32))
    ratio = tq // tk
    for j in range(ratio - 1, -1, -1):
        carry = tile(qi * ratio + j, carry, j * tk)
    n_full = qi * ratio

    def body(i, c):
        return tile(n_full - 1 - i, c, None)

    lax.fori_loop(0, n_full, body, carry)
    o_ref[0] = acc_o[...].astype(o_ref.dtype)


def _sb_attention(qkv, u2):
    b, l, _ = qkv.shape
    tq, tk = min(SB_TQ, l), SB_TK
    npair = SB_HEADS * SB_HD // LANES
    return pl.pallas_call(
        functools.partial(_sb_kernel, tq=tq, tk=tk),
        out_shape=jax.ShapeDtypeStruct((b, l, SB_HEADS * SB_HD), BF16),
        grid=(b, npair, l // tq),
        in_specs=[
            pl.BlockSpec((1, tq, LANES), lambda bi, j, qi: (bi, qi, j)),
            pl.BlockSpec((1, l, LANES), lambda bi, j, qi: (bi, 0, npair + j)),
            pl.BlockSpec((1, l, LANES), lambda bi, j, qi: (bi, 0, 2 * npair + j)),
            pl.BlockSpec((2 * tk, 2 * tk), lambda bi, j, qi: (0, 0)),
        ],
        out_specs=pl.BlockSpec((1, tq, LANES), lambda bi, j, qi: (bi, qi, j)),
        scratch_shapes=[pltpu.VMEM((tq, LANES), F32)],
        compiler_params=_cparams("parallel", "parallel", "arbitrary"),
        name="sb_attention",
    )(qkv, qkv, qkv, u2)


def _sb_suffix_matrix(tk):
    s = np.arange(2 * tk)
    same = (s[:, None] // tk) == (s[None, :] // tk)
    return jnp.asarray((same & (s[:, None] > s[None, :])).astype(np.float32), BF16)


def _mla_proj_kernel(x_ref, g_ref, wq_ref, wkv_ref, wkr_ref, qn_ref, wuq_ref, kvn_ref, wukv_ref,
                     pos_ref, invf_ref, q_out, kv_out, kr_out):
    h = _rms(x_ref[...], g_ref[...]).astype(BF16)
    cq = _dot(h, wq_ref[...])
    ckv = _dot(h, wkv_ref[...])
    kr = _dot(h, wkr_ref[...])
    q = _dot(_rms(cq, qn_ref[...]).astype(BF16), wuq_ref[...])
    kv = _dot(_rms(ckv, kvn_ref[...]).astype(BF16), wukv_ref[...])
    kv_out[...] = kv.astype(kv_out.dtype)

    ang = pos_ref[...].astype(F32) * invf_ref[...]
    cos, sin = jnp.cos(ang), jnp.sin(ang)
    lane = lax.broadcasted_iota(jnp.int32, ang.shape, 1)
    half = MLA_ROPE // 2
    sin_lo = jnp.where(lane < half, -sin, 0.0)
    sin_hi = jnp.where((lane >= half) & (lane < MLA_ROPE), sin, 0.0)

    def rope(t):
        return (t * cos + pltpu.roll(t, LANES - half, 1) * sin_lo + pltpu.roll(t, half, 1) * sin_hi)

    kr_out[...] = rope(kr).astype(kr_out.dtype)
    scale = (MLA_NOPE + MLA_ROPE) ** -0.5
    for hh in range(MLA_HEADS):
        base = hh * 2 * LANES
        q_out[:, base:base + LANES] = (q[:, base:base + LANES] * scale).astype(q_out.dtype)
        q_out[:, base + LANES:base + 2 * LANES] = (
            rope(q[:, base + LANES:base + 2 * LANES]) * scale).astype(q_out.dtype)


def _mla_proj(x, g, wq, wkv, wkr, qn, wuq, kvn, wukv, pos, invf):
    t, d = x.shape
    tm = min(TOK_TM, t)
    full = lambda a: pl.BlockSpec(a.shape, lambda i: (0, 0))
    nq = MLA_HEADS * 2 * LANES
    return pl.pallas_call(
        _mla_proj_kernel,
        out_shape=(jax.ShapeDtypeStruct((t, nq), BF16), jax.ShapeDtypeStruct((t, nq), BF16),
                   jax.ShapeDtypeStruct((t, LANES), BF16)),
        grid=(t // tm,),
        in_specs=[pl.BlockSpec((tm, d), lambda i: (i, 0)), full(g), full(wq), full(wkv), full(wkr),
                  full(qn), full(wuq), full(kvn), full(wukv),
                  pl.BlockSpec((tm, 1), lambda i: (i, 0)), full(invf)],
        out_specs=(pl.BlockSpec((tm, nq), lambda i: (i, 0)), pl.BlockSpec((tm, nq), lambda i: (i, 0)),
                   pl.BlockSpec((tm, LANES), lambda i: (i, 0))),
        compiler_params=_cparams("parallel"),
        name="mla_proj",
    )(x, g, wq, wkv, wkr, qn, wuq, kvn, wukv, pos, invf)


def _mla_attn_kernel(q_ref, kv_ref, kr_ref, o_ref, m_s, l_s, acc_s, *, tq, tk):
    qi = pl.program_id(2)
    q = q_ref[0]
    m_s[...] = jnp.full_like(m_s, -jnp.inf)
    l_s[...] = jnp.zeros_like(l_s)
    acc_s[...] = jnp.zeros_like(acc_s)
    row_minus_col = (lax.broadcasted_iota(jnp.int32, (tq, tk), 0)
                     - lax.broadcasted_iota(jnp.int32, (tq, tk), 1))

    def tile(kb, thr):
        start = pl.multiple_of(kb * tk, tk)
        kc = jnp.concatenate([kv_ref[0, pl.ds(start, tk), :LANES], kr_ref[0, pl.ds(start, tk), :]], axis=1)
        v = kv_ref[0, pl.ds(start, tk), LANES:]
        s = _dot_nt(q, kc)
        if thr is not None:
            s = jnp.where(row_minus_col >= thr, s, -jnp.inf)
        m_prev = m_s[...]
        m_new = jnp.maximum(m_prev, jnp.max(s, axis=1, keepdims=True))
        p = jnp.exp(s - m_new)
        corr = jnp.exp(m_prev - m_new)
        l_s[...] = l_s[...] * corr + jnp.sum(p, axis=1, keepdims=True)
        acc_s[...] = acc_s[...] * corr + _dot(p.astype(BF16), v)
        m_s[...] = m_new

    ratio = tq // tk
    n_full = qi * ratio

    def body(i, c):
        tile(i, None)
        return c

    lax.fori_loop(0, n_full, body, 0)
    for j in range(ratio):
        tile(n_full + j, j * tk)
    o_ref[0] = (acc_s[...] / l_s[...]).astype(o_ref.dtype)


def _mla_attention(qc, kv, kr):
    b, l, _ = qc.shape
    tq, tk = min(MLA_TQ, l), min(MLA_TK, l)
    return pl.pallas_call(
        functools.partial(_mla_attn_kernel, tq=tq, tk=tk),
        out_shape=jax.ShapeDtypeStruct((b, l, MLA_HEADS * MLA_V), BF16),
        grid=(b, MLA_HEADS, l // tq),
        in_specs=[
            pl.BlockSpec((1, tq, 2 * LANES), lambda bi, hh, qi: (bi, qi, hh)),
            pl.BlockSpec((1, l, 2 * LANES), lambda bi, hh, qi: (bi, 0, hh)),
            pl.BlockSpec((1, l, LANES), lambda bi, hh, qi: (bi, 0, 0)),
        ],
        out_specs=pl.BlockSpec((1, tq, LANES), lambda bi, hh, qi: (bi, qi, hh)),
        scratch_shapes=[pltpu.VMEM((tq, 1), F32), pltpu.VMEM((tq, 1), F32), pltpu.VMEM((tq, MLA_V), F32)],
        compiler_params=_cparams("parallel", "parallel", "arbitrary"),
        name="mla_attention",
    )(qc, kv, kr)


def _gdn_in_kernel(x_ref, g_ref, wqkv_ref, wz_ref, wba_ref, cw_ref, alog_ref, dtb_ref, tri_ref,
                   q_out, k_out, v_out, z_out, bg_out, pre_s, *, tm, tiles_per_seq):
    i = pl.program_id(0)
    h = _rms(x_ref[...], g_ref[...]).astype(BF16)
    pre = _dot(h, wqkv_ref[...])

    @pl.when(i % tiles_per_seq == 0)
    def _():
        pre_s[0:8, :] = jnp.zeros((8, GDN_CONV_DIM), F32)

    pre_s[8:8 + tm, :] = pre
    cw = cw_ref[...]
    conv = pre * cw[3:4, :]
    for j in range(1, GDN_CONV):
        conv = conv + pre_s[8 - j:8 - j + tm, :] * cw[3 - j:4 - j, :]
    pre_s[0:8, :] = pre_s[tm:tm + 8, :]
    y = conv * _sigmoid(conv)

    for hh in range(GDN_HEADS):
        sl = slice(hh * GDN_DK, (hh + 1) * GDN_DK)
        qh = y[:, sl]
        q_out[:, sl] = qh * lax.rsqrt(jnp.sum(qh * qh, axis=1, keepdims=True) + EPS) * GDN_DK ** -0.5
        kh = y[:, GDN_QK + hh * GDN_DK:GDN_QK + (hh + 1) * GDN_DK]
        k_out[:, sl] = kh * lax.rsqrt(jnp.sum(kh * kh, axis=1, keepdims=True) + EPS)
    v_out[...] = y[:, 2 * GDN_QK:]

    z = _dot(h, wz_ref[...])
    z_out[...] = z * _sigmoid(z)

    ba = _dot(h, wba_ref[...])
    lane = lax.broadcasted_iota(jnp.int32, ba.shape, 1)
    beta = _sigmoid(ba)
    gval = -jnp.exp(alog_ref[...]) * _softplus(ba + dtb_ref[...])
    gcum = _split_dot_left(tri_ref[...], gval, 3)
    bg_out[...] = jnp.where(lane < GDN_HEADS, beta, gcum)


def _split_dot_left(m_bf16, x, terms):
    out = None
    r = x
    for _ in range(terms):
        p = r.astype(BF16)
        d = _dot(m_bf16, p)
        out = d if out is None else out + d
        r = r - p.astype(F32)
    return out


def _gdn_in(x, g, wqkv, wz, wba, cw, alog, dtb, tri, seq):
    t, d = x.shape
    tm = min(TOK_TM, seq)
    full = lambda a: pl.BlockSpec(a.shape, lambda i: (0, 0))
    row = lambda n: pl.BlockSpec((tm, n), lambda i: (i, 0))
    return pl.pallas_call(
        functools.partial(_gdn_in_kernel, tm=tm, tiles_per_seq=seq // tm),
        out_shape=(jax.ShapeDtypeStruct((t, GDN_QK), F32), jax.ShapeDtypeStruct((t, GDN_QK), F32),
                   jax.ShapeDtypeStruct((t, GDN_V), F32), jax.ShapeDtypeStruct((t, GDN_V), F32),
                   jax.ShapeDtypeStruct((t, LANES), F32)),
        grid=(t // tm,),
        in_specs=[row(d), full(g), full(wqkv), full(wz), full(wba), full(cw), full(alog), full(dtb), full(tri)],
        out_specs=(row(GDN_QK), row(GDN_QK), row(GDN_V), row(GDN_V), row(LANES)),
        scratch_shapes=[pltpu.VMEM((tm + 8, GDN_CONV_DIM), F32)],
        compiler_params=_cparams("arbitrary"),
        name="gdn_in",
    )(x, g, wqkv, wz, wba, cw, alog, dtb, tri)


def _gdn_chunk_kernel(q_ref, k_ref, v_ref, bg_ref, bgt_ref, o_ref, s_ref):
    c = GDN_CHUNK

    @pl.when(pl.program_id(1) == 0)
    def _():
        s_ref[...] = jnp.zeros_like(s_ref)

    r = lax.broadcasted_iota(jnp.int32, (c, c), 0)
    s_ = lax.broadcasted_iota(jnp.int32, (c, c), 1)
    incl = r >= s_
    strict = r > s_
    for hh in range(GDN_HEADS):
        sl = slice(hh * GDN_DK, (hh + 1) * GDN_DK)
        qh, kh, vh = q_ref[0, :, sl], k_ref[0, :, sl], v_ref[0, :, sl]
        beta = bg_ref[0, :, hh:hh + 1]
        gc = bg_ref[0, :, GDN_HEADS + hh:GDN_HEADS + hh + 1]
        gr = bgt_ref[0, 0, GDN_HEADS + hh:GDN_HEADS + hh + 1, :]
        g_last = gc[c - 1:c, :]
        decay = jnp.where(incl, jnp.exp(jnp.where(incl, gc - gr, 0.0)), 0.0)
        kb = kh * beta
        kh16 = kh.astype(BF16)
        a = jnp.where(strict, _dot_nt(kb.astype(BF16), kh16) * decay, 0.0)
        attn = _dot_nt(qh.astype(BF16), kh16) * decay
        eg = jnp.exp(gc)
        y = jnp.concatenate([vh * beta, kb * eg], axis=1)
        p = a.astype(BF16)
        y = y - _dot(p, y.astype(BF16))
        step = 2
        while step < c:
            a = _dot(p, p)
            p = a.astype(BF16)
            y = y + _dot(p, y.astype(BF16))
            step *= 2
        u, w = y[:, :GDN_DV], y[:, GDN_DV:]
        s = s_ref[hh]
        s16 = s.astype(BF16)
        v_new = u - _dot(w.astype(BF16), s16)
        v16 = v_new.astype(BF16)
        o_ref[0, :, sl] = _dot((qh * eg).astype(BF16), s16) + _dot(attn.astype(BF16), v16)
        kd = kh * jnp.exp(g_last - gc)
        s_ref[hh] = s * jnp.exp(g_last) + _dot_tn(kd.astype(BF16), v16)


def _gdn_chunks(q, k, v, bg, bgt):
    b, l, n = q.shape
    c = GDN_CHUNK
    blk = lambda w: pl.BlockSpec((1, c, w), lambda bi, ci: (bi, ci, 0))
    return pl.pallas_call(
        _gdn_chunk_kernel,
        out_shape=jax.ShapeDtypeStruct((b, l, n), F32),
        grid=(b, l // c),
        in_specs=[blk(n), blk(n), blk(n), blk(LANES),
                  pl.BlockSpec((1, 1, 2 * GDN_HEADS, c), lambda bi, ci: (bi, ci, 0, 0))],
        out_specs=blk(n),
        scratch_shapes=[pltpu.VMEM((GDN_HEADS, GDN_DK, GDN_DV), F32)],
        compiler_params=_cparams("parallel", "arbitrary"),
        name="gdn_chunks",
    )(q, k, v, bg, bgt)


def _gdn_out_kernel(o_ref, z_ref, gn_ref, w_ref, x_ref, out_ref, a_s):
    gn = gn_ref[...]
    for hh in range(GDN_HEADS):
        sl = slice(hh * GDN_DV, (hh + 1) * GDN_DV)
        a_s[:, sl] = (_rms(o_ref[:, sl], gn) * z_ref[:, sl]).astype(BF16)
    out_ref[...] = x_ref[...] + _dot(a_s[...], w_ref[...])


def _gdn_out(o, zs, gn, w, x):
    t, d = x.shape
    tm = min(TOK_TM, t)
    row = lambda n: pl.BlockSpec((tm, n), lambda i: (i, 0))
    full = lambda a: pl.BlockSpec(a.shape, lambda i: (0, 0))
    return pl.pallas_call(
        _gdn_out_kernel,
        out_shape=jax.ShapeDtypeStruct((t, d), F32),
        grid=(t // tm,),
        in_specs=[row(GDN_V), row(GDN_V), full(gn), full(w), row(d)],
        out_specs=row(d),
        scratch_shapes=[pltpu.VMEM((tm, GDN_V), BF16)],
        compiler_params=_cparams("parallel"),
        name="gdn_out",
    )(o, zs, gn, w, x)


def _chunk_tri(tm):
    t = np.arange(tm)
    same = (t[:, None] // GDN_CHUNK) == (t[None, :] // GDN_CHUNK)
    return jnp.asarray((same & (t[:, None] >= t[None, :])).astype(np.float32), BF16)


def _row(v):
    return v.reshape(1, -1).astype(F32)


def _pad_lanes(v, offset):
    out = jnp.zeros((1, LANES), F32)
    return out.at[0, offset:offset + v.shape[0]].set(v.astype(F32))


def _gated_deltanet(x, norm_g, w_in, conv_w, a_log, dt_bias, out_norm, w_out, batch, seq):
    t = x.shape[0]
    wqkv = w_in[:, :GDN_CONV_DIM].astype(BF16)
    wz = w_in[:, GDN_CONV_DIM:GDN_CONV_DIM + GDN_V].astype(BF16)
    wba = jnp.pad(w_in[:, GDN_CONV_DIM + GDN_V:], ((0, 0), (0, LANES - 2 * GDN_HEADS))).astype(BF16)
    tm = min(TOK_TM, seq)
    q, k, v, zs, bg = _gdn_in(x, _row(norm_g), wqkv, wz, wba, conv_w.astype(F32),
                              _pad_lanes(a_log, GDN_HEADS), _pad_lanes(dt_bias, GDN_HEADS),
                              _chunk_tri(tm), seq)
    nc = seq // GDN_CHUNK
    bgt = bg[:, :2 * GDN_HEADS].reshape(batch, nc, GDN_CHUNK, 2 * GDN_HEADS).transpose(0, 1, 3, 2)
    shp = (batch, seq, GDN_QK)
    o = _gdn_chunks(q.reshape(shp), k.reshape(shp), v.reshape(shp), bg.reshape(batch, seq, LANES), bgt)
    return _gdn_out(o.reshape(t, GDN_V), zs, _row(out_norm), w_out.astype(BF16), x)


def _stick_breaking(x, norm_g, w_qkv, w_out, batch, seq):
    t = x.shape[0]
    n = SB_HEADS * SB_HD
    w = jnp.concatenate([w_qkv[:, :n] * SB_HD ** -0.5, w_qkv[:, n:]], axis=1).astype(BF16)
    qkv = _norm_matmul(x, _row(norm_g), w, 1024, BF16)
    o = _sb_attention(qkv.reshape(batch, seq, 3 * n), _sb_suffix_matrix(SB_TK))
    return _matmul_res(o.reshape(t, n), w_out.astype(BF16), x)


def _mla(x, norm_g, positions, w_in, q_norm, w_uq, kv_norm, w_ukv, w_out, batch, seq):
    t = x.shape[0]
    wq = w_in[:, :MLA_Q_RANK].astype(BF16)
    wkv = w_in[:, MLA_Q_RANK:MLA_Q_RANK + MLA_KV_RANK].astype(BF16)
    wkr = jnp.pad(w_in[:, MLA_Q_RANK + MLA_KV_RANK:], ((0, 0), (0, LANES - MLA_ROPE))).astype(BF16)
    qd = MLA_NOPE + MLA_ROPE
    wuq = jnp.pad(w_uq.reshape(MLA_Q_RANK, MLA_HEADS, qd), ((0, 0), (0, 0), (0, 2 * LANES - qd)))
    wuq = wuq.reshape(MLA_Q_RANK, MLA_HEADS * 2 * LANES).astype(BF16)
    half = MLA_ROPE // 2
    inv_freq = ROPE_THETA ** (-jnp.arange(half, dtype=F32) / half)
    invf = jnp.zeros((1, LANES), F32).at[0, :MLA_ROPE].set(jnp.concatenate([inv_freq, inv_freq]))
    qc, kv, kr = _mla_proj(x, _row(norm_g), wq, wkv, wkr, _row(q_norm), wuq, _row(kv_norm),
                           w_ukv.astype(BF16), positions.reshape(t, 1), invf)
    n = MLA_HEADS * 2 * LANES
    o = _mla_attention(qc.reshape(batch, seq, n), kv.reshape(batch, seq, n), kr.reshape(batch, seq, LANES))
    return _matmul_res(o.reshape(t, MLA_HEADS * MLA_V), w_out.astype(BF16), x)


def kernel(x, p, positions, ffn1_norm, ffn1_w_gate, ffn1_w_up, ffn1_w_down, mix_norm, ffn2_norm, ffn2_w_gate, ffn2_w_up, ffn2_w_down, ple_norm, ple_w_gate, ple_w_proj, gdn_w_in, gdn_conv_w, gdn_a_log, gdn_dt_bias, gdn_out_norm, gdn_w_out, sb_w_qkv, sb_w_out, mla_w_in, mla_q_norm, mla_w_uq, mla_kv_norm, mla_w_ukv, mla_w_out, final_norm):
    batch, seq, d = x.shape
    t = batch * seq
    depth = ffn1_norm.shape[0]
    x = x.reshape(t, d)
    p = p.reshape(depth, t, PLE_DIM)
    for i in range(depth):
        mixer, slot = i % N_MIXERS, i // N_MIXERS
        x = _ffn(x, _row(ffn1_norm[i]), ffn1_w_gate[i].astype(BF16), ffn1_w_up[i].astype(BF16),
                 ffn1_w_down[i].astype(BF16))
        if mixer == 0:
            x = _gated_deltanet(x, mix_norm[i], gdn_w_in[slot], gdn_conv_w[slot], gdn_a_log[slot],
                                gdn_dt_bias[slot], gdn_out_norm[slot], gdn_w_out[slot], batch, seq)
        elif mixer == 1:
            x = _stick_breaking(x, mix_norm[i], sb_w_qkv[slot], sb_w_out[slot], batch, seq)
        else:
            x = _mla(x, mix_norm[i], positions, mla_w_in[slot], mla_q_norm[slot], mla_w_uq[slot],
                     mla_kv_norm[slot], mla_w_ukv[slot], mla_w_out[slot], batch, seq)
        x = _ffn(x, _row(ffn2_norm[i]), ffn2_w_gate[i].astype(BF16), ffn2_w_up[i].astype(BF16),
                 ffn2_w_down[i].astype(BF16))
        x = _ple(x, _row(ple_norm[i]), ple_w_gate[i].astype(BF16), p[i], ple_w_proj[i].astype(BF16),
                 _row(final_norm), final=(i == depth - 1))
    return x.reshape(batch, seq, d)
```

```python
import functools

import numpy as np
import jax
import jax.numpy as jnp
from jax import lax
from jax.experimental import pallas as pl
from jax.experimental.pallas import tpu as pltpu

F32 = jnp.float32
BF16 = jnp.bfloat16

D_MODEL = 1024
DEPTH = 4
N_MIXERS = 3
PLE_DIM = 256
D_FF = 2816
EPS = 1e-6

GDN_HEADS = 8
GDN_DK = 128
GDN_DV = 128
GDN_CONV = 4
GDN_CHUNK = 64
GDN_QK = GDN_HEADS * GDN_DK
GDN_V = GDN_HEADS * GDN_DV
GDN_CONV_DIM = 2 * GDN_QK + GDN_V

SB_HEADS = 16
SB_HD = 64

MLA_HEADS = 8
MLA_Q_RANK = 384
MLA_KV_RANK = 256
MLA_NOPE = 128
MLA_ROPE = 64
MLA_V = 128
ROPE_THETA = 10000.0

LANES = 128
VMEM_LIMIT = 56 * 1024 * 1024

FFN_TM = 512
FFN_TF = 1408
TOK_TM = 512
SB_TQ = 512
SB_TK = 128
MLA_TQ = 512
MLA_TK = 256

LOG2E = 1.4426950408889634
NEG_BIG = -1e30
SB_SPLIT_TERMS = 1


def _cparams(*sem):
    return pltpu.CompilerParams(dimension_semantics=sem, vmem_limit_bytes=VMEM_LIMIT)


def _dot(a, b):
    return jnp.dot(a, b, preferred_element_type=F32)


def _dot_nt(a, b):
    return lax.dot_general(a, b, (((1,), (1,)), ((), ())), preferred_element_type=F32)


def _dot_tn(a, b):
    return lax.dot_general(a, b, (((0,), (0,)), ((), ())), preferred_element_type=F32)


def _rms(xf, g):
    return xf * lax.rsqrt(jnp.mean(xf * xf, axis=-1, keepdims=True) + EPS) * g


def _sigmoid(x):
    return 1.0 / (1.0 + jnp.exp(-x))


def _softplus(x):
    return jnp.maximum(x, 0.0) + jnp.log(1.0 + jnp.exp(-jnp.abs(x)))


def _split_dot(x, m_bf16, terms):
    out = None
    r = x
    for i in range(terms):
        p = r.astype(BF16)
        d = _dot(p, m_bf16)
        out = d if out is None else out + d
        if i + 1 < terms:
            r = r - p.astype(F32)
    return out


def _ffn_kernel(x_ref, g_ref, wg_ref, wu_ref, wd_ref, o_ref, h_ref, acc_ref):
    f = pl.program_id(1)

    @pl.when(f == 0)
    def _():
        h_ref[...] = _rms(x_ref[...], g_ref[...]).astype(BF16)
        acc_ref[...] = jnp.zeros_like(acc_ref)

    h = h_ref[...]
    gate = _dot(h, wg_ref[...])
    up = _dot(h, wu_ref[...])
    act = (gate * _sigmoid(gate) * up).astype(BF16)
    acc_ref[...] += _dot(act, wd_ref[...])

    @pl.when(f == pl.num_programs(1) - 1)
    def _():
        o_ref[...] = x_ref[...] + 0.5 * acc_ref[...]


def _ffn(x, g, wg, wu, wd):
    t, d = x.shape
    tm, tf = min(FFN_TM, t), FFN_TF
    nf = D_FF // tf
    return pl.pallas_call(
        _ffn_kernel,
        out_shape=jax.ShapeDtypeStruct((t, d), F32),
        grid=(t // tm, nf),
        in_specs=[
            pl.BlockSpec((tm, d), lambda i, f: (i, 0)),
            pl.BlockSpec((1, d), lambda i, f: (0, 0)),
            pl.BlockSpec((d, tf), lambda i, f: (0, f)),
            pl.BlockSpec((d, tf), lambda i, f: (0, f)),
            pl.BlockSpec((tf, d), lambda i, f: (f, 0)),
        ],
        out_specs=pl.BlockSpec((tm, d), lambda i, f: (i, 0)),
        scratch_shapes=[pltpu.VMEM((tm, d), BF16), pltpu.VMEM((tm, d), F32)],
        compiler_params=_cparams("parallel", "arbitrary"),
        name="ffn",
    )(x, g, wg, wu, wd)


def _ple_kernel(x_ref, g_ref, wgate_ref, p_ref, wp_ref, fin_ref, o_ref, *, final):
    x = x_ref[...]
    h = _rms(x, g_ref[...]).astype(BF16)
    gate = _sigmoid(_dot(h, wgate_ref[...]))
    y = x + gate * _dot(p_ref[...].astype(BF16), wp_ref[...])
    if final:
        y = _rms(y, fin_ref[...])
    o_ref[...] = y


def _ple(x, g, wgate, p, wp, fin, final):
    t, d = x.shape
    tm = min(TOK_TM, t)
    return pl.pallas_call(
        functools.partial(_ple_kernel, final=final),
        out_shape=jax.ShapeDtypeStruct((t, d), F32),
        grid=(t // tm,),
        in_specs=[
            pl.BlockSpec((tm, d), lambda i: (i, 0)),
            pl.BlockSpec((1, d), lambda i: (0, 0)),
            pl.BlockSpec((d, d), lambda i: (0, 0)),
            pl.BlockSpec((tm, PLE_DIM), lambda i: (i, 0)),
            pl.BlockSpec((PLE_DIM, d), lambda i: (0, 0)),
            pl.BlockSpec((1, d), lambda i: (0, 0)),
        ],
        out_specs=pl.BlockSpec((tm, d), lambda i: (i, 0)),
        compiler_params=_cparams("parallel"),
        name="ple",
    )(x, g, wgate, p, wp, fin)


def _norm_matmul_kernel(x_ref, g_ref, w_ref, o_ref, h_ref):
    @pl.when(pl.program_id(1) == 0)
    def _():
        h_ref[...] = _rms(x_ref[...], g_ref[...]).astype(BF16)

    o_ref[...] = _dot(h_ref[...], w_ref[...]).astype(o_ref.dtype)


def _norm_matmul(x, g, w, tn, out_dtype):
    t, d = x.shape
    n = w.shape[1]
    tm = min(TOK_TM, t)
    return pl.pallas_call(
        _norm_matmul_kernel,
        out_shape=jax.ShapeDtypeStruct((t, n), out_dtype),
        grid=(t // tm, n // tn),
        in_specs=[
            pl.BlockSpec((tm, d), lambda i, j: (i, 0)),
            pl.BlockSpec((1, d), lambda i, j: (0, 0)),
            pl.BlockSpec((d, tn), lambda i, j: (0, j)),
        ],
        out_specs=pl.BlockSpec((tm, tn), lambda i, j: (i, j)),
        scratch_shapes=[pltpu.VMEM((tm, d), BF16)],
        compiler_params=_cparams("parallel", "arbitrary"),
        name="norm_matmul",
    )(x, g, w)


def _matmul_res_kernel(a_ref, w_ref, x_ref, o_ref):
    o_ref[...] = x_ref[...] + _dot(a_ref[...].astype(BF16), w_ref[...])


def _matmul_res(a, w, x):
    t, k = a.shape
    d = w.shape[1]
    tm = min(TOK_TM, t)
    return pl.pallas_call(
        _matmul_res_kernel,
        out_shape=jax.ShapeDtypeStruct((t, d), F32),
        grid=(t // tm,),
        in_specs=[
            pl.BlockSpec((tm, k), lambda i: (i, 0)),
            pl.BlockSpec((k, d), lambda i: (0, 0)),
            pl.BlockSpec((tm, d), lambda i: (i, 0)),
        ],
        out_specs=pl.BlockSpec((tm, d), lambda i: (i, 0)),
        compiler_params=_cparams("parallel"),
        name="matmul_res",
    )(a, w, x)


def _sb_kernel(q_ref, k_ref, v_ref, u_ref, o_ref,
               z0, z1, lb0, lb1, sf0, sf1, a0, a1, run_a, run_b, acc_o, *, tq, tk):
    qi = pl.program_id(2)
    ratio = tq // tk
    n_tiles = ratio * (qi + 1)
    q = q_ref[0]
    u2 = u_ref[...]
    zb, lbb, sfb, ab = (z0, z1), (lb0, lb1), (sf0, sf1), (a0, a1)
    first = lax.broadcasted_iota(jnp.int32, (tk, LANES), 1) < SB_HD
    row_minus_col = (lax.broadcasted_iota(jnp.int32, (tq, 2 * tk), 0)
                     - (lax.broadcasted_iota(jnp.int32, (tq, 2 * tk), 1) & (tk - 1)))
    acc_o[...] = jnp.zeros_like(acc_o)
    run_a[...] = jnp.zeros_like(run_a)
    run_b[...] = jnp.zeros_like(run_b)

    def blockdiag(t):
        zero = jnp.zeros_like(t)
        return jnp.concatenate([jnp.where(first, t, zero), jnp.where(first, zero, t)], axis=0)

    def key_start(t):
        kb = jnp.maximum(ratio * qi + ratio - 1 - t, 0)
        return pl.multiple_of(kb * tk, tk)

    def stage_q(t, par):
        kk = blockdiag(k_ref[0, pl.ds(key_start(t), tk), :])
        zb[par][...] = _dot_nt(q, kk)

    def stage_s1(par, thr):
        z = zb[par][...]
        neg_abs = lax.bitcast_convert_type(
            lax.bitcast_convert_type(z, jnp.uint32) | jnp.uint32(0x80000000), F32)
        sp = jnp.maximum(z, 0.0) + jnp.log(1.0 + jnp.exp2(neg_abs)) * LOG2E
        ra, rb = run_a[...], run_b[...]
        lb = z - sp - jnp.concatenate([ra, rb], axis=1)
        if thr is not None:
            valid = row_minus_col > thr
            sp = jnp.where(valid, sp, 0.0)
            lb = jnp.where(valid, lb, NEG_BIG)
        lbb[par][...] = lb
        sfb[par][...] = _split_dot(sp, u2, SB_SPLIT_TERMS)
        run_a[...] = ra + jnp.sum(sp[:, :tk], axis=1, keepdims=True)
        run_b[...] = rb + jnp.sum(sp[:, tk:], axis=1, keepdims=True)

    def stage_s2(par):
        ab[par][...] = jnp.exp2(lbb[par][...] - sfb[par][...]).astype(BF16)

    def stage_p(t, par):
        vv = blockdiag(v_ref[0, pl.ds(key_start(t), tk), :])
        acc_o[...] += _dot(ab[par][...], vv)

    def step(u, par, thr=None, q_=True, s1=True, s2=True, p_=True):
        if q_:
            stage_q(u, par)
        if s1:
            stage_s1(1 - par, thr)
        if s2:
            stage_s2(par)
        if p_:
            stage_p(u - 3, 1 - par)

    for u in range(ratio + 1):
        step(u, u % 2, thr=(ratio - u) * tk, s1=u >= 1, s2=u >= 2, p_=u >= 3)

    def body(j, c):
        u = ratio + 1 + 2 * j
        step(u, (ratio + 1) % 2)
        step(u + 1, ratio % 2)
        return c

    lax.fori_loop(0, (n_tiles - ratio) // 2, body, 0)
    step(n_tiles + 1, (ratio + 1) % 2, q_=False, s1=False)
    step(n_tiles + 2, ratio % 2, q_=False, s1=False, s2=False)
    o_ref[0] = acc_o[...].astype(o_ref.dtype)


def _sb_attention(qkv, u2):
    b, l, _ = qkv.shape
    tq, tk = min(SB_TQ, l), SB_TK
    assert (tq // tk) % 2 == 0 and tk == LANES
    npair = SB_HEADS * SB_HD // LANES
    wide = pltpu.VMEM((tq, 2 * tk), F32)
    col = pltpu.VMEM((tq, LANES), F32)
    return pl.pallas_call(
        functools.partial(_sb_kernel, tq=tq, tk=tk),
        out_shape=jax.ShapeDtypeStruct((b, l, SB_HEADS * SB_HD), BF16),
        grid=(b, npair, l // tq),
        in_specs=[
            pl.BlockSpec((1, tq, LANES), lambda bi, j, qi: (bi, qi, j)),
            pl.BlockSpec((1, l, LANES), lambda bi, j, qi: (bi, 0, npair + j)),
            pl.BlockSpec((1, l, LANES), lambda bi, j, qi: (bi, 0, 2 * npair + j)),
            pl.BlockSpec((2 * tk, 2 * tk), lambda bi, j, qi: (0, 0)),
        ],
        out_specs=pl.BlockSpec((1, tq, LANES), lambda bi, j, qi: (bi, qi, j)),
        scratch_shapes=[wide, wide, wide, wide, wide, wide,
                        pltpu.VMEM((tq, 2 * tk), BF16), pltpu.VMEM((tq, 2 * tk), BF16),
                        col, col, pltpu.VMEM((tq, LANES), F32)],
        compiler_params=_cparams("parallel", "parallel", "arbitrary"),
        name="sb_attention",
    )(qkv, qkv, qkv, u2)


def _sb_suffix_matrix(tk):
    s = np.arange(2 * tk)
    same = (s[:, None] // tk) == (s[None, :] // tk)
    return jnp.asarray((same & (s[:, None] > s[None, :])).astype(np.float32), BF16)


def _mla_proj_kernel(x_ref, g_ref, wq_ref, wkv_ref, wkr_ref, qn_ref, wuq_ref, kvn_ref, wukv_ref,
                     pos_ref, invf_ref, q_out, kv_out, kr_out):
    h = _rms(x_ref[...], g_ref[...]).astype(BF16)
    cq = _dot(h, wq_ref[...])
    ckv = _dot(h, wkv_ref[...])
    kr = _dot(h, wkr_ref[...])
    q = _dot(_rms(cq, qn_ref[...]).astype(BF16), wuq_ref[...])
    kv = _dot(_rms(ckv, kvn_ref[...]).astype(BF16), wukv_ref[...])
    kv_out[...] = kv.astype(kv_out.dtype)

    ang = pos_ref[...].astype(F32) * invf_ref[...]
    cos, sin = jnp.cos(ang), jnp.sin(ang)
    lane = lax.broadcasted_iota(jnp.int32, ang.shape, 1)
    half = MLA_ROPE // 2
    sin_lo = jnp.where(lane < half, -sin, 0.0)
    sin_hi = jnp.where((lane >= half) & (lane < MLA_ROPE), sin, 0.0)

    def rope(t):
        return (t * cos + pltpu.roll(t, LANES - half, 1) * sin_lo + pltpu.roll(t, half, 1) * sin_hi)

    kr_out[...] = rope(kr).astype(kr_out.dtype)
    scale = (MLA_NOPE + MLA_ROPE) ** -0.5
    for hh in range(MLA_HEADS):
        base = hh * 2 * LANES
        q_out[:, base:base + LANES] = (q[:, base:base + LANES] * scale).astype(q_out.dtype)
        q_out[:, base + LANES:base + 2 * LANES] = (
            rope(q[:, base + LANES:base + 2 * LANES]) * scale).astype(q_out.dtype)


def _mla_proj(x, g, wq, wkv, wkr, qn, wuq, kvn, wukv, pos, invf):
    t, d = x.shape
    tm = min(TOK_TM, t)
    full = lambda a: pl.BlockSpec(a.shape, lambda i: (0, 0))
    nq = MLA_HEADS * 2 * LANES
    return pl.pallas_call(
        _mla_proj_kernel,
        out_shape=(jax.ShapeDtypeStruct((t, nq), BF16), jax.ShapeDtypeStruct((t, nq), BF16),
                   jax.ShapeDtypeStruct((t, LANES), BF16)),
        grid=(t // tm,),
        in_specs=[pl.BlockSpec((tm, d), lambda i: (i, 0)), full(g), full(wq), full(wkv), full(wkr),
                  full(qn), full(wuq), full(kvn), full(wukv),
                  pl.BlockSpec((tm, 1), lambda i: (i, 0)), full(invf)],
        out_specs=(pl.BlockSpec((tm, nq), lambda i: (i, 0)), pl.BlockSpec((tm, nq), lambda i: (i, 0)),
                   pl.BlockSpec((tm, LANES), lambda i: (i, 0))),
        compiler_params=_cparams("parallel"),
        name="mla_proj",
    )(x, g, wq, wkv, wkr, qn, wuq, kvn, wukv, pos, invf)


def _mla_attn_kernel(q_ref, kv_ref, kr_ref, o_ref, s0, s1, p0, p1, c0, c1, m_s, l_s, acc_s, *, tq, tk):
    qi = pl.program_id(2)
    q = q_ref[0]
    sb, pb, cb = (s0, s1), (p0, p1), (c0, c1)
    m_s[...] = jnp.full_like(m_s, -jnp.inf)
    l_s[...] = jnp.zeros_like(l_s)
    acc_s[...] = jnp.zeros_like(acc_s)
    p1[...] = jnp.zeros_like(p1)
    c1[...] = jnp.ones_like(c1)
    row_minus_col = (lax.broadcasted_iota(jnp.int32, (tq, tk), 0)
                     - lax.broadcasted_iota(jnp.int32, (tq, tk), 1))

    def key_start(t):
        return pl.multiple_of(jnp.maximum(t, 0) * tk, tk)

    def stage_q(t, par):
        start = key_start(t)
        kc = jnp.concatenate([kv_ref[0, pl.ds(start, tk), :LANES], kr_ref[0, pl.ds(start, tk), :]], axis=1)
        sb[par][...] = _dot_nt(q, kc)

    def stage_s(par, thr):
        s = sb[par][...]
        if thr is not None:
            s = jnp.where(row_minus_col >= thr, s, -jnp.inf)
        m_prev = m_s[...]
        m_new = jnp.maximum(m_prev, jnp.max(s, axis=1, keepdims=True))
        p = jnp.exp(s - jnp.concatenate([m_new] * (tk // LANES), axis=1))
        corr = jnp.exp(m_prev - m_new)
        pb[par][...] = p.astype(BF16)
        cb[par][...] = corr
        l_s[...] = l_s[...] * corr + jnp.sum(p, axis=1, keepdims=True)
        m_s[...] = m_new

    def stage_p(t, par):
        v = kv_ref[0, pl.ds(key_start(t), tk), LANES:]
        acc_s[...] = acc_s[...] * cb[par][...] + _dot(pb[par][...], v)

    stage_q(0, 0)

    def body(j, c):
        t = 2 * j
        stage_q(t + 1, 1)
        stage_s(0, None)
        stage_p(t - 1, 1)
        stage_q(t + 2, 0)
        stage_s(1, None)
        stage_p(t, 0)
        return c

    lax.fori_loop(0, qi, body, 0)
    d0 = 2 * qi
    stage_q(d0 + 1, 1)
    stage_s(0, 0)
    stage_p(d0 - 1, 1)
    stage_s(1, tk)
    stage_p(d0, 0)
    stage_p(d0 + 1, 1)
    o_ref[0] = (acc_s[...] / l_s[...]).astype(o_ref.dtype)


def _mla_attention(qc, kv, kr):
    b, l, _ = qc.shape
    tq, tk = min(MLA_TQ, l), min(MLA_TK, l)
    assert tq == 2 * tk and tk % LANES == 0 and MLA_V == LANES
    col = pltpu.VMEM((tq, LANES), F32)
    return pl.pallas_call(
        functools.partial(_mla_attn_kernel, tq=tq, tk=tk),
        out_shape=jax.ShapeDtypeStruct((b, l, MLA_HEADS * MLA_V), BF16),
        grid=(b, MLA_HEADS, l // tq),
        in_specs=[
            pl.BlockSpec((1, tq, 2 * LANES), lambda bi, hh, qi: (bi, qi, hh)),
            pl.BlockSpec((1, l, 2 * LANES), lambda bi, hh, qi: (bi, 0, hh)),
            pl.BlockSpec((1, l, LANES), lambda bi, hh, qi: (bi, 0, 0)),
        ],
        out_specs=pl.BlockSpec((1, tq, LANES), lambda bi, hh, qi: (bi, qi, hh)),
        scratch_shapes=[pltpu.VMEM((tq, tk), F32), pltpu.VMEM((tq, tk), F32),
                        pltpu.VMEM((tq, tk), BF16), pltpu.VMEM((tq, tk), BF16),
                        col, col, col, col, pltpu.VMEM((tq, MLA_V), F32)],
        compiler_params=_cparams("parallel", "parallel", "arbitrary"),
        name="mla_attention",
    )(qc, kv, kr)


GDN_TM = 256
GDN_GROUP = 4
GDN_STEP_CHUNKS = 2


def _gdn_in_kernel(x_ref, g_ref, wqkv_ref, wz_ref, wba_ref, cw_ref, alog_ref, dtb_ref, tri_ref, ex_ref,
                   q_out, k_out, v_out, z_out, bw_out, gw_out, bg_out, pre_s, *, tm, tiles_per_seq):
    i = pl.program_id(0)
    h = _rms(x_ref[...], g_ref[...]).astype(BF16)
    pre = _dot(h, wqkv_ref[...])

    @pl.when(i % tiles_per_seq == 0)
    def _():
        pre_s[0:8, :] = jnp.zeros((8, GDN_CONV_DIM), F32)

    pre_s[8:8 + tm, :] = pre
    cw = cw_ref[...]
    conv = pre * cw[3:4, :]
    for j in range(1, GDN_CONV):
        conv = conv + pre_s[8 - j:8 - j + tm, :] * cw[3 - j:4 - j, :]
    pre_s[0:8, :] = pre_s[tm:tm + 8, :]
    y = conv * _sigmoid(conv)

    for hh in range(GDN_HEADS):
        sl = slice(hh * GDN_DK, (hh + 1) * GDN_DK)
        qh = y[:, sl]
        q_out[:, sl] = qh * lax.rsqrt(jnp.sum(qh * qh, axis=1, keepdims=True) + EPS) * GDN_DK ** -0.5
        kh = y[:, GDN_QK + hh * GDN_DK:GDN_QK + (hh + 1) * GDN_DK]
        k_out[:, sl] = kh * lax.rsqrt(jnp.sum(kh * kh, axis=1, keepdims=True) + EPS)
    v_out[...] = y[:, 2 * GDN_QK:]

    z = _dot(h, wz_ref[...])
    z_out[...] = z * _sigmoid(z)

    ba = _dot(h, wba_ref[...])
    lane = lax.broadcasted_iota(jnp.int32, ba.shape, 1)
    beta = _sigmoid(ba)
    gval = -jnp.exp(alog_ref[...]) * _softplus(ba + dtb_ref[...])
    gcum = _split_dot_left(tri_ref[...], gval, 3)
    bg = jnp.where(lane < GDN_HEADS, beta, gcum)
    bg_out[...] = bg
    wide = _split_dot(bg, ex_ref[...], 3)
    bw_out[...] = wide[:, :GDN_QK]
    gw_out[...] = wide[:, GDN_QK:]


def _split_dot_left(m_bf16, x, terms):
    out = None
    r = x
    for i in range(terms):
        p = r.astype(BF16)
        d = _dot(m_bf16, p)
        out = d if out is None else out + d
        if i + 1 < terms:
            r = r - p.astype(F32)
    return out


def _gdn_in(x, g, wqkv, wz, wba, cw, alog, dtb, tri, expand, seq):
    t, d = x.shape
    tm = min(GDN_TM, seq)
    full = lambda a: pl.BlockSpec(a.shape, lambda i: (0, 0))
    row = lambda n: pl.BlockSpec((tm, n), lambda i: (i, 0))
    wide = jax.ShapeDtypeStruct((t, GDN_QK), F32)
    return pl.pallas_call(
        functools.partial(_gdn_in_kernel, tm=tm, tiles_per_seq=seq // tm),
        out_shape=(wide, wide, wide, wide, wide, wide, jax.ShapeDtypeStruct((t, LANES), F32)),
        grid=(t // tm,),
        in_specs=[row(d), full(g), full(wqkv), full(wz), full(wba), full(cw), full(alog), full(dtb), full(tri),
                  full(expand)],
        out_specs=(row(GDN_QK),) * 6 + (row(LANES),),
        scratch_shapes=[pltpu.VMEM((tm + 8, GDN_CONV_DIM), F32)],
        compiler_params=_cparams("arbitrary"),
        name="gdn_in",
    )(x, g, wqkv, wz, wba, cw, alog, dtb, tri, expand)


def _gdn_chunk_kernel(q_ref, k_ref, v_ref, bw_ref, gw_ref, gr_ref, o_ref, s_ref, *, nchunk):
    c, grp = GDN_CHUNK, GDN_GROUP
    rows = grp * c

    @pl.when(pl.program_id(1) == 0)
    def _():
        s_ref[...] = jnp.zeros_like(s_ref)

    r = lax.broadcasted_iota(jnp.int32, (rows, rows), 0)
    cc = lax.broadcasted_iota(jnp.int32, (rows, rows), 1)
    same = lax.shift_right_logical(r, 6) == lax.shift_right_logical(cc, 6)
    incl = same & (r >= cc)
    strict = same & (r > cc)
    own_block = (lax.shift_right_logical(lax.broadcasted_iota(jnp.int32, (rows, grp * GDN_DK), 0), 6)
                 == lax.shift_right_logical(lax.broadcasted_iota(jnp.int32, (rows, grp * GDN_DK), 1), 7))

    def spread(t):
        t16 = t.astype(BF16)
        return jnp.where(own_block, jnp.concatenate([t16] * grp, axis=1), jnp.zeros((), BF16))

    for ci in range(nchunk):
        tok = slice(ci * c, (ci + 1) * c)
        for g in range(GDN_HEADS // grp):
            def stack(ref):
                return jnp.concatenate(
                    [ref[0, tok, (g * grp + j) * GDN_DK:(g * grp + j + 1) * GDN_DK] for j in range(grp)], axis=0)

            qs, ks, vs, bw, gw = stack(q_ref), stack(k_ref), stack(v_ref), stack(bw_ref), stack(gw_ref)
            grow = gr_ref[0, ci, g:g + 1, :]
            gdiff = jnp.concatenate([gw, gw], axis=1) - grow
            decay = jnp.where(incl, jnp.exp(jnp.where(incl, gdiff, 0.0)), 0.0)
            kb = ks * bw
            k16 = ks.astype(BF16)
            a = jnp.where(strict, _dot_nt(kb.astype(BF16), k16) * decay, 0.0)
            attn = _dot_nt(qs.astype(BF16), k16) * decay
            eg = jnp.exp(gw)
            y = jnp.concatenate([vs * bw, kb * eg], axis=1)
            p = a.astype(BF16)
            y = y - _dot(p, y.astype(BF16))
            step = 2
            while step < c:
                p = _dot(p, p).astype(BF16)
                y = y + _dot(p, y.astype(BF16))
                step *= 2
            u, w = y[:, :GDN_DV], y[:, GDN_DV:]
            s = s_ref[g]
            s16 = s.astype(BF16)
            ws_qs = _dot(jnp.concatenate([spread(w), spread(qs * eg)], axis=0), s16)
            v_new = u - ws_qs[:rows]
            v16 = v_new.astype(BF16)
            o = ws_qs[rows:] + _dot(attn.astype(BF16), v16)
            g_last = jnp.concatenate(
                [jnp.broadcast_to(gw[j * c + c - 1:j * c + c, :], (c, GDN_DK)) for j in range(grp)], axis=0)
            kd = ks * jnp.exp(g_last - gw)
            s_decay = jnp.concatenate(
                [jnp.broadcast_to(jnp.exp(gw[j * c + c - 1:j * c + c, :]), (GDN_DK, GDN_DV)) for j in range(grp)],
                axis=0)
            s_ref[g] = s * s_decay + _dot_tn(spread(kd), v16)
            for j in range(grp):
                o_ref[0, tok, (g * grp + j) * GDN_DV:(g * grp + j + 1) * GDN_DV] = o[j * c:(j + 1) * c]


def _gdn_chunks(q, k, v, bw, gw, grow):
    b, l, n = q.shape
    nchunk = GDN_STEP_CHUNKS
    tl = nchunk * GDN_CHUNK
    ngrp = GDN_HEADS // GDN_GROUP
    blk = pl.BlockSpec((1, tl, n), lambda bi, ci: (bi, ci, 0))
    return pl.pallas_call(
        functools.partial(_gdn_chunk_kernel, nchunk=nchunk),
        out_shape=jax.ShapeDtypeStruct((b, l, n), F32),
        grid=(b, l // tl),
        in_specs=[blk, blk, blk, blk, blk,
                  pl.BlockSpec((1, nchunk, ngrp, GDN_GROUP * GDN_CHUNK), lambda bi, ci: (bi, ci, 0, 0))],
        out_specs=blk,
        scratch_shapes=[pltpu.VMEM((ngrp, GDN_GROUP * GDN_DK, GDN_DV), F32)],
        compiler_params=_cparams("parallel", "arbitrary"),
        name="gdn_chunks",
    )(q, k, v, bw, gw, grow)


def _gdn_out_kernel(o_ref, z_ref, gn_ref, w_ref, x_ref, out_ref, a_s):
    gn = gn_ref[...]
    for hh in range(GDN_HEADS):
        sl = slice(hh * GDN_DV, (hh + 1) * GDN_DV)
        a_s[:, sl] = (_rms(o_ref[:, sl], gn) * z_ref[:, sl]).astype(BF16)
    out_ref[...] = x_ref[...] + _dot(a_s[...], w_ref[...])


def _gdn_out(o, zs, gn, w, x):
    t, d = x.shape
    tm = min(TOK_TM, t)
    row = lambda n: pl.BlockSpec((tm, n), lambda i: (i, 0))
    full = lambda a: pl.BlockSpec(a.shape, lambda i: (0, 0))
    return pl.pallas_call(
        _gdn_out_kernel,
        out_shape=jax.ShapeDtypeStruct((t, d), F32),
        grid=(t // tm,),
        in_specs=[row(GDN_V), row(GDN_V), full(gn), full(w), row(d)],
        out_specs=row(d),
        scratch_shapes=[pltpu.VMEM((tm, GDN_V), BF16)],
        compiler_params=_cparams("parallel"),
        name="gdn_out",
    )(o, zs, gn, w, x)


def _chunk_tri(tm):
    t = np.arange(tm)
    same = (t[:, None] // GDN_CHUNK) == (t[None, :] // GDN_CHUNK)
    return jnp.asarray((same & (t[:, None] >= t[None, :])).astype(np.float32), BF16)


def _head_expand():
    e = np.zeros((LANES, 2 * GDN_QK), np.float32)
    for i in range(2 * GDN_HEADS):
        e[i, i * GDN_DK:(i + 1) * GDN_DK] = 1.0
    return jnp.asarray(e, BF16)


def _row(v):
    return v.reshape(1, -1).astype(F32)


def _pad_lanes(v, offset):
    out = jnp.zeros((1, LANES), F32)
    return out.at[0, offset:offset + v.shape[0]].set(v.astype(F32))


def _gated_deltanet(x, norm_g, w_in, conv_w, a_log, dt_bias, out_norm, w_out, batch, seq):
    t = x.shape[0]
    wqkv = w_in[:, :GDN_CONV_DIM].astype(BF16)
    wz = w_in[:, GDN_CONV_DIM:GDN_CONV_DIM + GDN_V].astype(BF16)
    wba = jnp.pad(w_in[:, GDN_CONV_DIM + GDN_V:], ((0, 0), (0, LANES - 2 * GDN_HEADS))).astype(BF16)
    q, k, v, zs, bw, gw, bg = _gdn_in(x, _row(norm_g), wqkv, wz, wba, conv_w.astype(F32),
                                      _pad_lanes(a_log, GDN_HEADS), _pad_lanes(dt_bias, GDN_HEADS),
                                      _chunk_tri(min(GDN_TM, seq)), _head_expand(), seq)
    nc = seq // GDN_CHUNK
    grow = bg[:, GDN_HEADS:2 * GDN_HEADS].reshape(batch, nc, GDN_CHUNK, GDN_HEADS).transpose(0, 1, 3, 2)
    grow = grow.reshape(batch, nc, GDN_HEADS // GDN_GROUP, GDN_GROUP * GDN_CHUNK)
    shp = (batch, seq, GDN_QK)
    o = _gdn_chunks(q.reshape(shp), k.reshape(shp), v.reshape(shp), bw.reshape(shp), gw.reshape(shp), grow)
    return _gdn_out(o.reshape(t, GDN_V), zs, _row(out_norm), w_out.astype(BF16), x)


def _stick_breaking(x, norm_g, w_qkv, w_out, batch, seq):
    t = x.shape[0]
    n = SB_HEADS * SB_HD
    w = jnp.concatenate([w_qkv[:, :n] * (SB_HD ** -0.5 * LOG2E), w_qkv[:, n:]], axis=1).astype(BF16)
    qkv = _norm_matmul(x, _row(norm_g), w, 1024, BF16)
    o = _sb_attention(qkv.reshape(batch, seq, 3 * n), _sb_suffix_matrix(SB_TK))
    return _matmul_res(o.reshape(t, n), w_out.astype(BF16), x)


def _mla(x, norm_g, positions, w_in, q_norm, w_uq, kv_norm, w_ukv, w_out, batch, seq):
    t = x.shape[0]
    wq = w_in[:, :MLA_Q_RANK].astype(BF16)
    wkv = w_in[:, MLA_Q_RANK:MLA_Q_RANK + MLA_KV_RANK].astype(BF16)
    wkr = jnp.pad(w_in[:, MLA_Q_RANK + MLA_KV_RANK:], ((0, 0), (0, LANES - MLA_ROPE))).astype(BF16)
    qd = MLA_NOPE + MLA_ROPE
    wuq = jnp.pad(w_uq.reshape(MLA_Q_RANK, MLA_HEADS, qd), ((0, 0), (0, 0), (0, 2 * LANES - qd)))
    wuq = wuq.reshape(MLA_Q_RANK, MLA_HEADS * 2 * LANES).astype(BF16)
    half = MLA_ROPE // 2
    inv_freq = ROPE_THETA ** (-jnp.arange(half, dtype=F32) / half)
    invf = jnp.zeros((1, LANES), F32).at[0, :MLA_ROPE].set(jnp.concatenate([inv_freq, inv_freq]))
    qc, kv, kr = _mla_proj(x, _row(norm_g), wq, wkv, wkr, _row(q_norm), wuq, _row(kv_norm),
                           w_ukv.astype(BF16), positions.reshape(t, 1), invf)
    n = MLA_HEADS * 2 * LANES
    o = _mla_attention(qc.reshape(batch, seq, n), kv.reshape(batch, seq, n), kr.reshape(batch, seq, LANES))
    return _matmul_res(o.reshape(t, MLA_HEADS * MLA_V), w_out.astype(BF16), x)


def kernel(x, p, positions, ffn1_norm, ffn1_w_gate, ffn1_w_up, ffn1_w_down, mix_norm, ffn2_norm, ffn2_w_gate, ffn2_w_up, ffn2_w_down, ple_norm, ple_w_gate, ple_w_proj, gdn_w_in, gdn_conv_w, gdn_a_log, gdn_dt_bias, gdn_out_norm, gdn_w_out, sb_w_qkv, sb_w_out, mla_w_in, mla_q_norm, mla_w_uq, mla_kv_norm, mla_w_ukv, mla_w_out, final_norm):
    batch, seq, d = x.shape
    t = batch * seq
    depth = ffn1_norm.shape[0]
    x = x.reshape(t, d)
    p = p.reshape(depth, t, PLE_DIM)
    for i in range(depth):
        mixer, slot = i % N_MIXERS, i // N_MIXERS
        x = _ffn(x, _row(ffn1_norm[i]), ffn1_w_gate[i].astype(BF16), ffn1_w_up[i].astype(BF16),
                 ffn1_w_down[i].astype(BF16))
        if mixer == 0:
            x = _gated_deltanet(x, mix_norm[i], gdn_w_in[slot], gdn_conv_w[slot], gdn_a_log[slot],
                                gdn_dt_bias[slot], gdn_out_norm[slot], gdn_w_out[slot], batch, seq)
        elif mixer == 1:
            x = _stick_breaking(x, mix_norm[i], sb_w_qkv[slot], sb_w_out[slot], batch, seq)
        else:
            x = _mla(x, mix_norm[i], positions, mla_w_in[slot], mla_q_norm[slot], mla_w_uq[slot],
                     mla_kv_norm[slot], mla_w_ukv[slot], mla_w_out[slot], batch, seq)
        x = _ffn(x, _row(ffn2_norm[i]), ffn2_w_gate[i].astype(BF16), ffn2_w_up[i].astype(BF16),
                 ffn2_w_down[i].astype(BF16))
        x = _ple(x, _row(ple_norm[i]), ple_w_gate[i].astype(BF16), p[i], ple_w_proj[i].astype(BF16),
                 _row(final_norm), final=(i == depth - 1))
    return x.reshape(batch, seq, d)
```

```python
import functools

import numpy as np
import jax
import jax.numpy as jnp
from jax import lax
from jax.experimental import pallas as pl
from jax.experimental.pallas import tpu as pltpu

F32 = jnp.float32
BF16 = jnp.bfloat16

D_MODEL = 1024
DEPTH = 4
N_MIXERS = 3
PLE_DIM = 256
D_FF = 2816
EPS = 1e-6

GDN_HEADS = 8
GDN_DK = 128
GDN_DV = 128
GDN_CONV = 4
GDN_CHUNK = 64
GDN_QK = GDN_HEADS * GDN_DK
GDN_V = GDN_HEADS * GDN_DV
GDN_CONV_DIM = 2 * GDN_QK + GDN_V

SB_HEADS = 16
SB_HD = 64

MLA_HEADS = 8
MLA_Q_RANK = 384
MLA_KV_RANK = 256
MLA_NOPE = 128
MLA_ROPE = 64
MLA_V = 128
ROPE_THETA = 10000.0

LANES = 128
VMEM_LIMIT = 56 * 1024 * 1024

FFN_TM = 512
FFN_TF = 1408
TOK_TM = 512
SB_TQ = 512
SB_TK = 128
MLA_TQ = 1024
MLA_TK = 512

LOG2E = 1.4426950408889634
NEG_BIG = -1e30
SB_SPLIT_TERMS = 1
SB_DEAD_BITS = 150.0


def _cparams(*sem):
    return pltpu.CompilerParams(dimension_semantics=sem, vmem_limit_bytes=VMEM_LIMIT)


def _dot(a, b):
    return jnp.dot(a, b, preferred_element_type=F32)


def _dot_nt(a, b):
    return lax.dot_general(a, b, (((1,), (1,)), ((), ())), preferred_element_type=F32)


def _dot_tn(a, b):
    return lax.dot_general(a, b, (((0,), (0,)), ((), ())), preferred_element_type=F32)


def _rms(xf, g):
    return xf * lax.rsqrt(jnp.mean(xf * xf, axis=-1, keepdims=True) + EPS) * g


def _sigmoid(x):
    return 1.0 / (1.0 + jnp.exp(-x))


def _softplus(x):
    return jnp.maximum(x, 0.0) + jnp.log(1.0 + jnp.exp(-jnp.abs(x)))


def _split_dot(x, m_bf16, terms):
    out = None
    r = x
    for i in range(terms):
        p = r.astype(BF16)
        d = _dot(p, m_bf16)
        out = d if out is None else out + d
        if i + 1 < terms:
            r = r - p.astype(F32)
    return out


def _ffn_kernel(x_ref, g_ref, wg_ref, wu_ref, wd_ref, o_ref, h_ref, acc_ref):
    f = pl.program_id(1)

    @pl.when(f == 0)
    def _():
        h_ref[...] = _rms(x_ref[...], g_ref[...]).astype(BF16)
        acc_ref[...] = jnp.zeros_like(acc_ref)

    h = h_ref[...]
    gate = _dot(h, wg_ref[...])
    up = _dot(h, wu_ref[...])
    act = (gate * _sigmoid(gate) * up).astype(BF16)
    acc_ref[...] += _dot(act, wd_ref[...])

    @pl.when(f == pl.num_programs(1) - 1)
    def _():
        o_ref[...] = x_ref[...] + 0.5 * acc_ref[...]


def _ffn(x, g, wg, wu, wd):
    t, d = x.shape
    tm, tf = min(FFN_TM, t), FFN_TF
    nf = D_FF // tf
    return pl.pallas_call(
        _ffn_kernel,
        out_shape=jax.ShapeDtypeStruct((t, d), F32),
        grid=(t // tm, nf),
        in_specs=[
            pl.BlockSpec((tm, d), lambda i, f: (i, 0)),
            pl.BlockSpec((1, d), lambda i, f: (0, 0)),
            pl.BlockSpec((d, tf), lambda i, f: (0, f)),
            pl.BlockSpec((d, tf), lambda i, f: (0, f)),
            pl.BlockSpec((tf, d), lambda i, f: (f, 0)),
        ],
        out_specs=pl.BlockSpec((tm, d), lambda i, f: (i, 0)),
        scratch_shapes=[pltpu.VMEM((tm, d), BF16), pltpu.VMEM((tm, d), F32)],
        compiler_params=_cparams("parallel", "arbitrary"),
        name="ffn",
    )(x, g, wg, wu, wd)


def _ple_kernel(x_ref, g_ref, wgate_ref, p_ref, wp_ref, fin_ref, o_ref, *, final):
    x = x_ref[...]
    h = _rms(x, g_ref[...]).astype(BF16)
    gate = _sigmoid(_dot(h, wgate_ref[...]))
    y = x + gate * _dot(p_ref[...].astype(BF16), wp_ref[...])
    if final:
        y = _rms(y, fin_ref[...])
    o_ref[...] = y


def _ple(x, g, wgate, p, wp, fin, final):
    t, d = x.shape
    tm = min(TOK_TM, t)
    return pl.pallas_call(
        functools.partial(_ple_kernel, final=final),
        out_shape=jax.ShapeDtypeStruct((t, d), F32),
        grid=(t // tm,),
        in_specs=[
            pl.BlockSpec((tm, d), lambda i: (i, 0)),
            pl.BlockSpec((1, d), lambda i: (0, 0)),
            pl.BlockSpec((d, d), lambda i: (0, 0)),
            pl.BlockSpec((tm, PLE_DIM), lambda i: (i, 0)),
            pl.BlockSpec((PLE_DIM, d), lambda i: (0, 0)),
            pl.BlockSpec((1, d), lambda i: (0, 0)),
        ],
        out_specs=pl.BlockSpec((tm, d), lambda i: (i, 0)),
        compiler_params=_cparams("parallel"),
        name="ple",
    )(x, g, wgate, p, wp, fin)


def _norm_matmul_kernel(x_ref, g_ref, w_ref, o_ref, h_ref):
    @pl.when(pl.program_id(1) == 0)
    def _():
        h_ref[...] = _rms(x_ref[...], g_ref[...]).astype(BF16)

    o_ref[...] = _dot(h_ref[...], w_ref[...]).astype(o_ref.dtype)


def _norm_matmul(x, g, w, tn, out_dtype):
    t, d = x.shape
    n = w.shape[1]
    tm = min(TOK_TM, t)
    return pl.pallas_call(
        _norm_matmul_kernel,
        out_shape=jax.ShapeDtypeStruct((t, n), out_dtype),
        grid=(t // tm, n // tn),
        in_specs=[
            pl.BlockSpec((tm, d), lambda i, j: (i, 0)),
            pl.BlockSpec((1, d), lambda i, j: (0, 0)),
            pl.BlockSpec((d, tn), lambda i, j: (0, j)),
        ],
        out_specs=pl.BlockSpec((tm, tn), lambda i, j: (i, j)),
        scratch_shapes=[pltpu.VMEM((tm, d), BF16)],
        compiler_params=_cparams("parallel", "arbitrary"),
        name="norm_matmul",
    )(x, g, w)


def _matmul_res_kernel(a_ref, w_ref, x_ref, o_ref):
    o_ref[...] = x_ref[...] + _dot(a_ref[...].astype(BF16), w_ref[...])


def _matmul_res(a, w, x):
    t, k = a.shape
    d = w.shape[1]
    tm = min(TOK_TM, t)
    return pl.pallas_call(
        _matmul_res_kernel,
        out_shape=jax.ShapeDtypeStruct((t, d), F32),
        grid=(t // tm,),
        in_specs=[
            pl.BlockSpec((tm, k), lambda i: (i, 0)),
            pl.BlockSpec((k, d), lambda i: (0, 0)),
            pl.BlockSpec((tm, d), lambda i: (i, 0)),
        ],
        out_specs=pl.BlockSpec((tm, d), lambda i: (i, 0)),
        compiler_params=_cparams("parallel"),
        name="matmul_res",
    )(a, w, x)


def _sb_kernel(q_ref, k_ref, v_ref, u_ref, o_ref,
               z0, z1, lb0, lb1, sf0, sf1, a0, a1, run_a, run_b, acc_o, *, tq, tk):
    qi = pl.program_id(2)
    ratio = tq // tk
    n_tiles = ratio * (qi + 1)
    q = q_ref[0]
    u2 = u_ref[...]
    zb, lbb, sfb, ab = (z0, z1), (lb0, lb1), (sf0, sf1), (a0, a1)
    first = lax.broadcasted_iota(jnp.int32, (tk, LANES), 1) < SB_HD
    row_minus_col = (lax.broadcasted_iota(jnp.int32, (tq, 2 * tk), 0)
                     - (lax.broadcasted_iota(jnp.int32, (tq, 2 * tk), 1) & (tk - 1)))
    acc_o[...] = jnp.zeros_like(acc_o)
    run_a[...] = jnp.zeros_like(run_a)
    run_b[...] = jnp.zeros_like(run_b)

    def blockdiag(t):
        zero = jnp.zeros_like(t)
        return jnp.concatenate([jnp.where(first, t, zero), jnp.where(first, zero, t)], axis=0)

    def key_start(t):
        kb = jnp.maximum(ratio * qi + ratio - 1 - t, 0)
        return pl.multiple_of(kb * tk, tk)

    def stage_q(t, par):
        kk = blockdiag(k_ref[0, pl.ds(key_start(t), tk), :])
        zb[par][...] = _dot_nt(q, kk)

    def stage_s1(par, thr):
        z = zb[par][...]
        neg_abs = lax.bitcast_convert_type(
            lax.bitcast_convert_type(z, jnp.uint32) | jnp.uint32(0x80000000), F32)
        sp = jnp.maximum(z, 0.0) + jnp.log(1.0 + jnp.exp2(neg_abs)) * LOG2E
        ra, rb = run_a[...], run_b[...]
        lb = z - sp - jnp.concatenate([ra, rb], axis=1)
        if thr is not None:
            valid = row_minus_col > thr
            sp = jnp.where(valid, sp, 0.0)
            lb = jnp.where(valid, lb, NEG_BIG)
        lbb[par][...] = lb
        sfb[par][...] = _split_dot(sp, u2, SB_SPLIT_TERMS)
        run_a[...] = ra + jnp.sum(sp[:, :tk], axis=1, keepdims=True)
        run_b[...] = rb + jnp.sum(sp[:, tk:], axis=1, keepdims=True)

    def stage_s2(par):
        ab[par][...] = jnp.exp2(lbb[par][...] - sfb[par][...]).astype(BF16)

    def stage_p(t, par):
        vv = blockdiag(v_ref[0, pl.ds(key_start(t), tk), :])
        acc_o[...] += _dot(ab[par][...], vv)

    def step(u, par, thr=None, q_=True, s1=True, s2=True, p_=True):
        if q_:
            stage_q(u, par)
        if s1:
            stage_s1(1 - par, thr)
        if s2:
            stage_s2(par)
        if p_:
            stage_p(u - 3, 1 - par)

    for u in range(ratio + 1):
        step(u, u % 2, thr=(ratio - u) * tk, s1=u >= 1, s2=u >= 2, p_=u >= 3)

    n_pairs = (n_tiles - ratio) // 2

    def body(c):
        j, _ = c
        u = ratio + 1 + 2 * j
        step(u, (ratio + 1) % 2)
        step(u + 1, ratio % 2)
        return j + 1, jnp.min(jnp.minimum(run_a[...], run_b[...]))

    def live(c):
        j, lowest = c
        return (j < n_pairs) & (lowest < SB_DEAD_BITS)

    j_end, _ = lax.while_loop(live, body, (jnp.int32(0), jnp.float32(0.0)))
    n_done = ratio + 2 * j_end
    step(n_done + 1, (ratio + 1) % 2, q_=False, s1=False)
    step(n_done + 2, ratio % 2, q_=False, s1=False, s2=False)
    o_ref[0] = acc_o[...].astype(o_ref.dtype)


def _sb_attention(qkv, u2):
    b, l, _ = qkv.shape
    tq, tk = min(SB_TQ, l), SB_TK
    assert (tq // tk) % 2 == 0 and tk == LANES
    npair = SB_HEADS * SB_HD // LANES
    wide = pltpu.VMEM((tq, 2 * tk), F32)
    col = pltpu.VMEM((tq, LANES), F32)
    return pl.pallas_call(
        functools.partial(_sb_kernel, tq=tq, tk=tk),
        out_shape=jax.ShapeDtypeStruct((b, l, SB_HEADS * SB_HD), BF16),
        grid=(b, npair, l // tq),
        in_specs=[
            pl.BlockSpec((1, tq, LANES), lambda bi, j, qi: (bi, qi, j)),
            pl.BlockSpec((1, l, LANES), lambda bi, j, qi: (bi, 0, npair + j)),
            pl.BlockSpec((1, l, LANES), lambda bi, j, qi: (bi, 0, 2 * npair + j)),
            pl.BlockSpec((2 * tk, 2 * tk), lambda bi, j, qi: (0, 0)),
        ],
        out_specs=pl.BlockSpec((1, tq, LANES), lambda bi, j, qi: (bi, qi, j)),
        scratch_shapes=[wide, wide, wide, wide, wide, wide,
                        pltpu.VMEM((tq, 2 * tk), BF16), pltpu.VMEM((tq, 2 * tk), BF16),
                        col, col, pltpu.VMEM((tq, LANES), F32)],
        compiler_params=_cparams("parallel", "parallel", "arbitrary"),
        name="sb_attention",
    )(qkv, qkv, qkv, u2)


def _sb_suffix_matrix(tk):
    s = np.arange(2 * tk)
    same = (s[:, None] // tk) == (s[None, :] // tk)
    return jnp.asarray((same & (s[:, None] > s[None, :])).astype(np.float32), BF16)


def _mla_proj_kernel(x_ref, g_ref, wq_ref, wkv_ref, wkr_ref, qn_ref, wuq_ref, kvn_ref, wukv_ref,
                     pos_ref, invf_ref, q_out, kv_out, kr_out):
    h = _rms(x_ref[...], g_ref[...]).astype(BF16)
    cq = _dot(h, wq_ref[...])
    ckv = _dot(h, wkv_ref[...])
    kr = _dot(h, wkr_ref[...])
    q = _dot(_rms(cq, qn_ref[...]).astype(BF16), wuq_ref[...])
    kv = _dot(_rms(ckv, kvn_ref[...]).astype(BF16), wukv_ref[...])
    kv_out[...] = kv.astype(kv_out.dtype)

    ang = pos_ref[...].astype(F32) * invf_ref[...]
    cos, sin = jnp.cos(ang), jnp.sin(ang)
    lane = lax.broadcasted_iota(jnp.int32, ang.shape, 1)
    half = MLA_ROPE // 2
    sin_lo = jnp.where(lane < half, -sin, 0.0)
    sin_hi = jnp.where((lane >= half) & (lane < MLA_ROPE), sin, 0.0)

    def rope(t):
        return (t * cos + pltpu.roll(t, LANES - half, 1) * sin_lo + pltpu.roll(t, half, 1) * sin_hi)

    kr_out[...] = rope(kr).astype(kr_out.dtype)
    scale = (MLA_NOPE + MLA_ROPE) ** -0.5
    for hh in range(MLA_HEADS):
        base = hh * 2 * LANES
        q_out[:, base:base + LANES] = (q[:, base:base + LANES] * scale).astype(q_out.dtype)
        q_out[:, base + LANES:base + 2 * LANES] = (
            rope(q[:, base + LANES:base + 2 * LANES]) * scale).astype(q_out.dtype)


def _mla_proj(x, g, wq, wkv, wkr, qn, wuq, kvn, wukv, pos, invf):
    t, d = x.shape
    tm = min(TOK_TM, t)
    full = lambda a: pl.BlockSpec(a.shape, lambda i: (0, 0))
    nq = MLA_HEADS * 2 * LANES
    return pl.pallas_call(
        _mla_proj_kernel,
        out_shape=(jax.ShapeDtypeStruct((t, nq), BF16), jax.ShapeDtypeStruct((t, nq), BF16),
                   jax.ShapeDtypeStruct((t, LANES), BF16)),
        grid=(t // tm,),
        in_specs=[pl.BlockSpec((tm, d), lambda i: (i, 0)), full(g), full(wq), full(wkv), full(wkr),
                  full(qn), full(wuq), full(kvn), full(wukv),
                  pl.BlockSpec((tm, 1), lambda i: (i, 0)), full(invf)],
        out_specs=(pl.BlockSpec((tm, nq), lambda i: (i, 0)), pl.BlockSpec((tm, nq), lambda i: (i, 0)),
                   pl.BlockSpec((tm, LANES), lambda i: (i, 0))),
        compiler_params=_cparams("parallel"),
        name="mla_proj",
    )(x, g, wq, wkv, wkr, qn, wuq, kvn, wukv, pos, invf)


def _mla_attn_kernel(q_ref, kv_ref, kr_ref, o_ref, s0, s1, p0, p1, c0, c1, m_s, l_s, acc_s, *, tq, tk):
    qi = pl.program_id(2)
    q = q_ref[0]
    sb, pb, cb = (s0, s1), (p0, p1), (c0, c1)
    m_s[...] = jnp.full_like(m_s, -jnp.inf)
    l_s[...] = jnp.zeros_like(l_s)
    acc_s[...] = jnp.zeros_like(acc_s)
    p1[...] = jnp.zeros_like(p1)
    c1[...] = jnp.ones_like(c1)
    row_minus_col = (lax.broadcasted_iota(jnp.int32, (tq, tk), 0)
                     - lax.broadcasted_iota(jnp.int32, (tq, tk), 1))

    def key_start(t):
        return pl.multiple_of(jnp.maximum(t, 0) * tk, tk)

    def stage_q(t, par):
        start = key_start(t)
        kc = jnp.concatenate([kv_ref[0, pl.ds(start, tk), :LANES], kr_ref[0, pl.ds(start, tk), :]], axis=1)
        sb[par][...] = _dot_nt(q, kc)

    def stage_s(par, thr):
        s = sb[par][...]
        if thr is not None:
            s = jnp.where(row_minus_col >= thr, s, -jnp.inf)
        m_prev = m_s[...]
        m_new = jnp.maximum(m_prev, jnp.max(s, axis=1, keepdims=True))
        p = jnp.exp(s - jnp.concatenate([m_new] * (tk // LANES), axis=1))
        corr = jnp.exp(m_prev - m_new)
        pb[par][...] = p.astype(BF16)
        cb[par][...] = corr
        l_s[...] = l_s[...] * corr + jnp.sum(p, axis=1, keepdims=True)
        m_s[...] = m_new

    def stage_p(t, par):
        v = kv_ref[0, pl.ds(key_start(t), tk), LANES:]
        acc_s[...] = acc_s[...] * cb[par][...] + _dot(pb[par][...], v)

    stage_q(0, 0)

    def body(j, c):
        t = 2 * j
        stage_q(t + 1, 1)
        stage_s(0, None)
        stage_p(t - 1, 1)
        stage_q(t + 2, 0)
        stage_s(1, None)
        stage_p(t, 0)
        return c

    lax.fori_loop(0, qi, body, 0)
    d0 = 2 * qi
    stage_q(d0 + 1, 1)
    stage_s(0, 0)
    stage_p(d0 - 1, 1)
    stage_s(1, tk)
    stage_p(d0, 0)
    stage_p(d0 + 1, 1)
    o_ref[0] = (acc_s[...] / l_s[...]).astype(o_ref.dtype)


def _mla_attention(qc, kv, kr):
    b, l, _ = qc.shape
    tq, tk = min(MLA_TQ, l), min(MLA_TK, l)
    assert tq == 2 * tk and tk % LANES == 0 and MLA_V == LANES
    col = pltpu.VMEM((tq, LANES), F32)
    return pl.pallas_call(
        functools.partial(_mla_attn_kernel, tq=tq, tk=tk),
        out_shape=jax.ShapeDtypeStruct((b, l, MLA_HEADS * MLA_V), BF16),
        grid=(b, MLA_HEADS, l // tq),
        in_specs=[
            pl.BlockSpec((1, tq, 2 * LANES), lambda bi, hh, qi: (bi, qi, hh)),
            pl.BlockSpec((1, l, 2 * LANES), lambda bi, hh, qi: (bi, 0, hh)),
            pl.BlockSpec((1, l, LANES), lambda bi, hh, qi: (bi, 0, 0)),
        ],
        out_specs=pl.BlockSpec((1, tq, LANES), lambda bi, hh, qi: (bi, qi, hh)),
        scratch_shapes=[pltpu.VMEM((tq, tk), F32), pltpu.VMEM((tq, tk), F32),
                        pltpu.VMEM((tq, tk), BF16), pltpu.VMEM((tq, tk), BF16),
                        col, col, col, col, pltpu.VMEM((tq, MLA_V), F32)],
        compiler_params=_cparams("parallel", "parallel", "arbitrary"),
        name="mla_attention",
    )(qc, kv, kr)


GDN_TM = 256
GDN_GROUP = 4
GDN_STEP_CHUNKS = 2


def _gdn_in_kernel(x_ref, g_ref, wqkv_ref, wz_ref, wba_ref, cw_ref, alog_ref, dtb_ref, tri_ref,
                   q_out, k_out, v_out, z_out, bw_out, gw_out, bg_out, pre_s, *, tm, tiles_per_seq):
    i = pl.program_id(0)
    h = _rms(x_ref[...], g_ref[...]).astype(BF16)

    @pl.when(i % tiles_per_seq == 0)
    def _():
        pre_s[0:8, :] = jnp.zeros((8, GDN_CONV_DIM), F32)

    ba = _dot(h, wba_ref[...])
    lane = lax.broadcasted_iota(jnp.int32, ba.shape, 1)
    beta = _sigmoid(ba)
    gval = -jnp.exp(alog_ref[...]) * _softplus(ba + dtb_ref[...])
    gcum = _split_dot_left(tri_ref[...], gval, 3)
    bg = jnp.where(lane < GDN_HEADS, beta, gcum)
    bg_out[...] = bg
    for hh in range(GDN_HEADS):
        sl = slice(hh * GDN_DK, (hh + 1) * GDN_DK)
        bw_out[:, sl] = jnp.broadcast_to(bg[:, hh:hh + 1], (tm, GDN_DK))
        gw_out[:, sl] = jnp.broadcast_to(bg[:, GDN_HEADS + hh:GDN_HEADS + hh + 1], (tm, GDN_DK))

    cw = cw_ref[...]
    width = 2 * GDN_DK
    for blk in range(GDN_CONV_DIM // width):
        cs = slice(blk * width, (blk + 1) * width)
        pre = _dot(h, wqkv_ref[:, cs])
        pre_s[8:8 + tm, cs] = pre
        conv = pre * cw[3:4, cs]
        for j in range(1, GDN_CONV):
            conv = conv + pre_s[8 - j:8 - j + tm, cs] * cw[3 - j:4 - j, cs]
        pre_s[0:8, cs] = pre_s[tm:tm + 8, cs]
        y = conv * _sigmoid(conv)
        if blk * width >= 2 * GDN_QK:
            v_out[:, blk * width - 2 * GDN_QK:(blk + 1) * width - 2 * GDN_QK] = y
            continue
        is_q = blk * width < GDN_QK
        out = q_out if is_q else k_out
        for half in range(2):
            yh = y[:, half * GDN_DK:(half + 1) * GDN_DK]
            yn = yh * lax.rsqrt(jnp.sum(yh * yh, axis=1, keepdims=True) + EPS)
            col = (blk * width) % GDN_QK + half * GDN_DK
            out[:, col:col + GDN_DK] = yn * GDN_DK ** -0.5 if is_q else yn

    for blk in range(GDN_V // width):
        cs = slice(blk * width, (blk + 1) * width)
        z = _dot(h, wz_ref[:, cs])
        z_out[:, cs] = z * _sigmoid(z)


def _split_dot_left(m_bf16, x, terms):
    out = None
    r = x
    for i in range(terms):
        p = r.astype(BF16)
        d = _dot(m_bf16, p)
        out = d if out is None else out + d
        if i + 1 < terms:
            r = r - p.astype(F32)
    return out


def _gdn_in(x, g, wqkv, wz, wba, cw, alog, dtb, tri, seq):
    t, d = x.shape
    tm = min(GDN_TM, seq)
    full = lambda a: pl.BlockSpec(a.shape, lambda i: (0, 0))
    row = lambda n: pl.BlockSpec((tm, n), lambda i: (i, 0))
    wide = jax.ShapeDtypeStruct((t, GDN_QK), F32)
    return pl.pallas_call(
        functools.partial(_gdn_in_kernel, tm=tm, tiles_per_seq=seq // tm),
        out_shape=(wide, wide, wide, wide, wide, wide, jax.ShapeDtypeStruct((t, LANES), F32)),
        grid=(t // tm,),
        in_specs=[row(d), full(g), full(wqkv), full(wz), full(wba), full(cw), full(alog), full(dtb), full(tri)],
        out_specs=(row(GDN_QK),) * 6 + (row(LANES),),
        scratch_shapes=[pltpu.VMEM((tm + 8, GDN_CONV_DIM), F32)],
        compiler_params=_cparams("arbitrary"),
        name="gdn_in",
    )(x, g, wqkv, wz, wba, cw, alog, dtb, tri)


def _gdn_chunk_kernel(q_ref, k_ref, v_ref, bw_ref, gw_ref, gr_ref, o_ref, s_ref, *, nchunk):
    c, grp = GDN_CHUNK, GDN_GROUP
    rows = grp * c

    @pl.when(pl.program_id(1) == 0)
    def _():
        s_ref[...] = jnp.zeros_like(s_ref)

    r = lax.broadcasted_iota(jnp.int32, (rows, rows), 0)
    cc = lax.broadcasted_iota(jnp.int32, (rows, rows), 1)
    same = lax.shift_right_logical(r, 6) == lax.shift_right_logical(cc, 6)
    incl = same & (r >= cc)
    strict = same & (r > cc)
    own_block = (lax.shift_right_logical(lax.broadcasted_iota(jnp.int32, (rows, grp * GDN_DK), 0), 6)
                 == lax.shift_right_logical(lax.broadcasted_iota(jnp.int32, (rows, grp * GDN_DK), 1), 7))

    def spread(t):
        t16 = t.astype(BF16)
        return jnp.where(own_block, jnp.concatenate([t16] * grp, axis=1), jnp.zeros((), BF16))

    ngrp = GDN_HEADS // grp
    units = [(ci, g) for ci in range(nchunk) for g in range(ngrp)]

    def stack(ref, ci, g):
        return jnp.concatenate(
            [ref[0, ci * c:(ci + 1) * c, (g * grp + j) * GDN_DK:(g * grp + j + 1) * GDN_DK] for j in range(grp)],
            axis=0)

    ks = [stack(k_ref, ci, g) for ci, g in units]
    gw = [stack(gw_ref, ci, g) for ci, g in units]
    bw = [stack(bw_ref, ci, g) for ci, g in units]
    k16 = [t.astype(BF16) for t in ks]
    kb = [ks[i] * bw[i] for i in range(len(units))]
    eg = [jnp.exp(t) for t in gw]
    decay = []
    for i, (ci, g) in enumerate(units):
        gdiff = jnp.concatenate([gw[i], gw[i]], axis=1) - gr_ref[0, ci, g:g + 1, :]
        decay.append(jnp.where(incl, jnp.exp(jnp.where(incl, gdiff, 0.0)), 0.0))
    p = [jnp.where(strict, _dot_nt(kb[i].astype(BF16), k16[i]) * decay[i], 0.0).astype(BF16)
         for i in range(len(units))]
    y = [jnp.concatenate([stack(v_ref, ci, g) * bw[i], kb[i] * eg[i]], axis=1)
         for i, (ci, g) in enumerate(units)]
    y = [y[i] - _dot(p[i], y[i].astype(BF16)) for i in range(len(units))]
    step = 2
    while step < c:
        p = [_dot(t, t).astype(BF16) for t in p]
        y = [y[i] + _dot(p[i], y[i].astype(BF16)) for i in range(len(units))]
        step *= 2
    qs = [stack(q_ref, ci, g) for ci, g in units]
    attn = [(_dot_nt(qs[i].astype(BF16), k16[i]) * decay[i]).astype(BF16) for i in range(len(units))]

    for ci in range(nchunk):
        idx = [ci * ngrp + g for g in range(ngrp)]
        s = [s_ref[g] for g in range(ngrp)]
        ws_qs = [_dot(jnp.concatenate([spread(y[i][:, GDN_DV:]), spread(qs[i] * eg[i])], axis=0),
                      s[g].astype(BF16)) for g, i in enumerate(idx)]
        v16 = [(y[i][:, :GDN_DV] - ws_qs[g][:rows]).astype(BF16) for g, i in enumerate(idx)]
        o = [ws_qs[g][rows:] + _dot(attn[i], v16[g]) for g, i in enumerate(idx)]
        for g, i in enumerate(idx):
            last = [gw[i][j * c + c - 1:j * c + c, :] for j in range(grp)]
            g_last = jnp.concatenate([jnp.broadcast_to(t, (c, GDN_DK)) for t in last], axis=0)
            s_decay = jnp.concatenate([jnp.broadcast_to(jnp.exp(t), (GDN_DK, GDN_DV)) for t in last], axis=0)
            kd = ks[i] * jnp.exp(g_last - gw[i])
            s_ref[g] = s[g] * s_decay + _dot_tn(spread(kd), v16[g])
            for j in range(grp):
                o_ref[0, ci * c:(ci + 1) * c, (g * grp + j) * GDN_DV:(g * grp + j + 1) * GDN_DV] = (
                    o[g][j * c:(j + 1) * c])


def _gdn_chunks(q, k, v, bw, gw, grow):
    b, l, n = q.shape
    nchunk = GDN_STEP_CHUNKS
    tl = nchunk * GDN_CHUNK
    ngrp = GDN_HEADS // GDN_GROUP
    blk = pl.BlockSpec((1, tl, n), lambda bi, ci: (bi, ci, 0))
    return pl.pallas_call(
        functools.partial(_gdn_chunk_kernel, nchunk=nchunk),
        out_shape=jax.ShapeDtypeStruct((b, l, n), F32),
        grid=(b, l // tl),
        in_specs=[blk, blk, blk, blk, blk,
                  pl.BlockSpec((1, nchunk, ngrp, GDN_GROUP * GDN_CHUNK), lambda bi, ci: (bi, ci, 0, 0))],
        out_specs=blk,
        scratch_shapes=[pltpu.VMEM((ngrp, GDN_GROUP * GDN_DK, GDN_DV), F32)],
        compiler_params=_cparams("parallel", "arbitrary"),
        name="gdn_chunks",
    )(q, k, v, bw, gw, grow)


def _gdn_out_kernel(o_ref, z_ref, gn_ref, w_ref, x_ref, out_ref, a_s):
    gn = gn_ref[...]
    for hh in range(GDN_HEADS):
        sl = slice(hh * GDN_DV, (hh + 1) * GDN_DV)
        a_s[:, sl] = (_rms(o_ref[:, sl], gn) * z_ref[:, sl]).astype(BF16)
    out_ref[...] = x_ref[...] + _dot(a_s[...], w_ref[...])


def _gdn_out(o, zs, gn, w, x):
    t, d = x.shape
    tm = min(TOK_TM, t)
    row = lambda n: pl.BlockSpec((tm, n), lambda i: (i, 0))
    full = lambda a: pl.BlockSpec(a.shape, lambda i: (0, 0))
    return pl.pallas_call(
        _gdn_out_kernel,
        out_shape=jax.ShapeDtypeStruct((t, d), F32),
        grid=(t // tm,),
        in_specs=[row(GDN_V), row(GDN_V), full(gn), full(w), row(d)],
        out_specs=row(d),
        scratch_shapes=[pltpu.VMEM((tm, GDN_V), BF16)],
        compiler_params=_cparams("parallel"),
        name="gdn_out",
    )(o, zs, gn, w, x)


def _chunk_tri(tm):
    t = np.arange(tm)
    same = (t[:, None] // GDN_CHUNK) == (t[None, :] // GDN_CHUNK)
    return jnp.asarray((same & (t[:, None] >= t[None, :])).astype(np.float32), BF16)


def _row(v):
    return v.reshape(1, -1).astype(F32)


def _pad_lanes(v, offset):
    out = jnp.zeros((1, LANES), F32)
    return out.at[0, offset:offset + v.shape[0]].set(v.astype(F32))


def _gated_deltanet(x, norm_g, w_in, conv_w, a_log, dt_bias, out_norm, w_out, batch, seq):
    t = x.shape[0]
    wqkv = w_in[:, :GDN_CONV_DIM].astype(BF16)
    wz = w_in[:, GDN_CONV_DIM:GDN_CONV_DIM + GDN_V].astype(BF16)
    wba = jnp.pad(w_in[:, GDN_CONV_DIM + GDN_V:], ((0, 0), (0, LANES - 2 * GDN_HEADS))).astype(BF16)
    q, k, v, zs, bw, gw, bg = _gdn_in(x, _row(norm_g), wqkv, wz, wba, conv_w.astype(F32),
                                      _pad_lanes(a_log, GDN_HEADS), _pad_lanes(dt_bias, GDN_HEADS),
                                      _chunk_tri(min(GDN_TM, seq)), seq)
    nc = seq // GDN_CHUNK
    grow = bg[:, GDN_HEADS:2 * GDN_HEADS].reshape(batch, nc, GDN_CHUNK, GDN_HEADS).transpose(0, 1, 3, 2)
    grow = grow.reshape(batch, nc, GDN_HEADS // GDN_GROUP, GDN_GROUP * GDN_CHUNK)
    shp = (batch, seq, GDN_QK)
    o = _gdn_chunks(q.reshape(shp), k.reshape(shp), v.reshape(shp), bw.reshape(shp), gw.reshape(shp), grow)
    return _gdn_out(o.reshape(t, GDN_V), zs, _row(out_norm), w_out.astype(BF16), x)


def _stick_breaking(x, norm_g, w_qkv, w_out, batch, seq):
    t = x.shape[0]
    n = SB_HEADS * SB_HD
    w = jnp.concatenate([w_qkv[:, :n] * (SB_HD ** -0.5 * LOG2E), w_qkv[:, n:]], axis=1).astype(BF16)
    qkv = _norm_matmul(x, _row(norm_g), w, 1024, BF16)
    o = _sb_attention(qkv.reshape(batch, seq, 3 * n), _sb_suffix_matrix(SB_TK))
    return _matmul_res(o.reshape(t, n), w_out.astype(BF16), x)


def _mla(x, norm_g, positions, w_in, q_norm, w_uq, kv_norm, w_ukv, w_out, batch, seq):
    t = x.shape[0]
    wq = w_in[:, :MLA_Q_RANK].astype(BF16)
    wkv = w_in[:, MLA_Q_RANK:MLA_Q_RANK + MLA_KV_RANK].astype(BF16)
    wkr = jnp.pad(w_in[:, MLA_Q_RANK + MLA_KV_RANK:], ((0, 0), (0, LANES - MLA_ROPE))).astype(BF16)
    qd = MLA_NOPE + MLA_ROPE
    wuq = jnp.pad(w_uq.reshape(MLA_Q_RANK, MLA_HEADS, qd), ((0, 0), (0, 0), (0, 2 * LANES - qd)))
    wuq = wuq.reshape(MLA_Q_RANK, MLA_HEADS * 2 * LANES).astype(BF16)
    half = MLA_ROPE // 2
    inv_freq = ROPE_THETA ** (-jnp.arange(half, dtype=F32) / half)
    invf = jnp.zeros((1, LANES), F32).at[0, :MLA_ROPE].set(jnp.concatenate([inv_freq, inv_freq]))
    qc, kv, kr = _mla_proj(x, _row(norm_g), wq, wkv, wkr, _row(q_norm), wuq, _row(kv_norm),
                           w_ukv.astype(BF16), positions.reshape(t, 1), invf)
    n = MLA_HEADS * 2 * LANES
    o = _mla_attention(qc.reshape(batch, seq, n), kv.reshape(batch, seq, n), kr.reshape(batch, seq, LANES))
    return _matmul_res(o.reshape(t, MLA_HEADS * MLA_V), w_out.astype(BF16), x)


def kernel(x, p, positions, ffn1_norm, ffn1_w_gate, ffn1_w_up, ffn1_w_down, mix_norm, ffn2_norm, ffn2_w_gate, ffn2_w_up, ffn2_w_down, ple_norm, ple_w_gate, ple_w_proj, gdn_w_in, gdn_conv_w, gdn_a_log, gdn_dt_bias, gdn_out_norm, gdn_w_out, sb_w_qkv, sb_w_out, mla_w_in, mla_q_norm, mla_w_uq, mla_kv_norm, mla_w_ukv, mla_w_out, final_norm):
    batch, seq, d = x.shape
    t = batch * seq
    depth = ffn1_norm.shape[0]
    x = x.reshape(t, d)
    p = p.reshape(depth, t, PLE_DIM)
    for i in range(depth):
        mixer, slot = i % N_MIXERS, i // N_MIXERS
        x = _ffn(x, _row(ffn1_norm[i]), ffn1_w_gate[i].astype(BF16), ffn1_w_up[i].astype(BF16),
                 ffn1_w_down[i].astype(BF16))
        if mixer == 0:
            x = _gated_deltanet(x, mix_norm[i], gdn_w_in[slot], gdn_conv_w[slot], gdn_a_log[slot],
                                gdn_dt_bias[slot], gdn_out_norm[slot], gdn_w_out[slot], batch, seq)
        elif mixer == 1:
            x = _stick_breaking(x, mix_norm[i], sb_w_qkv[slot], sb_w_out[slot], batch, seq)
        else:
            x = _mla(x, mix_norm[i], positions, mla_w_in[slot], mla_q_norm[slot], mla_w_uq[slot],
                     mla_kv_norm[slot], mla_w_ukv[slot], mla_w_out[slot], batch, seq)
        x = _ffn(x, _row(ffn2_norm[i]), ffn2_w_gate[i].astype(BF16), ffn2_w_up[i].astype(BF16),
                 ffn2_w_down[i].astype(BF16))
        x = _ple(x, _row(ple_norm[i]), ple_w_gate[i].astype(BF16), p[i], ple_w_proj[i].astype(BF16),
                 _row(final_norm), final=(i == depth - 1))
    return x.reshape(batch, seq, d)
```

```python
import functools

import numpy as np
import jax
import jax.numpy as jnp
from jax import lax
from jax.experimental import pallas as pl
from jax.experimental.pallas import tpu as pltpu

F32 = jnp.float32
BF16 = jnp.bfloat16

D_MODEL = 1024
DEPTH = 4
N_MIXERS = 3
PLE_DIM = 256
D_FF = 2816
EPS = 1e-6

GDN_HEADS = 8
GDN_DK = 128
GDN_DV = 128
GDN_CONV = 4
GDN_CHUNK = 64
GDN_QK = GDN_HEADS * GDN_DK
GDN_V = GDN_HEADS * GDN_DV
GDN_CONV_DIM = 2 * GDN_QK + GDN_V

SB_HEADS = 16
SB_HD = 64

MLA_HEADS = 8
MLA_Q_RANK = 384
MLA_KV_RANK = 256
MLA_NOPE = 128
MLA_ROPE = 64
MLA_V = 128
ROPE_THETA = 10000.0

LANES = 128
VMEM_LIMIT = 56 * 1024 * 1024

FFN_TM = 512
FFN_TF = 1408
TOK_TM = 512
SB_TQ = 512
SB_TK = 128
MLA_TQ = 2 * TOK_TM

LOG2E = 1.4426950408889634
NEG_BIG = -1e30
SB_SPLIT_TERMS = 1
SB_DEAD_BITS = 150.0


def _cparams(*sem):
    return pltpu.CompilerParams(dimension_semantics=sem, vmem_limit_bytes=VMEM_LIMIT)


def _dot(a, b):
    return jnp.dot(a, b, preferred_element_type=F32)


def _dot_nt(a, b):
    return lax.dot_general(a, b, (((1,), (1,)), ((), ())), preferred_element_type=F32)


def _dot_tn(a, b):
    return lax.dot_general(a, b, (((0,), (0,)), ((), ())), preferred_element_type=F32)


def _rms(xf, g):
    return xf * lax.rsqrt(jnp.mean(xf * xf, axis=-1, keepdims=True) + EPS) * g


def _sigmoid(x):
    return 1.0 / (1.0 + jnp.exp(-x))


def _softplus(x):
    return jnp.maximum(x, 0.0) + jnp.log(1.0 + jnp.exp(-jnp.abs(x)))


def _split_dot(x, m_bf16, terms):
    out = None
    r = x
    for i in range(terms):
        p = r.astype(BF16)
        d = _dot(p, m_bf16)
        out = d if out is None else out + d
        if i + 1 < terms:
            r = r - p.astype(F32)
    return out


def _ffn_kernel(x_ref, g_ref, wg_ref, wu_ref, wd_ref, o_ref, h_ref, acc_ref):
    f = pl.program_id(1)

    @pl.when(f == 0)
    def _():
        h_ref[...] = _rms(x_ref[...], g_ref[...]).astype(BF16)
        acc_ref[...] = jnp.zeros_like(acc_ref)

    h = h_ref[...]
    gate = _dot(h, wg_ref[...])
    up = _dot(h, wu_ref[...])
    act = (gate * _sigmoid(gate) * up).astype(BF16)
    acc_ref[...] += _dot(act, wd_ref[...])

    @pl.when(f == pl.num_programs(1) - 1)
    def _():
        o_ref[...] = x_ref[...] + 0.5 * acc_ref[...]


def _ffn(x, g, wg, wu, wd):
    t, d = x.shape
    tm, tf = min(FFN_TM, t), FFN_TF
    nf = D_FF // tf
    return pl.pallas_call(
        _ffn_kernel,
        out_shape=jax.ShapeDtypeStruct((t, d), F32),
        grid=(t // tm, nf),
        in_specs=[
            pl.BlockSpec((tm, d), lambda i, f: (i, 0)),
            pl.BlockSpec((1, d), lambda i, f: (0, 0)),
            pl.BlockSpec((d, tf), lambda i, f: (0, f)),
            pl.BlockSpec((d, tf), lambda i, f: (0, f)),
            pl.BlockSpec((tf, d), lambda i, f: (f, 0)),
        ],
        out_specs=pl.BlockSpec((tm, d), lambda i, f: (i, 0)),
        scratch_shapes=[pltpu.VMEM((tm, d), BF16), pltpu.VMEM((tm, d), F32)],
        compiler_params=_cparams("parallel", "arbitrary"),
        name="ffn",
    )(x, g, wg, wu, wd)


def _ple_kernel(x_ref, g_ref, wgate_ref, p_ref, wp_ref, fin_ref, o_ref, *, final):
    x = x_ref[...]
    h = _rms(x, g_ref[...]).astype(BF16)
    gate = _sigmoid(_dot(h, wgate_ref[...]))
    y = x + gate * _dot(p_ref[...].astype(BF16), wp_ref[...])
    if final:
        y = _rms(y, fin_ref[...])
    o_ref[...] = y


def _ple(x, g, wgate, p, wp, fin, final):
    t, d = x.shape
    tm = min(TOK_TM, t)
    return pl.pallas_call(
        functools.partial(_ple_kernel, final=final),
        out_shape=jax.ShapeDtypeStruct((t, d), F32),
        grid=(t // tm,),
        in_specs=[
            pl.BlockSpec((tm, d), lambda i: (i, 0)),
            pl.BlockSpec((1, d), lambda i: (0, 0)),
            pl.BlockSpec((d, d), lambda i: (0, 0)),
            pl.BlockSpec((tm, PLE_DIM), lambda i: (i, 0)),
            pl.BlockSpec((PLE_DIM, d), lambda i: (0, 0)),
            pl.BlockSpec((1, d), lambda i: (0, 0)),
        ],
        out_specs=pl.BlockSpec((tm, d), lambda i: (i, 0)),
        compiler_params=_cparams("parallel"),
        name="ple",
    )(x, g, wgate, p, wp, fin)


def _norm_matmul_kernel(x_ref, g_ref, w_ref, o_ref, h_ref):
    @pl.when(pl.program_id(1) == 0)
    def _():
        h_ref[...] = _rms(x_ref[...], g_ref[...]).astype(BF16)

    o_ref[...] = _dot(h_ref[...], w_ref[...]).astype(o_ref.dtype)


def _norm_matmul(x, g, w, tn, out_dtype):
    t, d = x.shape
    n = w.shape[1]
    tm = min(TOK_TM, t)
    return pl.pallas_call(
        _norm_matmul_kernel,
        out_shape=jax.ShapeDtypeStruct((t, n), out_dtype),
        grid=(t // tm, n // tn),
        in_specs=[
            pl.BlockSpec((tm, d), lambda i, j: (i, 0)),
            pl.BlockSpec((1, d), lambda i, j: (0, 0)),
            pl.BlockSpec((d, tn), lambda i, j: (0, j)),
        ],
        out_specs=pl.BlockSpec((tm, tn), lambda i, j: (i, j)),
        scratch_shapes=[pltpu.VMEM((tm, d), BF16)],
        compiler_params=_cparams("parallel", "arbitrary"),
        name="norm_matmul",
    )(x, g, w)


def _matmul_res_kernel(a_ref, w_ref, x_ref, o_ref):
    o_ref[...] = x_ref[...] + _dot(a_ref[...].astype(BF16), w_ref[...])


def _matmul_res(a, w, x):
    t, k = a.shape
    d = w.shape[1]
    tm = min(TOK_TM, t)
    return pl.pallas_call(
        _matmul_res_kernel,
        out_shape=jax.ShapeDtypeStruct((t, d), F32),
        grid=(t // tm,),
        in_specs=[
            pl.BlockSpec((tm, k), lambda i: (i, 0)),
            pl.BlockSpec((k, d), lambda i: (0, 0)),
            pl.BlockSpec((tm, d), lambda i: (i, 0)),
        ],
        out_specs=pl.BlockSpec((tm, d), lambda i: (i, 0)),
        compiler_params=_cparams("parallel"),
        name="matmul_res",
    )(a, w, x)


def _sb_kernel(q_ref, k_ref, v_ref, u_ref, o_ref,
               z0, z1, lb0, lb1, sf0, sf1, a0, a1, run_a, run_b, acc_o, *, tq, tk):
    qi = pl.program_id(2)
    ratio = tq // tk
    n_tiles = ratio * (qi + 1)
    q = q_ref[0]
    u2 = u_ref[...]
    zb, lbb, sfb, ab = (z0, z1), (lb0, lb1), (sf0, sf1), (a0, a1)
    first = lax.broadcasted_iota(jnp.int32, (tk, LANES), 1) < SB_HD
    row_minus_col = (lax.broadcasted_iota(jnp.int32, (tq, 2 * tk), 0)
                     - (lax.broadcasted_iota(jnp.int32, (tq, 2 * tk), 1) & (tk - 1)))
    acc_o[...] = jnp.zeros_like(acc_o)
    run_a[...] = jnp.zeros_like(run_a)
    run_b[...] = jnp.zeros_like(run_b)

    def blockdiag(t):
        zero = jnp.zeros_like(t)
        return jnp.concatenate([jnp.where(first, t, zero), jnp.where(first, zero, t)], axis=0)

    def key_start(t):
        kb = jnp.maximum(ratio * qi + ratio - 1 - t, 0)
        return pl.multiple_of(kb * tk, tk)

    def stage_q(t, par):
        kk = blockdiag(k_ref[0, pl.ds(key_start(t), tk), :])
        zb[par][...] = _dot_nt(q, kk)

    def stage_s1(par, thr):
        z = zb[par][...]
        neg_abs = lax.bitcast_convert_type(
            lax.bitcast_convert_type(z, jnp.uint32) | jnp.uint32(0x80000000), F32)
        sp = jnp.maximum(z, 0.0) + jnp.log(1.0 + jnp.exp2(neg_abs)) * LOG2E
        ra, rb = run_a[...], run_b[...]
        lb = z - sp - jnp.concatenate([ra, rb], axis=1)
        if thr is not None:
            valid = row_minus_col > thr
            sp = jnp.where(valid, sp, 0.0)
            lb = jnp.where(valid, lb, NEG_BIG)
        lbb[par][...] = lb
        sfb[par][...] = _split_dot(sp, u2, SB_SPLIT_TERMS)
        run_a[...] = ra + jnp.sum(sp[:, :tk], axis=1, keepdims=True)
        run_b[...] = rb + jnp.sum(sp[:, tk:], axis=1, keepdims=True)

    def stage_s2(par):
        ab[par][...] = jnp.exp2(lbb[par][...] - sfb[par][...]).astype(BF16)

    def stage_p(t, par):
        vv = blockdiag(v_ref[0, pl.ds(key_start(t), tk), :])
        acc_o[...] += _dot(ab[par][...], vv)

    def step(u, par, thr=None, q_=True, s1=True, s2=True, p_=True):
        if q_:
            stage_q(u, par)
        if s1:
            stage_s1(1 - par, thr)
        if s2:
            stage_s2(par)
        if p_:
            stage_p(u - 3, 1 - par)

    for u in range(ratio + 1):
        step(u, u % 2, thr=(ratio - u) * tk, s1=u >= 1, s2=u >= 2, p_=u >= 3)

    n_pairs = (n_tiles - ratio) // 2

    def body(c):
        j, _ = c
        u = ratio + 1 + 2 * j
        step(u, (ratio + 1) % 2)
        step(u + 1, ratio % 2)
        return j + 1, jnp.min(jnp.minimum(run_a[...], run_b[...]))

    def live(c):
        j, lowest = c
        return (j < n_pairs) & (lowest < SB_DEAD_BITS)

    j_end, _ = lax.while_loop(live, body, (jnp.int32(0), jnp.float32(0.0)))
    n_done = ratio + 2 * j_end
    step(n_done + 1, (ratio + 1) % 2, q_=False, s1=False)
    step(n_done + 2, ratio % 2, q_=False, s1=False, s2=False)
    o_ref[0] = acc_o[...].astype(o_ref.dtype)


def _sb_attention(qkv, u2):
    b, l, _ = qkv.shape
    tq, tk = min(SB_TQ, l), SB_TK
    assert (tq // tk) % 2 == 0 and tk == LANES
    npair = SB_HEADS * SB_HD // LANES
    wide = pltpu.VMEM((tq, 2 * tk), F32)
    col = pltpu.VMEM((tq, LANES), F32)
    return pl.pallas_call(
        functools.partial(_sb_kernel, tq=tq, tk=tk),
        out_shape=jax.ShapeDtypeStruct((b, l, SB_HEADS * SB_HD), BF16),
        grid=(b, npair, l // tq),
        in_specs=[
            pl.BlockSpec((1, tq, LANES), lambda bi, j, qi: (bi, qi, j)),
            pl.BlockSpec((1, l, LANES), lambda bi, j, qi: (bi, 0, npair + j)),
            pl.BlockSpec((1, l, LANES), lambda bi, j, qi: (bi, 0, 2 * npair + j)),
            pl.BlockSpec((2 * tk, 2 * tk), lambda bi, j, qi: (0, 0)),
        ],
        out_specs=pl.BlockSpec((1, tq, LANES), lambda bi, j, qi: (bi, qi, j)),
        scratch_shapes=[wide, wide, wide, wide, wide, wide,
                        pltpu.VMEM((tq, 2 * tk), BF16), pltpu.VMEM((tq, 2 * tk), BF16),
                        col, col, pltpu.VMEM((tq, LANES), F32)],
        compiler_params=_cparams("parallel", "parallel", "arbitrary"),
        name="sb_attention",
    )(qkv, qkv, qkv, u2)


def _sb_suffix_matrix(tk):
    s = np.arange(2 * tk)
    same = (s[:, None] // tk) == (s[None, :] // tk)
    return jnp.asarray((same & (s[:, None] > s[None, :])).astype(np.float32), BF16)


def _mla_proj_kernel(x_ref, g_ref, wq_ref, wkv_ref, wkr_ref, qn_ref, wuq_ref, kvn_ref, wuk_ref, wuvt_ref,
                     pos_ref, invf_ref, q_out, kn_out, kr_out, vt_out):
    h = _rms(x_ref[...], g_ref[...]).astype(BF16)
    cq = _dot(h, wq_ref[...])
    ckv = _dot(h, wkv_ref[...])
    kr = _dot(h, wkr_ref[...])
    q = _dot(_rms(cq, qn_ref[...]).astype(BF16), wuq_ref[...])
    ckv_n = _rms(ckv, kvn_ref[...]).astype(BF16)
    kn_out[...] = _dot(ckv_n, wuk_ref[...]).astype(kn_out.dtype)
    vt_out[0] = _dot_nt(wuvt_ref[...], ckv_n).astype(vt_out.dtype)

    ang = pos_ref[...].astype(F32) * invf_ref[...]
    cos, sin = jnp.cos(ang), jnp.sin(ang)
    lane = lax.broadcasted_iota(jnp.int32, ang.shape, 1)
    half = MLA_ROPE // 2
    sin_lo = jnp.where(lane < half, -sin, 0.0)
    sin_hi = jnp.where((lane >= half) & (lane < MLA_ROPE), sin, 0.0)

    def rope(t):
        return (t * cos + pltpu.roll(t, LANES - half, 1) * sin_lo + pltpu.roll(t, half, 1) * sin_hi)

    kr_out[...] = rope(kr).astype(kr_out.dtype)
    scale = (MLA_NOPE + MLA_ROPE) ** -0.5 * LOG2E
    for hh in range(MLA_HEADS):
        base = hh * 2 * LANES
        q_out[:, base:base + LANES] = (q[:, base:base + LANES] * scale).astype(q_out.dtype)
        q_out[:, base + LANES:base + 2 * LANES] = (
            rope(q[:, base + LANES:base + 2 * LANES]) * scale).astype(q_out.dtype)


def _mla_proj(x, g, wq, wkv, wkr, qn, wuq, kvn, wuk, wuvt, pos, invf):
    t, d = x.shape
    tm = min(TOK_TM, t)
    full = lambda a: pl.BlockSpec(a.shape, lambda i: (0, 0))
    nq = MLA_HEADS * 2 * LANES
    nv = MLA_HEADS * MLA_V
    return pl.pallas_call(
        _mla_proj_kernel,
        out_shape=(jax.ShapeDtypeStruct((t, nq), BF16), jax.ShapeDtypeStruct((t, MLA_HEADS * MLA_NOPE), BF16),
                   jax.ShapeDtypeStruct((t, LANES), BF16), jax.ShapeDtypeStruct((t // tm, nv, tm), BF16)),
        grid=(t // tm,),
        in_specs=[pl.BlockSpec((tm, d), lambda i: (i, 0)), full(g), full(wq), full(wkv), full(wkr),
                  full(qn), full(wuq), full(kvn), full(wuk), full(wuvt),
                  pl.BlockSpec((tm, 1), lambda i: (i, 0)), full(invf)],
        out_specs=(pl.BlockSpec((tm, nq), lambda i: (i, 0)),
                   pl.BlockSpec((tm, MLA_HEADS * MLA_NOPE), lambda i: (i, 0)),
                   pl.BlockSpec((tm, LANES), lambda i: (i, 0)),
                   pl.BlockSpec((1, nv, tm), lambda i: (i, 0, 0))),
        compiler_params=_cparams("parallel"),
        name="mla_proj",
    )(x, g, wq, wkv, wkr, qn, wuq, kvn, wuk, wuvt, pos, invf)


def _mla_attn_kernel(q_ref, kn_ref, kr_ref, vt_ref, o_ref, s0, s1, p0, p1, c0, c1, m_s, l_s, acc_s, *, tq, tk):
    qi = pl.program_id(2)
    q = q_ref[0]
    sb, pb, cb = (s0, s1), (p0, p1), (c0, c1)
    m_s[...] = jnp.full_like(m_s, -jnp.inf)
    l_s[...] = jnp.zeros_like(l_s)
    acc_s[...] = jnp.zeros_like(acc_s)
    p1[...] = jnp.zeros_like(p1)
    c1[...] = jnp.ones_like(c1)
    query_minus_key = (lax.broadcasted_iota(jnp.int32, (tk, tq), 1)
                       - lax.broadcasted_iota(jnp.int32, (tk, tq), 0))

    def tile_index(t):
        return jnp.maximum(t, 0)

    def stage_q(t, par):
        start = pl.multiple_of(tile_index(t) * tk, tk)
        kc = jnp.concatenate([kn_ref[0, pl.ds(start, tk), :], kr_ref[0, pl.ds(start, tk), :]], axis=1)
        sb[par][...] = _dot_nt(kc, q)

    def stage_s(par, thr):
        s = sb[par][...]
        if thr is not None:
            s = jnp.where(query_minus_key >= thr, s, -jnp.inf)
        m_prev = m_s[...]
        m_new = jnp.maximum(m_prev, jnp.max(s, axis=0, keepdims=True))
        p = jnp.exp2(s - m_new)
        corr = jnp.exp2(m_prev - m_new)
        pb[par][...] = p.astype(BF16)
        cb[par][...] = corr
        l_s[...] = l_s[...] * corr + jnp.sum(p, axis=0, keepdims=True)
        m_s[...] = m_new

    def stage_p(t, par):
        acc_s[...] = acc_s[...] * cb[par][...] + _dot(vt_ref[0, tile_index(t)], pb[par][...])

    stage_q(0, 0)

    def body(j, c):
        t = 2 * j
        stage_q(t + 1, 1)
        stage_s(0, None)
        stage_p(t - 1, 1)
        stage_q(t + 2, 0)
        stage_s(1, None)
        stage_p(t, 0)
        return c

    lax.fori_loop(0, qi, body, 0)
    d0 = 2 * qi
    stage_q(d0 + 1, 1)
    stage_s(0, 0)
    stage_p(d0 - 1, 1)
    stage_s(1, tk)
    stage_p(d0, 0)
    stage_p(d0 + 1, 1)
    o_ref[0] = (acc_s[...] / l_s[...]).T.astype(o_ref.dtype)


def _mla_attention(qc, kn, kr, vt):
    b, l, _ = qc.shape
    tq, tk = min(MLA_TQ, l), vt.shape[-1]
    assert tq == 2 * tk and tk % LANES == 0 and MLA_V == LANES and MLA_NOPE == LANES
    row = pltpu.VMEM((1, tq), F32)
    return pl.pallas_call(
        functools.partial(_mla_attn_kernel, tq=tq, tk=tk),
        out_shape=jax.ShapeDtypeStruct((b, l, MLA_HEADS * MLA_V), BF16),
        grid=(b, MLA_HEADS, l // tq),
        in_specs=[
            pl.BlockSpec((1, tq, 2 * LANES), lambda bi, hh, qi: (bi, qi, hh)),
            pl.BlockSpec((1, l, LANES), lambda bi, hh, qi: (bi, 0, hh)),
            pl.BlockSpec((1, l, LANES), lambda bi, hh, qi: (bi, 0, 0)),
            pl.BlockSpec((1, l // tk, MLA_V, tk), lambda bi, hh, qi: (bi, 0, hh, 0)),
        ],
        out_specs=pl.BlockSpec((1, tq, LANES), lambda bi, hh, qi: (bi, qi, hh)),
        scratch_shapes=[pltpu.VMEM((tk, tq), F32), pltpu.VMEM((tk, tq), F32),
                        pltpu.VMEM((tk, tq), BF16), pltpu.VMEM((tk, tq), BF16),
                        row, row, row, row, pltpu.VMEM((MLA_V, tq), F32)],
        compiler_params=_cparams("parallel", "parallel", "arbitrary"),
        name="mla_attention",
    )(qc, kn, kr, vt)


GDN_TM = 256
GDN_GROUP = 4
GDN_STEP_CHUNKS = 2


def _gdn_in_kernel(x_ref, g_ref, wqkv_ref, wz_ref, wba_ref, cw_ref, alog_ref, dtb_ref, tri_ref,
                   q_out, k_out, v_out, z_out, bw_out, gw_out, bg_out, pre_s, *, tm, tiles_per_seq):
    i = pl.program_id(0)
    h = _rms(x_ref[...], g_ref[...]).astype(BF16)

    @pl.when(i % tiles_per_seq == 0)
    def _():
        pre_s[0:8, :] = jnp.zeros((8, GDN_CONV_DIM), F32)

    ba = _dot(h, wba_ref[...])
    lane = lax.broadcasted_iota(jnp.int32, ba.shape, 1)
    beta = _sigmoid(ba)
    gval = -jnp.exp(alog_ref[...]) * _softplus(ba + dtb_ref[...])
    gcum = _split_dot_left(tri_ref[...], gval, 3)
    bg = jnp.where(lane < GDN_HEADS, beta, gcum)
    bg_out[...] = bg
    for hh in range(GDN_HEADS):
        sl = slice(hh * GDN_DK, (hh + 1) * GDN_DK)
        bw_out[:, sl] = jnp.broadcast_to(bg[:, hh:hh + 1], (tm, GDN_DK))
        gw_out[:, sl] = jnp.broadcast_to(bg[:, GDN_HEADS + hh:GDN_HEADS + hh + 1], (tm, GDN_DK))

    cw = cw_ref[...]
    width = 2 * GDN_DK
    for blk in range(GDN_CONV_DIM // width):
        cs = slice(blk * width, (blk + 1) * width)
        pre = _dot(h, wqkv_ref[:, cs])
        pre_s[8:8 + tm, cs] = pre
        conv = pre * cw[3:4, cs]
        for j in range(1, GDN_CONV):
            conv = conv + pre_s[8 - j:8 - j + tm, cs] * cw[3 - j:4 - j, cs]
        pre_s[0:8, cs] = pre_s[tm:tm + 8, cs]
        y = conv * _sigmoid(conv)
        if blk * width >= 2 * GDN_QK:
            v_out[:, blk * width - 2 * GDN_QK:(blk + 1) * width - 2 * GDN_QK] = y
            continue
        is_q = blk * width < GDN_QK
        out = q_out if is_q else k_out
        for half in range(2):
            yh = y[:, half * GDN_DK:(half + 1) * GDN_DK]
            yn = yh * lax.rsqrt(jnp.sum(yh * yh, axis=1, keepdims=True) + EPS)
            col = (blk * width) % GDN_QK + half * GDN_DK
            out[:, col:col + GDN_DK] = yn * GDN_DK ** -0.5 if is_q else yn

    for blk in range(GDN_V // width):
        cs = slice(blk * width, (blk + 1) * width)
        z = _dot(h, wz_ref[:, cs])
        z_out[:, cs] = z * _sigmoid(z)


def _split_dot_left(m_bf16, x, terms):
    out = None
    r = x
    for i in range(terms):
        p = r.astype(BF16)
        d = _dot(m_bf16, p)
        out = d if out is None else out + d
        if i + 1 < terms:
            r = r - p.astype(F32)
    return out


def _gdn_in(x, g, wqkv, wz, wba, cw, alog, dtb, tri, seq):
    t, d = x.shape
    tm = min(GDN_TM, seq)
    full = lambda a: pl.BlockSpec(a.shape, lambda i: (0, 0))
    row = lambda n: pl.BlockSpec((tm, n), lambda i: (i, 0))
    wide = jax.ShapeDtypeStruct((t, GDN_QK), F32)
    return pl.pallas_call(
        functools.partial(_gdn_in_kernel, tm=tm, tiles_per_seq=seq // tm),
        out_shape=(wide, wide, wide, wide, wide, wide, jax.ShapeDtypeStruct((t, LANES), F32)),
        grid=(t // tm,),
        in_specs=[row(d), full(g), full(wqkv), full(wz), full(wba), full(cw), full(alog), full(dtb), full(tri)],
        out_specs=(row(GDN_QK),) * 6 + (row(LANES),),
        scratch_shapes=[pltpu.VMEM((tm + 8, GDN_CONV_DIM), F32)],
        compiler_params=_cparams("arbitrary"),
        name="gdn_in",
    )(x, g, wqkv, wz, wba, cw, alog, dtb, tri)


def _gdn_chunk_kernel(q_ref, k_ref, v_ref, bw_ref, gw_ref, gr_ref, o_ref, s_ref, *, nchunk):
    c, grp = GDN_CHUNK, GDN_GROUP
    rows = grp * c

    @pl.when(pl.program_id(1) == 0)
    def _():
        s_ref[...] = jnp.zeros_like(s_ref)

    r = lax.broadcasted_iota(jnp.int32, (rows, rows), 0)
    cc = lax.broadcasted_iota(jnp.int32, (rows, rows), 1)
    same = lax.shift_right_logical(r, 6) == lax.shift_right_logical(cc, 6)
    incl = same & (r >= cc)
    strict = same & (r > cc)
    own_block = (lax.shift_right_logical(lax.broadcasted_iota(jnp.int32, (rows, grp * GDN_DK), 0), 6)
                 == lax.shift_right_logical(lax.broadcasted_iota(jnp.int32, (rows, grp * GDN_DK), 1), 7))

    def spread(t):
        t16 = t.astype(BF16)
        return jnp.where(own_block, jnp.concatenate([t16] * grp, axis=1), jnp.zeros((), BF16))

    ngrp = GDN_HEADS // grp
    units = [(ci, g) for ci in range(nchunk) for g in range(ngrp)]

    def stack(ref, ci, g):
        return jnp.concatenate(
            [ref[0, ci * c:(ci + 1) * c, (g * grp + j) * GDN_DK:(g * grp + j + 1) * GDN_DK] for j in range(grp)],
            axis=0)

    ks = [stack(k_ref, ci, g) for ci, g in units]
    gw = [stack(gw_ref, ci, g) for ci, g in units]
    bw = [stack(bw_ref, ci, g) for ci, g in units]
    k16 = [t.astype(BF16) for t in ks]
    kb = [ks[i] * bw[i] for i in range(len(units))]
    eg = [jnp.exp(t) for t in gw]
    decay = []
    for i, (ci, g) in enumerate(units):
        gdiff = jnp.concatenate([gw[i], gw[i]], axis=1) - gr_ref[0, ci, g:g + 1, :]
        decay.append(jnp.where(incl, jnp.exp(jnp.where(incl, gdiff, 0.0)), 0.0))
    p = [jnp.where(strict, _dot_nt(kb[i].astype(BF16), k16[i]) * decay[i], 0.0).astype(BF16)
         for i in range(len(units))]
    y = [jnp.concatenate([stack(v_ref, ci, g) * bw[i], kb[i] * eg[i]], axis=1)
         for i, (ci, g) in enumerate(units)]
    y = [y[i] - _dot(p[i], y[i].astype(BF16)) for i in range(len(units))]
    step = 2
    while step < c:
        p = [_dot(t, t).astype(BF16) for t in p]
        y = [y[i] + _dot(p[i], y[i].astype(BF16)) for i in range(len(units))]
        step *= 2
    qs = [stack(q_ref, ci, g) for ci, g in units]
    attn = [(_dot_nt(qs[i].astype(BF16), k16[i]) * decay[i]).astype(BF16) for i in range(len(units))]

    for ci in range(nchunk):
        idx = [ci * ngrp + g for g in range(ngrp)]
        s = [s_ref[g] for g in range(ngrp)]
        ws_qs = [_dot(jnp.concatenate([spread(y[i][:, GDN_DV:]), spread(qs[i] * eg[i])], axis=0),
                      s[g].astype(BF16)) for g, i in enumerate(idx)]
        v16 = [(y[i][:, :GDN_DV] - ws_qs[g][:rows]).astype(BF16) for g, i in enumerate(idx)]
        o = [ws_qs[g][rows:] + _dot(attn[i], v16[g]) for g, i in enumerate(idx)]
        for g, i in enumerate(idx):
            last = [gw[i][j * c + c - 1:j * c + c, :] for j in range(grp)]
            g_last = jnp.concatenate([jnp.broadcast_to(t, (c, GDN_DK)) for t in last], axis=0)
            s_decay = jnp.concatenate([jnp.broadcast_to(jnp.exp(t), (GDN_DK, GDN_DV)) for t in last], axis=0)
            kd = ks[i] * jnp.exp(g_last - gw[i])
            s_ref[g] = s[g] * s_decay + _dot_tn(spread(kd), v16[g])
            for j in range(grp):
                o_ref[0, ci * c:(ci + 1) * c, (g * grp + j) * GDN_DV:(g * grp + j + 1) * GDN_DV] = (
                    o[g][j * c:(j + 1) * c])


def _gdn_chunks(q, k, v, bw, gw, grow):
    b, l, n = q.shape
    nchunk = GDN_STEP_CHUNKS
    tl = nchunk * GDN_CHUNK
    ngrp = GDN_HEADS // GDN_GROUP
    blk = pl.BlockSpec((1, tl, n), lambda bi, ci: (bi, ci, 0))
    return pl.pallas_call(
        functools.partial(_gdn_chunk_kernel, nchunk=nchunk),
        out_shape=jax.ShapeDtypeStruct((b, l, n), F32),
        grid=(b, l // tl),
        in_specs=[blk, blk, blk, blk, blk,
                  pl.BlockSpec((1, nchunk, ngrp, GDN_GROUP * GDN_CHUNK), lambda bi, ci: (bi, ci, 0, 0))],
        out_specs=blk,
        scratch_shapes=[pltpu.VMEM((ngrp, GDN_GROUP * GDN_DK, GDN_DV), F32)],
        compiler_params=_cparams("parallel", "arbitrary"),
        name="gdn_chunks",
    )(q, k, v, bw, gw, grow)


def _gdn_out_kernel(o_ref, z_ref, gn_ref, w_ref, x_ref, out_ref, a_s):
    gn = gn_ref[...]
    for hh in range(GDN_HEADS):
        sl = slice(hh * GDN_DV, (hh + 1) * GDN_DV)
        a_s[:, sl] = (_rms(o_ref[:, sl], gn) * z_ref[:, sl]).astype(BF16)
    out_ref[...] = x_ref[...] + _dot(a_s[...], w_ref[...])


def _gdn_out(o, zs, gn, w, x):
    t, d = x.shape
    tm = min(TOK_TM, t)
    row = lambda n: pl.BlockSpec((tm, n), lambda i: (i, 0))
    full = lambda a: pl.BlockSpec(a.shape, lambda i: (0, 0))
    return pl.pallas_call(
        _gdn_out_kernel,
        out_shape=jax.ShapeDtypeStruct((t, d), F32),
        grid=(t // tm,),
        in_specs=[row(GDN_V), row(GDN_V), full(gn), full(w), row(d)],
        out_specs=row(d),
        scratch_shapes=[pltpu.VMEM((tm, GDN_V), BF16)],
        compiler_params=_cparams("parallel"),
        name="gdn_out",
    )(o, zs, gn, w, x)


def _chunk_tri(tm):
    t = np.arange(tm)
    same = (t[:, None] // GDN_CHUNK) == (t[None, :] // GDN_CHUNK)
    return jnp.asarray((same & (t[:, None] >= t[None, :])).astype(np.float32), BF16)


def _row(v):
    return v.reshape(1, -1).astype(F32)


def _pad_lanes(v, offset):
    out = jnp.zeros((1, LANES), F32)
    return out.at[0, offset:offset + v.shape[0]].set(v.astype(F32))


def _gated_deltanet(x, norm_g, w_in, conv_w, a_log, dt_bias, out_norm, w_out, batch, seq):
    t = x.shape[0]
    wqkv = w_in[:, :GDN_CONV_DIM].astype(BF16)
    wz = w_in[:, GDN_CONV_DIM:GDN_CONV_DIM + GDN_V].astype(BF16)
    wba = jnp.pad(w_in[:, GDN_CONV_DIM + GDN_V:], ((0, 0), (0, LANES - 2 * GDN_HEADS))).astype(BF16)
    q, k, v, zs, bw, gw, bg = _gdn_in(x, _row(norm_g), wqkv, wz, wba, conv_w.astype(F32),
                                      _pad_lanes(a_log, GDN_HEADS), _pad_lanes(dt_bias, GDN_HEADS),
                                      _chunk_tri(min(GDN_TM, seq)), seq)
    nc = seq // GDN_CHUNK
    grow = bg[:, GDN_HEADS:2 * GDN_HEADS].reshape(batch, nc, GDN_CHUNK, GDN_HEADS).transpose(0, 1, 3, 2)
    grow = grow.reshape(batch, nc, GDN_HEADS // GDN_GROUP, GDN_GROUP * GDN_CHUNK)
    shp = (batch, seq, GDN_QK)
    o = _gdn_chunks(q.reshape(shp), k.reshape(shp), v.reshape(shp), bw.reshape(shp), gw.reshape(shp), grow)
    return _gdn_out(o.reshape(t, GDN_V), zs, _row(out_norm), w_out.astype(BF16), x)


def _stick_breaking(x, norm_g, w_qkv, w_out, batch, seq):
    t = x.shape[0]
    n = SB_HEADS * SB_HD
    w = jnp.concatenate([w_qkv[:, :n] * (SB_HD ** -0.5 * LOG2E), w_qkv[:, n:]], axis=1).astype(BF16)
    qkv = _norm_matmul(x, _row(norm_g), w, 1024, BF16)
    o = _sb_attention(qkv.reshape(batch, seq, 3 * n), _sb_suffix_matrix(SB_TK))
    return _matmul_res(o.reshape(t, n), w_out.astype(BF16), x)


def _mla(x, norm_g, positions, w_in, q_norm, w_uq, kv_norm, w_ukv, w_out, batch, seq):
    t = x.shape[0]
    wq = w_in[:, :MLA_Q_RANK].astype(BF16)
    wkv = w_in[:, MLA_Q_RANK:MLA_Q_RANK + MLA_KV_RANK].astype(BF16)
    wkr = jnp.pad(w_in[:, MLA_Q_RANK + MLA_KV_RANK:], ((0, 0), (0, LANES - MLA_ROPE))).astype(BF16)
    qd = MLA_NOPE + MLA_ROPE
    wuq = jnp.pad(w_uq.reshape(MLA_Q_RANK, MLA_HEADS, qd), ((0, 0), (0, 0), (0, 2 * LANES - qd)))
    wuq = wuq.reshape(MLA_Q_RANK, MLA_HEADS * 2 * LANES).astype(BF16)
    half = MLA_ROPE // 2
    inv_freq = ROPE_THETA ** (-jnp.arange(half, dtype=F32) / half)
    invf = jnp.zeros((1, LANES), F32).at[0, :MLA_ROPE].set(jnp.concatenate([inv_freq, inv_freq]))
    wukv = w_ukv.reshape(MLA_KV_RANK, MLA_HEADS, MLA_NOPE + MLA_V)
    wuk = wukv[:, :, :MLA_NOPE].reshape(MLA_KV_RANK, MLA_HEADS * MLA_NOPE).astype(BF16)
    wuvt = wukv[:, :, MLA_NOPE:].reshape(MLA_KV_RANK, MLA_HEADS * MLA_V).T.astype(BF16)
    qc, kn, kr, vt = _mla_proj(x, _row(norm_g), wq, wkv, wkr, _row(q_norm), wuq, _row(kv_norm),
                               wuk, wuvt, positions.reshape(t, 1), invf)
    tk = vt.shape[-1]
    o = _mla_attention(qc.reshape(batch, seq, MLA_HEADS * 2 * LANES), kn.reshape(batch, seq, MLA_HEADS * MLA_NOPE),
                       kr.reshape(batch, seq, LANES), vt.reshape(batch, seq // tk, MLA_HEADS * MLA_V, tk))
    return _matmul_res(o.reshape(t, MLA_HEADS * MLA_V), w_out.astype(BF16), x)


def kernel(x, p, positions, ffn1_norm, ffn1_w_gate, ffn1_w_up, ffn1_w_down, mix_norm, ffn2_norm, ffn2_w_gate, ffn2_w_up, ffn2_w_down, ple_norm, ple_w_gate, ple_w_proj, gdn_w_in, gdn_conv_w, gdn_a_log, gdn_dt_bias, gdn_out_norm, gdn_w_out, sb_w_qkv, sb_w_out, mla_w_in, mla_q_norm, mla_w_uq, mla_kv_norm, mla_w_ukv, mla_w_out, final_norm):
    batch, seq, d = x.shape
    t = batch * seq
    depth = ffn1_norm.shape[0]
    x = x.reshape(t, d)
    p = p.reshape(depth, t, PLE_DIM)
    for i in range(depth):
        mixer, slot = i % N_MIXERS, i // N_MIXERS
        x = _ffn(x, _row(ffn1_norm[i]), ffn1_w_gate[i].astype(BF16), ffn1_w_up[i].astype(BF16),
                 ffn1_w_down[i].astype(BF16))
        if mixer == 0:
            x = _gated_deltanet(x, mix_norm[i], gdn_w_in[slot], gdn_conv_w[slot], gdn_a_log[slot],
                                gdn_dt_bias[slot], gdn_out_norm[slot], gdn_w_out[slot], batch, seq)
        elif mixer == 1:
            x = _stick_breaking(x, mix_norm[i], sb_w_qkv[slot], sb_w_out[slot], batch, seq)
        else:
            x = _mla(x, mix_norm[i], positions, mla_w_in[slot], mla_q_norm[slot], mla_w_uq[slot],
                     mla_kv_norm[slot], mla_w_ukv[slot], mla_w_out[slot], batch, seq)
        x = _ffn(x, _row(ffn2_norm[i]), ffn2_w_gate[i].astype(BF16), ffn2_w_up[i].astype(BF16),
                 ffn2_w_down[i].astype(BF16))
        x = _ple(x, _row(ple_norm[i]), ple_w_gate[i].astype(BF16), p[i], ple_w_proj[i].astype(BF16),
                 _row(final_norm), final=(i == depth - 1))
    return x.reshape(batch, seq, d)
```

```python
import functools

import numpy as np
import jax
import jax.numpy as jnp
from jax import lax
from jax.experimental import pallas as pl
from jax.experimental.pallas import tpu as pltpu

F32 = jnp.float32
BF16 = jnp.bfloat16

D_MODEL = 1024
DEPTH = 4
N_MIXERS = 3
PLE_DIM = 256
D_FF = 2816
EPS = 1e-6

GDN_HEADS = 8
GDN_DK = 128
GDN_DV = 128
GDN_CONV = 4
GDN_CHUNK = 64
GDN_QK = GDN_HEADS * GDN_DK
GDN_V = GDN_HEADS * GDN_DV
GDN_CONV_DIM = 2 * GDN_QK + GDN_V

SB_HEADS = 16
SB_HD = 64

MLA_HEADS = 8
MLA_Q_RANK = 384
MLA_KV_RANK = 256
MLA_NOPE = 128
MLA_ROPE = 64
MLA_V = 128
ROPE_THETA = 10000.0

LANES = 128
VMEM_LIMIT = 56 * 1024 * 1024

FFN_TM = 1024
FFN_CHUNK = 512
TOK_TM = 512
SB_TQ = 512
SB_TK = 128
MLA_TQ = 2 * TOK_TM

LOG2E = 1.4426950408889634
NEG_BIG = -1e30
SB_SPLIT_TERMS = 1
SB_DEAD_BITS = 150.0


def _cparams(*sem):
    return pltpu.CompilerParams(dimension_semantics=sem, vmem_limit_bytes=VMEM_LIMIT)


def _dot(a, b):
    return jnp.dot(a, b, preferred_element_type=F32)


def _dot_nt(a, b):
    return lax.dot_general(a, b, (((1,), (1,)), ((), ())), preferred_element_type=F32)


def _dot_tn(a, b):
    return lax.dot_general(a, b, (((0,), (0,)), ((), ())), preferred_element_type=F32)


def _rms(xf, g):
    return xf * lax.rsqrt(jnp.mean(xf * xf, axis=-1, keepdims=True) + EPS) * g


def _sigmoid(x):
    return 1.0 / (1.0 + jnp.exp(-x))


def _softplus(x):
    return jnp.maximum(x, 0.0) + jnp.log(1.0 + jnp.exp(-jnp.abs(x)))


def _split_dot(x, m_bf16, terms):
    out = None
    r = x
    for i in range(terms):
        p = r.astype(BF16)
        d = _dot(p, m_bf16)
        out = d if out is None else out + d
        if i + 1 < terms:
            r = r - p.astype(F32)
    return out


def _ffn_kernel(x_ref, g_ref, wg_ref, wu_ref, wd_ref, o_ref):
    x = x_ref[...]
    h = _rms(x, g_ref[...]).astype(BF16)
    acc = None
    for lo in range(0, D_FF, FFN_CHUNK):
        hi = min(lo + FFN_CHUNK, D_FF)
        gate = _dot(h, wg_ref[:, lo:hi])
        up = _dot(h, wu_ref[:, lo:hi])
        act = (gate * _sigmoid(gate) * up).astype(BF16)
        part = _dot(act, wd_ref[lo:hi, :])
        acc = part if acc is None else acc + part
    o_ref[...] = x + 0.5 * acc


def _ffn(x, g, wg, wu, wd):
    t, d = x.shape
    tm = min(FFN_TM, t)
    once = lambda a: pl.BlockSpec(a.shape, lambda i: (0, 0), pipeline_mode=pl.Buffered(1))
    return pl.pallas_call(
        _ffn_kernel,
        out_shape=jax.ShapeDtypeStruct((t, d), F32),
        grid=(t // tm,),
        in_specs=[pl.BlockSpec((tm, d), lambda i: (i, 0)), once(g), once(wg), once(wu), once(wd)],
        out_specs=pl.BlockSpec((tm, d), lambda i: (i, 0)),
        compiler_params=_cparams("parallel"),
        name="ffn",
    )(x, g, wg, wu, wd)


def _ple_kernel(x_ref, g_ref, wgate_ref, p_ref, wp_ref, fin_ref, o_ref, *, final):
    x = x_ref[...]
    h = _rms(x, g_ref[...]).astype(BF16)
    gate = _sigmoid(_dot(h, wgate_ref[...]))
    y = x + gate * _dot(p_ref[...].astype(BF16), wp_ref[...])
    if final:
        y = _rms(y, fin_ref[...])
    o_ref[...] = y


def _ple(x, g, wgate, p, wp, fin, final):
    t, d = x.shape
    tm = min(TOK_TM, t)
    return pl.pallas_call(
        functools.partial(_ple_kernel, final=final),
        out_shape=jax.ShapeDtypeStruct((t, d), F32),
        grid=(t // tm,),
        in_specs=[
            pl.BlockSpec((tm, d), lambda i: (i, 0)),
            pl.BlockSpec((1, d), lambda i: (0, 0)),
            pl.BlockSpec((d, d), lambda i: (0, 0)),
            pl.BlockSpec((tm, PLE_DIM), lambda i: (i, 0)),
            pl.BlockSpec((PLE_DIM, d), lambda i: (0, 0)),
            pl.BlockSpec((1, d), lambda i: (0, 0)),
        ],
        out_specs=pl.BlockSpec((tm, d), lambda i: (i, 0)),
        compiler_params=_cparams("parallel"),
        name="ple",
    )(x, g, wgate, p, wp, fin)


def _norm_matmul_kernel(x_ref, g_ref, w_ref, o_ref, *, tn):
    h = _rms(x_ref[...], g_ref[...]).astype(BF16)
    for lo in range(0, w_ref.shape[1], tn):
        o_ref[:, lo:lo + tn] = _dot(h, w_ref[:, lo:lo + tn]).astype(o_ref.dtype)


def _norm_matmul(x, g, w, tn, out_dtype):
    t, d = x.shape
    n = w.shape[1]
    tm = min(FFN_TM, t)
    once = lambda a: pl.BlockSpec(a.shape, lambda i: (0, 0), pipeline_mode=pl.Buffered(1))
    return pl.pallas_call(
        functools.partial(_norm_matmul_kernel, tn=tn),
        out_shape=jax.ShapeDtypeStruct((t, n), out_dtype),
        grid=(t // tm,),
        in_specs=[pl.BlockSpec((tm, d), lambda i: (i, 0)), once(g), once(w)],
        out_specs=pl.BlockSpec((tm, n), lambda i: (i, 0)),
        compiler_params=_cparams("parallel"),
        name="norm_matmul",
    )(x, g, w)


def _matmul_res_kernel(a_ref, w_ref, x_ref, o_ref):
    o_ref[...] = x_ref[...] + _dot(a_ref[...].astype(BF16), w_ref[...])


def _matmul_res(a, w, x):
    t, k = a.shape
    d = w.shape[1]
    tm = min(TOK_TM, t)
    return pl.pallas_call(
        _matmul_res_kernel,
        out_shape=jax.ShapeDtypeStruct((t, d), F32),
        grid=(t // tm,),
        in_specs=[
            pl.BlockSpec((tm, k), lambda i: (i, 0)),
            pl.BlockSpec((k, d), lambda i: (0, 0)),
            pl.BlockSpec((tm, d), lambda i: (i, 0)),
        ],
        out_specs=pl.BlockSpec((tm, d), lambda i: (i, 0)),
        compiler_params=_cparams("parallel"),
        name="matmul_res",
    )(a, w, x)


def _sb_kernel(q_ref, k_ref, v_ref, u_ref, o_ref,
               z0, z1, lb0, lb1, sf0, sf1, a0, a1, run_a, run_b, acc_o, *, tq, tk):
    qi = pl.program_id(2)
    ratio = tq // tk
    n_tiles = ratio * (qi + 1)
    q = q_ref[0]
    u2 = u_ref[...]
    zb, lbb, sfb, ab = (z0, z1), (lb0, lb1), (sf0, sf1), (a0, a1)
    first = lax.broadcasted_iota(jnp.int32, (tk, LANES), 1) < SB_HD
    row_minus_col = (lax.broadcasted_iota(jnp.int32, (tq, 2 * tk), 0)
                     - (lax.broadcasted_iota(jnp.int32, (tq, 2 * tk), 1) & (tk - 1)))
    acc_o[...] = jnp.zeros_like(acc_o)
    run_a[...] = jnp.zeros_like(run_a)
    run_b[...] = jnp.zeros_like(run_b)

    def blockdiag(t):
        zero = jnp.zeros_like(t)
        return jnp.concatenate([jnp.where(first, t, zero), jnp.where(first, zero, t)], axis=0)

    def key_start(t):
        kb = jnp.maximum(ratio * qi + ratio - 1 - t, 0)
        return pl.multiple_of(kb * tk, tk)

    def stage_q(t, par):
        kk = blockdiag(k_ref[0, pl.ds(key_start(t), tk), :])
        zb[par][...] = _dot_nt(q, kk)

    def stage_s1(par, thr):
        z = zb[par][...]
        neg_abs = lax.bitcast_convert_type(
            lax.bitcast_convert_type(z, jnp.uint32) | jnp.uint32(0x80000000), F32)
        sp = jnp.maximum(z, 0.0) + jnp.log(1.0 + jnp.exp2(neg_abs)) * LOG2E
        ra, rb = run_a[...], run_b[...]
        lb = z - sp - jnp.concatenate([ra, rb], axis=1)
        if thr is not None:
            valid = row_minus_col > thr
            sp = jnp.where(valid, sp, 0.0)
            lb = jnp.where(valid, lb, NEG_BIG)
        lbb[par][...] = lb
        sfb[par][...] = _split_dot(sp, u2, SB_SPLIT_TERMS)
        run_a[...] = ra + jnp.sum(sp[:, :tk], axis=1, keepdims=True)
        run_b[...] = rb + jnp.sum(sp[:, tk:], axis=1, keepdims=True)

    def stage_s2(par):
        ab[par][...] = jnp.exp2(lbb[par][...] - sfb[par][...]).astype(BF16)

    def stage_p(t, par):
        vv = blockdiag(v_ref[0, pl.ds(key_start(t), tk), :])
        acc_o[...] += _dot(ab[par][...], vv)

    def step(u, par, thr=None, q_=True, s1=True, s2=True, p_=True):
        if q_:
            stage_q(u, par)
        if s1:
            stage_s1(1 - par, thr)
        if s2:
            stage_s2(par)
        if p_:
            stage_p(u - 3, 1 - par)

    for u in range(ratio + 1):
        step(u, u % 2, thr=(ratio - u) * tk, s1=u >= 1, s2=u >= 2, p_=u >= 3)

    n_pairs = (n_tiles - ratio) // 2

    def body(c):
        j, _ = c
        u = ratio + 1 + 2 * j
        step(u, (ratio + 1) % 2)
        step(u + 1, ratio % 2)
        return j + 1, jnp.min(jnp.minimum(run_a[...], run_b[...]))

    def live(c):
        j, lowest = c
        return (j < n_pairs) & (lowest < SB_DEAD_BITS)

    j_end, _ = lax.while_loop(live, body, (jnp.int32(0), jnp.float32(0.0)))
    n_done = ratio + 2 * j_end
    step(n_done + 1, (ratio + 1) % 2, q_=False, s1=False)
    step(n_done + 2, ratio % 2, q_=False, s1=False, s2=False)
    o_ref[0] = acc_o[...].astype(o_ref.dtype)


def _sb_attention(qkv, u2):
    b, l, _ = qkv.shape
    tq, tk = min(SB_TQ, l), SB_TK
    assert (tq // tk) % 2 == 0 and tk == LANES
    npair = SB_HEADS * SB_HD // LANES
    wide = pltpu.VMEM((tq, 2 * tk), F32)
    col = pltpu.VMEM((tq, LANES), F32)
    return pl.pallas_call(
        functools.partial(_sb_kernel, tq=tq, tk=tk),
        out_shape=jax.ShapeDtypeStruct((b, l, SB_HEADS * SB_HD), BF16),
        grid=(b, npair, l // tq),
        in_specs=[
            pl.BlockSpec((1, tq, LANES), lambda bi, j, qi: (bi, qi, j)),
            pl.BlockSpec((1, l, LANES), lambda bi, j, qi: (bi, 0, npair + j)),
            pl.BlockSpec((1, l, LANES), lambda bi, j, qi: (bi, 0, 2 * npair + j)),
            pl.BlockSpec((2 * tk, 2 * tk), lambda bi, j, qi: (0, 0)),
        ],
        out_specs=pl.BlockSpec((1, tq, LANES), lambda bi, j, qi: (bi, qi, j)),
        scratch_shapes=[wide, wide, wide, wide, wide, wide,
                        pltpu.VMEM((tq, 2 * tk), BF16), pltpu.VMEM((tq, 2 * tk), BF16),
                        col, col, pltpu.VMEM((tq, LANES), F32)],
        compiler_params=_cparams("parallel", "parallel", "arbitrary"),
        name="sb_attention",
    )(qkv, qkv, qkv, u2)


def _sb_suffix_matrix(tk):
    s = np.arange(2 * tk)
    same = (s[:, None] // tk) == (s[None, :] // tk)
    return jnp.asarray((same & (s[:, None] > s[None, :])).astype(np.float32), BF16)


def _mla_proj_kernel(x_ref, g_ref, wq_ref, wkv_ref, wkr_ref, qn_ref, wuq_ref, kvn_ref, wuk_ref, wuvt_ref,
                     pos_ref, invf_ref, q_out, kn_out, kr_out, vt_out):
    h = _rms(x_ref[...], g_ref[...]).astype(BF16)
    cq = _dot(h, wq_ref[...])
    ckv = _dot(h, wkv_ref[...])
    kr = _dot(h, wkr_ref[...])
    q = _dot(_rms(cq, qn_ref[...]).astype(BF16), wuq_ref[...])
    ckv_n = _rms(ckv, kvn_ref[...]).astype(BF16)
    kn_out[...] = _dot(ckv_n, wuk_ref[...]).astype(kn_out.dtype)
    vt_out[0] = _dot_nt(wuvt_ref[...], ckv_n).astype(vt_out.dtype)

    ang = pos_ref[...].astype(F32) * invf_ref[...]
    cos, sin = jnp.cos(ang), jnp.sin(ang)
    lane = lax.broadcasted_iota(jnp.int32, ang.shape, 1)
    half = MLA_ROPE // 2
    sin_lo = jnp.where(lane < half, -sin, 0.0)
    sin_hi = jnp.where((lane >= half) & (lane < MLA_ROPE), sin, 0.0)

    def rope(t):
        return (t * cos + pltpu.roll(t, LANES - half, 1) * sin_lo + pltpu.roll(t, half, 1) * sin_hi)

    kr_out[...] = rope(kr).astype(kr_out.dtype)
    scale = (MLA_NOPE + MLA_ROPE) ** -0.5 * LOG2E
    for hh in range(MLA_HEADS):
        base = hh * 2 * LANES
        q_out[:, base:base + LANES] = (q[:, base:base + LANES] * scale).astype(q_out.dtype)
        q_out[:, base + LANES:base + 2 * LANES] = (
            rope(q[:, base + LANES:base + 2 * LANES]) * scale).astype(q_out.dtype)


def _mla_proj(x, g, wq, wkv, wkr, qn, wuq, kvn, wuk, wuvt, pos, invf):
    t, d = x.shape
    tm = min(TOK_TM, t)
    full = lambda a: pl.BlockSpec(a.shape, lambda i: (0, 0))
    nq = MLA_HEADS * 2 * LANES
    nv = MLA_HEADS * MLA_V
    return pl.pallas_call(
        _mla_proj_kernel,
        out_shape=(jax.ShapeDtypeStruct((t, nq), BF16), jax.ShapeDtypeStruct((t, MLA_HEADS * MLA_NOPE), BF16),
                   jax.ShapeDtypeStruct((t, LANES), BF16), jax.ShapeDtypeStruct((t // tm, nv, tm), BF16)),
        grid=(t // tm,),
        in_specs=[pl.BlockSpec((tm, d), lambda i: (i, 0)), full(g), full(wq), full(wkv), full(wkr),
                  full(qn), full(wuq), full(kvn), full(wuk), full(wuvt),
                  pl.BlockSpec((tm, 1), lambda i: (i, 0)), full(invf)],
        out_specs=(pl.BlockSpec((tm, nq), lambda i: (i, 0)),
                   pl.BlockSpec((tm, MLA_HEADS * MLA_NOPE), lambda i: (i, 0)),
                   pl.BlockSpec((tm, LANES), lambda i: (i, 0)),
                   pl.BlockSpec((1, nv, tm), lambda i: (i, 0, 0))),
        compiler_params=_cparams("parallel"),
        name="mla_proj",
    )(x, g, wq, wkv, wkr, qn, wuq, kvn, wuk, wuvt, pos, invf)


def _mla_attn_kernel(q_ref, kn_ref, kr_ref, vt_ref, o_ref, s0, s1, p0, p1, c0, c1, x0, x1, m_s, l_s, acc_s,
                     *, tq, tk):
    qi = pl.program_id(2)
    q = q_ref[0]
    sb, pb, cb, xb = (s0, s1), (p0, p1), (c0, c1), (x0, x1)
    m_s[...] = jnp.full_like(m_s, -jnp.inf)
    l_s[...] = jnp.zeros_like(l_s)
    acc_s[...] = jnp.zeros_like(acc_s)
    p1[...] = jnp.zeros_like(p1)
    c1[...] = jnp.ones_like(c1)
    query_minus_key = (lax.broadcasted_iota(jnp.int32, (tk, tq), 1)
                       - lax.broadcasted_iota(jnp.int32, (tk, tq), 0))

    def tile_index(t):
        return jnp.maximum(t, 0)

    def stage_q(t, par):
        start = pl.multiple_of(tile_index(t) * tk, tk)
        kc = jnp.concatenate([kn_ref[0, pl.ds(start, tk), :], kr_ref[0, pl.ds(start, tk), :]], axis=1)
        s = _dot_nt(kc, q)
        sb[par][...] = s
        xb[par][...] = jnp.max(s, axis=0, keepdims=True)

    def stage_s(par, thr):
        s = sb[par][...]
        if thr is None:
            tile_max = xb[par][...]
        else:
            s = jnp.where(query_minus_key >= thr, s, -jnp.inf)
            tile_max = jnp.max(s, axis=0, keepdims=True)
        m_prev = m_s[...]
        m_new = jnp.maximum(m_prev, tile_max)
        p = jnp.exp2(s - m_new)
        corr = jnp.exp2(m_prev - m_new)
        pb[par][...] = p.astype(BF16)
        cb[par][...] = corr
        l_s[...] = l_s[...] * corr + jnp.sum(p, axis=0, keepdims=True)
        m_s[...] = m_new

    def stage_p(t, par):
        acc_s[...] = acc_s[...] * cb[par][...] + _dot(vt_ref[0, tile_index(t)], pb[par][...])

    stage_q(0, 0)

    def body(j, c):
        t = 2 * j
        stage_q(t + 1, 1)
        stage_s(0, None)
        stage_p(t - 1, 1)
        stage_q(t + 2, 0)
        stage_s(1, None)
        stage_p(t, 0)
        return c

    lax.fori_loop(0, qi, body, 0)
    d0 = 2 * qi
    stage_q(d0 + 1, 1)
    stage_s(0, 0)
    stage_p(d0 - 1, 1)
    stage_s(1, tk)
    stage_p(d0, 0)
    stage_p(d0 + 1, 1)
    o_ref[0] = (acc_s[...] / l_s[...]).T.astype(o_ref.dtype)


def _mla_attention(qc, kn, kr, vt):
    b, l, _ = qc.shape
    tq, tk = min(MLA_TQ, l), vt.shape[-1]
    assert tq == 2 * tk and tk % LANES == 0 and MLA_V == LANES and MLA_NOPE == LANES
    row = pltpu.VMEM((1, tq), F32)
    return pl.pallas_call(
        functools.partial(_mla_attn_kernel, tq=tq, tk=tk),
        out_shape=jax.ShapeDtypeStruct((b, l, MLA_HEADS * MLA_V), BF16),
        grid=(b, MLA_HEADS, l // tq),
        in_specs=[
            pl.BlockSpec((1, tq, 2 * LANES), lambda bi, hh, qi: (bi, qi, hh)),
            pl.BlockSpec((1, l, LANES), lambda bi, hh, qi: (bi, 0, hh)),
            pl.BlockSpec((1, l, LANES), lambda bi, hh, qi: (bi, 0, 0)),
            pl.BlockSpec((1, l // tk, MLA_V, tk), lambda bi, hh, qi: (bi, 0, hh, 0)),
        ],
        out_specs=pl.BlockSpec((1, tq, LANES), lambda bi, hh, qi: (bi, qi, hh)),
        scratch_shapes=[pltpu.VMEM((tk, tq), F32), pltpu.VMEM((tk, tq), F32),
                        pltpu.VMEM((tk, tq), BF16), pltpu.VMEM((tk, tq), BF16),
                        row, row, row, row, row, row, pltpu.VMEM((MLA_V, tq), F32)],
        compiler_params=_cparams("parallel", "parallel", "arbitrary"),
        name="mla_attention",
    )(qc, kn, kr, vt)


GDN_TM = 256
GDN_GROUP = 4
GDN_STEP_CHUNKS = 2


def _gdn_in_kernel(x_ref, g_ref, wqkv_ref, wz_ref, wba_ref, cw_ref, alog_ref, dtb_ref, tri_ref,
                   q_out, k_out, v_out, z_out, bw_out, gw_out, bg_out, pre_s, *, tm, tiles_per_seq):
    i = pl.program_id(0)
    h = _rms(x_ref[...], g_ref[...]).astype(BF16)

    @pl.when(i % tiles_per_seq == 0)
    def _():
        pre_s[0:8, :] = jnp.zeros((8, GDN_CONV_DIM), F32)

    ba = _dot(h, wba_ref[...])
    lane = lax.broadcasted_iota(jnp.int32, ba.shape, 1)
    beta = _sigmoid(ba)
    gval = -jnp.exp(alog_ref[...]) * _softplus(ba + dtb_ref[...])
    gcum = _split_dot_left(tri_ref[...], gval, 3)
    bg = jnp.where(lane < GDN_HEADS, beta, gcum)
    bg_out[...] = bg
    for hh in range(GDN_HEADS):
        sl = slice(hh * GDN_DK, (hh + 1) * GDN_DK)
        bw_out[:, sl] = jnp.broadcast_to(bg[:, hh:hh + 1], (tm, GDN_DK))
        gw_out[:, sl] = jnp.broadcast_to(bg[:, GDN_HEADS + hh:GDN_HEADS + hh + 1], (tm, GDN_DK))

    cw = cw_ref[...]
    width = 2 * GDN_DK
    for blk in range(GDN_CONV_DIM // width):
        cs = slice(blk * width, (blk + 1) * width)
        pre = _dot(h, wqkv_ref[:, cs])
        pre_s[8:8 + tm, cs] = pre
        conv = pre * cw[3:4, cs]
        for j in range(1, GDN_CONV):
            conv = conv + pre_s[8 - j:8 - j + tm, cs] * cw[3 - j:4 - j, cs]
        pre_s[0:8, cs] = pre_s[tm:tm + 8, cs]
        y = conv * _sigmoid(conv)
        if blk * width >= 2 * GDN_QK:
            v_out[:, blk * width - 2 * GDN_QK:(blk + 1) * width - 2 * GDN_QK] = y
            continue
        is_q = blk * width < GDN_QK
        out = q_out if is_q else k_out
        for half in range(2):
            yh = y[:, half * GDN_DK:(half + 1) * GDN_DK]
            yn = yh * lax.rsqrt(jnp.sum(yh * yh, axis=1, keepdims=True) + EPS)
            col = (blk * width) % GDN_QK + half * GDN_DK
            out[:, col:col + GDN_DK] = yn * GDN_DK ** -0.5 if is_q else yn

    for blk in range(GDN_V // width):
        cs = slice(blk * width, (blk + 1) * width)
        z = _dot(h, wz_ref[:, cs])
        z_out[:, cs] = z * _sigmoid(z)


def _split_dot_left(m_bf16, x, terms):
    out = None
    r = x
    for i in range(terms):
        p = r.astype(BF16)
        d = _dot(m_bf16, p)
        out = d if out is None else out + d
        if i + 1 < terms:
            r = r - p.astype(F32)
    return out


def _gdn_in(x, g, wqkv, wz, wba, cw, alog, dtb, tri, seq):
    t, d = x.shape
    tm = min(GDN_TM, seq)
    full = lambda a: pl.BlockSpec(a.shape, lambda i: (0, 0))
    row = lambda n: pl.BlockSpec((tm, n), lambda i: (i, 0))
    wide = jax.ShapeDtypeStruct((t, GDN_QK), F32)
    return pl.pallas_call(
        functools.partial(_gdn_in_kernel, tm=tm, tiles_per_seq=seq // tm),
        out_shape=(wide, wide, wide, wide, wide, wide, jax.ShapeDtypeStruct((t, LANES), F32)),
        grid=(t // tm,),
        in_specs=[row(d), full(g), full(wqkv), full(wz), full(wba), full(cw), full(alog), full(dtb), full(tri)],
        out_specs=(row(GDN_QK),) * 6 + (row(LANES),),
        scratch_shapes=[pltpu.VMEM((tm + 8, GDN_CONV_DIM), F32)],
        compiler_params=_cparams("arbitrary"),
        name="gdn_in",
    )(x, g, wqkv, wz, wba, cw, alog, dtb, tri)


def _gdn_chunk_kernel(q_ref, k_ref, v_ref, bw_ref, gw_ref, gr_ref, o_ref, s_ref, *, nchunk):
    c, grp = GDN_CHUNK, GDN_GROUP
    rows = grp * c

    @pl.when(pl.program_id(1) == 0)
    def _():
        s_ref[...] = jnp.zeros_like(s_ref)

    r = lax.broadcasted_iota(jnp.int32, (rows, rows), 0)
    cc = lax.broadcasted_iota(jnp.int32, (rows, rows), 1)
    same = lax.shift_right_logical(r, 6) == lax.shift_right_logical(cc, 6)
    incl = same & (r >= cc)
    strict = same & (r > cc)
    own_block = (lax.shift_right_logical(lax.broadcasted_iota(jnp.int32, (rows, grp * GDN_DK), 0), 6)
                 == lax.shift_right_logical(lax.broadcasted_iota(jnp.int32, (rows, grp * GDN_DK), 1), 7))

    def spread(t):
        t16 = t.astype(BF16)
        return jnp.where(own_block, jnp.concatenate([t16] * grp, axis=1), jnp.zeros((), BF16))

    ngrp = GDN_HEADS // grp
    units = [(ci, g) for ci in range(nchunk) for g in range(ngrp)]

    def stack(ref, ci, g):
        return jnp.concatenate(
            [ref[0, ci * c:(ci + 1) * c, (g * grp + j) * GDN_DK:(g * grp + j + 1) * GDN_DK] for j in range(grp)],
            axis=0)

    ks = [stack(k_ref, ci, g) for ci, g in units]
    gw = [stack(gw_ref, ci, g) for ci, g in units]
    bw = [stack(bw_ref, ci, g) for ci, g in units]
    k16 = [t.astype(BF16) for t in ks]
    kb = [ks[i] * bw[i] for i in range(len(units))]
    eg = [jnp.exp(t) for t in gw]
    decay = []
    for i, (ci, g) in enumerate(units):
        gdiff = jnp.concatenate([gw[i], gw[i]], axis=1) - gr_ref[0, ci, g:g + 1, :]
        decay.append(jnp.where(incl, jnp.exp(jnp.where(incl, gdiff, 0.0)), 0.0))
    p = [jnp.where(strict, _dot_nt(kb[i].astype(BF16), k16[i]) * decay[i], 0.0).astype(BF16)
         for i in range(len(units))]
    y = [jnp.concatenate([stack(v_ref, ci, g) * bw[i], kb[i] * eg[i]], axis=1)
         for i, (ci, g) in enumerate(units)]
    y = [y[i] - _dot(p[i], y[i].astype(BF16)) for i in range(len(units))]
    step = 2
    while step < c:
        p = [_dot(t, t).astype(BF16) for t in p]
        y = [y[i] + _dot(p[i], y[i].astype(BF16)) for i in range(len(units))]
        step *= 2
    qs = [stack(q_ref, ci, g) for ci, g in units]
    attn = [(_dot_nt(qs[i].astype(BF16), k16[i]) * decay[i]).astype(BF16) for i in range(len(units))]

    for ci in range(nchunk):
        idx = [ci * ngrp + g for g in range(ngrp)]
        s = [s_ref[g] for g in range(ngrp)]
        ws, qs_s = [], []
        for g, i in enumerate(idx):
            qd = qs[i] * eg[i]
            prod = [_dot(jnp.concatenate([y[i][j * c:(j + 1) * c, GDN_DV:], qd[j * c:(j + 1) * c]],
                                         axis=0).astype(BF16),
                         s[g][j * GDN_DK:(j + 1) * GDN_DK].astype(BF16)) for j in range(grp)]
            ws.append(jnp.concatenate([t[:c] for t in prod], axis=0))
            qs_s.append(jnp.concatenate([t[c:] for t in prod], axis=0))
        v16 = [(y[i][:, :GDN_DV] - ws[g]).astype(BF16) for g, i in enumerate(idx)]
        o = [qs_s[g] + _dot(attn[i], v16[g]) for g, i in enumerate(idx)]
        for g, i in enumerate(idx):
            last = [gw[i][j * c + c - 1:j * c + c, :] for j in range(grp)]
            g_last = jnp.concatenate([jnp.broadcast_to(t, (c, GDN_DK)) for t in last], axis=0)
            s_decay = jnp.concatenate([jnp.broadcast_to(jnp.exp(t), (GDN_DK, GDN_DV)) for t in last], axis=0)
            kd = ks[i] * jnp.exp(g_last - gw[i])
            s_ref[g] = s[g] * s_decay + _dot_tn(spread(kd), v16[g])
            for j in range(grp):
                o_ref[0, ci * c:(ci + 1) * c, (g * grp + j) * GDN_DV:(g * grp + j + 1) * GDN_DV] = (
                    o[g][j * c:(j + 1) * c])


def _gdn_chunks(q, k, v, bw, gw, grow):
    b, l, n = q.shape
    nchunk = GDN_STEP_CHUNKS
    tl = nchunk * GDN_CHUNK
    ngrp = GDN_HEADS // GDN_GROUP
    blk = pl.BlockSpec((1, tl, n), lambda bi, ci: (bi, ci, 0))
    return pl.pallas_call(
        functools.partial(_gdn_chunk_kernel, nchunk=nchunk),
        out_shape=jax.ShapeDtypeStruct((b, l, n), F32),
        grid=(b, l // tl),
        in_specs=[blk, blk, blk, blk, blk,
                  pl.BlockSpec((1, nchunk, ngrp, GDN_GROUP * GDN_CHUNK), lambda bi, ci: (bi, ci, 0, 0))],
        out_specs=blk,
        scratch_shapes=[pltpu.VMEM((ngrp, GDN_GROUP * GDN_DK, GDN_DV), F32)],
        compiler_params=_cparams("parallel", "arbitrary"),
        name="gdn_chunks",
    )(q, k, v, bw, gw, grow)


def _gdn_out_kernel(o_ref, z_ref, gn_ref, w_ref, x_ref, out_ref, a_s):
    gn = gn_ref[...]
    for hh in range(GDN_HEADS):
        sl = slice(hh * GDN_DV, (hh + 1) * GDN_DV)
        a_s[:, sl] = (_rms(o_ref[:, sl], gn) * z_ref[:, sl]).astype(BF16)
    out_ref[...] = x_ref[...] + _dot(a_s[...], w_ref[...])


def _gdn_out(o, zs, gn, w, x):
    t, d = x.shape
    tm = min(TOK_TM, t)
    row = lambda n: pl.BlockSpec((tm, n), lambda i: (i, 0))
    full = lambda a: pl.BlockSpec(a.shape, lambda i: (0, 0))
    return pl.pallas_call(
        _gdn_out_kernel,
        out_shape=jax.ShapeDtypeStruct((t, d), F32),
        grid=(t // tm,),
        in_specs=[row(GDN_V), row(GDN_V), full(gn), full(w), row(d)],
        out_specs=row(d),
        scratch_shapes=[pltpu.VMEM((tm, GDN_V), BF16)],
        compiler_params=_cparams("parallel"),
        name="gdn_out",
    )(o, zs, gn, w, x)


def _chunk_tri(tm):
    t = np.arange(tm)
    same = (t[:, None] // GDN_CHUNK) == (t[None, :] // GDN_CHUNK)
    return jnp.asarray((same & (t[:, None] >= t[None, :])).astype(np.float32), BF16)


def _row(v):
    return v.reshape(1, -1).astype(F32)


def _pad_lanes(v, offset):
    out = jnp.zeros((1, LANES), F32)
    return out.at[0, offset:offset + v.shape[0]].set(v.astype(F32))


def _gated_deltanet(x, norm_g, w_in, conv_w, a_log, dt_bias, out_norm, w_out, batch, seq):
    t = x.shape[0]
    wqkv = w_in[:, :GDN_CONV_DIM].astype(BF16)
    wz = w_in[:, GDN_CONV_DIM:GDN_CONV_DIM + GDN_V].astype(BF16)
    wba = jnp.pad(w_in[:, GDN_CONV_DIM + GDN_V:], ((0, 0), (0, LANES - 2 * GDN_HEADS))).astype(BF16)
    q, k, v, zs, bw, gw, bg = _gdn_in(x, _row(norm_g), wqkv, wz, wba, conv_w.astype(F32),
                                      _pad_lanes(a_log, GDN_HEADS), _pad_lanes(dt_bias, GDN_HEADS),
                                      _chunk_tri(min(GDN_TM, seq)), seq)
    nc = seq // GDN_CHUNK
    grow = bg[:, GDN_HEADS:2 * GDN_HEADS].reshape(batch, nc, GDN_CHUNK, GDN_HEADS).transpose(0, 1, 3, 2)
    grow = grow.reshape(batch, nc, GDN_HEADS // GDN_GROUP, GDN_GROUP * GDN_CHUNK)
    shp = (batch, seq, GDN_QK)
    o = _gdn_chunks(q.reshape(shp), k.reshape(shp), v.reshape(shp), bw.reshape(shp), gw.reshape(shp), grow)
    return _gdn_out(o.reshape(t, GDN_V), zs, _row(out_norm), w_out.astype(BF16), x)


def _stick_breaking(x, norm_g, w_qkv, w_out, batch, seq):
    t = x.shape[0]
    n = SB_HEADS * SB_HD
    w = jnp.concatenate([w_qkv[:, :n] * (SB_HD ** -0.5 * LOG2E), w_qkv[:, n:]], axis=1).astype(BF16)
    qkv = _norm_matmul(x, _row(norm_g), w, 1024, BF16)
    o = _sb_attention(qkv.reshape(batch, seq, 3 * n), _sb_suffix_matrix(SB_TK))
    return _matmul_res(o.reshape(t, n), w_out.astype(BF16), x)


def _mla(x, norm_g, positions, w_in, q_norm, w_uq, kv_norm, w_ukv, w_out, batch, seq):
    t = x.shape[0]
    wq = w_in[:, :MLA_Q_RANK].astype(BF16)
    wkv = w_in[:, MLA_Q_RANK:MLA_Q_RANK + MLA_KV_RANK].astype(BF16)
    wkr = jnp.pad(w_in[:, MLA_Q_RANK + MLA_KV_RANK:], ((0, 0), (0, LANES - MLA_ROPE))).astype(BF16)
    qd = MLA_NOPE + MLA_ROPE
    wuq = jnp.pad(w_uq.reshape(MLA_Q_RANK, MLA_HEADS, qd), ((0, 0), (0, 0), (0, 2 * LANES - qd)))
    wuq = wuq.reshape(MLA_Q_RANK, MLA_HEADS * 2 * LANES).astype(BF16)
    half = MLA_ROPE // 2
    inv_freq = ROPE_THETA ** (-jnp.arange(half, dtype=F32) / half)
    invf = jnp.zeros((1, LANES), F32).at[0, :MLA_ROPE].set(jnp.concatenate([inv_freq, inv_freq]))
    wukv = w_ukv.reshape(MLA_KV_RANK, MLA_HEADS, MLA_NOPE + MLA_V)
    wuk = wukv[:, :, :MLA_NOPE].reshape(MLA_KV_RANK, MLA_HEADS * MLA_NOPE).astype(BF16)
    wuvt = wukv[:, :, MLA_NOPE:].reshape(MLA_KV_RANK, MLA_HEADS * MLA_V).T.astype(BF16)
    qc, kn, kr, vt = _mla_proj(x, _row(norm_g), wq, wkv, wkr, _row(q_norm), wuq, _row(kv_norm),
                               wuk, wuvt, positions.reshape(t, 1), invf)
    tk = vt.shape[-1]
    o = _mla_attention(qc.reshape(batch, seq, MLA_HEADS * 2 * LANES), kn.reshape(batch, seq, MLA_HEADS * MLA_NOPE),
                       kr.reshape(batch, seq, LANES), vt.reshape(batch, seq // tk, MLA_HEADS * MLA_V, tk))
    return _matmul_res(o.reshape(t, MLA_HEADS * MLA_V), w_out.astype(BF16), x)


def kernel(x, p, positions, ffn1_norm, ffn1_w_gate, ffn1_w_up, ffn1_w_down, mix_norm, ffn2_norm, ffn2_w_gate, ffn2_w_up, ffn2_w_down, ple_norm, ple_w_gate, ple_w_proj, gdn_w_in, gdn_conv_w, gdn_a_log, gdn_dt_bias, gdn_out_norm, gdn_w_out, sb_w_qkv, sb_w_out, mla_w_in, mla_q_norm, mla_w_uq, mla_kv_norm, mla_w_ukv, mla_w_out, final_norm):
    batch, seq, d = x.shape
    t = batch * seq
    depth = ffn1_norm.shape[0]
    x = x.reshape(t, d)
    p = p.reshape(depth, t, PLE_DIM)
    for i in range(depth):
        mixer, slot = i % N_MIXERS, i // N_MIXERS
        x = _ffn(x, _row(ffn1_norm[i]), ffn1_w_gate[i].astype(BF16), ffn1_w_up[i].astype(BF16),
                 ffn1_w_down[i].astype(BF16))
        if mixer == 0:
            x = _gated_deltanet(x, mix_norm[i], gdn_w_in[slot], gdn_conv_w[slot], gdn_a_log[slot],
                                gdn_dt_bias[slot], gdn_out_norm[slot], gdn_w_out[slot], batch, seq)
        elif mixer == 1:
            x = _stick_breaking(x, mix_norm[i], sb_w_qkv[slot], sb_w_out[slot], batch, seq)
        else:
            x = _mla(x, mix_norm[i], positions, mla_w_in[slot], mla_q_norm[slot], mla_w_uq[slot],
                     mla_kv_norm[slot], mla_w_ukv[slot], mla_w_out[slot], batch, seq)
        x = _ffn(x, _row(ffn2_norm[i]), ffn2_w_gate[i].astype(BF16), ffn2_w_up[i].astype(BF16),
                 ffn2_w_down[i].astype(BF16))
        x = _ple(x, _row(ple_norm[i]), ple_w_gate[i].astype(BF16), p[i], ple_w_proj[i].astype(BF16),
                 _row(final_norm), final=(i == depth - 1))
    return x.reshape(batch, seq, d)
```

```python
import functools

import numpy as np
import jax
import jax.numpy as jnp
from jax import lax
from jax.experimental import pallas as pl
from jax.experimental.pallas import tpu as pltpu

F32 = jnp.float32
BF16 = jnp.bfloat16

D_MODEL = 1024
DEPTH = 4
N_MIXERS = 3
PLE_DIM = 256
D_FF = 2816
EPS = 1e-6

GDN_HEADS = 8
GDN_DK = 128
GDN_DV = 128
GDN_CONV = 4
GDN_CHUNK = 64
GDN_QK = GDN_HEADS * GDN_DK
GDN_V = GDN_HEADS * GDN_DV
GDN_CONV_DIM = 2 * GDN_QK + GDN_V

SB_HEADS = 16
SB_HD = 64

MLA_HEADS = 8
MLA_Q_RANK = 384
MLA_KV_RANK = 256
MLA_NOPE = 128
MLA_ROPE = 64
MLA_V = 128
ROPE_THETA = 10000.0

LANES = 128
VMEM_LIMIT = 56 * 1024 * 1024

FFN_TM = 1024
FFN_CHUNK = 512
TOK_TM = 512
SB_TQ = 512
SB_TK = 128
MLA_TQ = 2 * TOK_TM
MLA_PAIR = 2
MLA_LROWS = 16

LOG2E = 1.4426950408889634
NEG_BIG = -1e30
SB_SPLIT_TERMS = 1
SB_DEAD_BITS = 150.0


def _cparams(*sem):
    return pltpu.CompilerParams(dimension_semantics=sem, vmem_limit_bytes=VMEM_LIMIT)


def _dot(a, b):
    return jnp.dot(a, b, preferred_element_type=F32)


def _dot_nt(a, b):
    return lax.dot_general(a, b, (((1,), (1,)), ((), ())), preferred_element_type=F32)


def _dot_tn(a, b):
    return lax.dot_general(a, b, (((0,), (0,)), ((), ())), preferred_element_type=F32)


def _rms(xf, g):
    return xf * lax.rsqrt(jnp.mean(xf * xf, axis=-1, keepdims=True) + EPS) * g


def _sigmoid(x):
    return 1.0 / (1.0 + jnp.exp(-x))


def _softplus(x):
    return jnp.maximum(x, 0.0) + jnp.log(1.0 + jnp.exp(-jnp.abs(x)))


def _split_dot(x, m_bf16, terms):
    out = None
    r = x
    for i in range(terms):
        p = r.astype(BF16)
        d = _dot(p, m_bf16)
        out = d if out is None else out + d
        if i + 1 < terms:
            r = r - p.astype(F32)
    return out


def _ffn_kernel(x_ref, g_ref, wg_ref, wu_ref, wd_ref, o_ref):
    x = x_ref[...]
    h = _rms(x, g_ref[...]).astype(BF16)
    acc = None
    for lo in range(0, D_FF, FFN_CHUNK):
        hi = min(lo + FFN_CHUNK, D_FF)
        gate = _dot(h, wg_ref[:, lo:hi])
        up = _dot(h, wu_ref[:, lo:hi])
        act = (gate * _sigmoid(gate) * up).astype(BF16)
        part = _dot(act, wd_ref[lo:hi, :])
        acc = part if acc is None else acc + part
    o_ref[...] = x + 0.5 * acc


def _ffn(x, g, wg, wu, wd):
    t, d = x.shape
    tm = min(FFN_TM, t)
    once = lambda a: pl.BlockSpec(a.shape, lambda i: (0, 0), pipeline_mode=pl.Buffered(1))
    return pl.pallas_call(
        _ffn_kernel,
        out_shape=jax.ShapeDtypeStruct((t, d), F32),
        grid=(t // tm,),
        in_specs=[pl.BlockSpec((tm, d), lambda i: (i, 0)), once(g), once(wg), once(wu), once(wd)],
        out_specs=pl.BlockSpec((tm, d), lambda i: (i, 0)),
        compiler_params=_cparams("parallel"),
        name="ffn",
    )(x, g, wg, wu, wd)


def _ple_kernel(x_ref, g_ref, wgate_ref, p_ref, wp_ref, fin_ref, o_ref, *, final):
    x = x_ref[...]
    h = _rms(x, g_ref[...]).astype(BF16)
    gate = _sigmoid(_dot(h, wgate_ref[...]))
    y = x + gate * _dot(p_ref[...].astype(BF16), wp_ref[...])
    if final:
        y = _rms(y, fin_ref[...])
    o_ref[...] = y


def _ple(x, g, wgate, p, wp, fin, final):
    t, d = x.shape
    tm = min(FFN_TM, t)
    return pl.pallas_call(
        functools.partial(_ple_kernel, final=final),
        out_shape=jax.ShapeDtypeStruct((t, d), F32),
        grid=(t // tm,),
        in_specs=[
            pl.BlockSpec((tm, d), lambda i: (i, 0)),
            pl.BlockSpec((1, d), lambda i: (0, 0)),
            pl.BlockSpec((d, d), lambda i: (0, 0)),
            pl.BlockSpec((tm, PLE_DIM), lambda i: (i, 0)),
            pl.BlockSpec((PLE_DIM, d), lambda i: (0, 0)),
            pl.BlockSpec((1, d), lambda i: (0, 0)),
        ],
        out_specs=pl.BlockSpec((tm, d), lambda i: (i, 0)),
        compiler_params=_cparams("parallel"),
        name="ple",
    )(x, g, wgate, p, wp, fin)


def _norm_matmul_kernel(x_ref, g_ref, w_ref, o_ref, *, tn):
    h = _rms(x_ref[...], g_ref[...]).astype(BF16)
    for lo in range(0, w_ref.shape[1], tn):
        o_ref[:, lo:lo + tn] = _dot(h, w_ref[:, lo:lo + tn]).astype(o_ref.dtype)


def _norm_matmul(x, g, w, tn, out_dtype):
    t, d = x.shape
    n = w.shape[1]
    tm = min(FFN_TM, t)
    once = lambda a: pl.BlockSpec(a.shape, lambda i: (0, 0), pipeline_mode=pl.Buffered(1))
    return pl.pallas_call(
        functools.partial(_norm_matmul_kernel, tn=tn),
        out_shape=jax.ShapeDtypeStruct((t, n), out_dtype),
        grid=(t // tm,),
        in_specs=[pl.BlockSpec((tm, d), lambda i: (i, 0)), once(g), once(w)],
        out_specs=pl.BlockSpec((tm, n), lambda i: (i, 0)),
        compiler_params=_cparams("parallel"),
        name="norm_matmul",
    )(x, g, w)


def _matmul_res_kernel(a_ref, w_ref, x_ref, o_ref):
    o_ref[...] = x_ref[...] + _dot(a_ref[...].astype(BF16), w_ref[...])


def _matmul_res(a, w, x):
    t, k = a.shape
    d = w.shape[1]
    tm = min(FFN_TM, t)
    return pl.pallas_call(
        _matmul_res_kernel,
        out_shape=jax.ShapeDtypeStruct((t, d), F32),
        grid=(t // tm,),
        in_specs=[
            pl.BlockSpec((tm, k), lambda i: (i, 0)),
            pl.BlockSpec((k, d), lambda i: (0, 0)),
            pl.BlockSpec((tm, d), lambda i: (i, 0)),
        ],
        out_specs=pl.BlockSpec((tm, d), lambda i: (i, 0)),
        compiler_params=_cparams("parallel"),
        name="matmul_res",
    )(a, w, x)


def _sb_kernel(q_ref, k_ref, v_ref, u_ref, o_ref,
               z0, z1, lb0, lb1, sf0, sf1, a0, a1, run_a, run_b, acc_o, *, tq, tk):
    qi = pl.program_id(2)
    ratio = tq // tk
    n_tiles = ratio * (qi + 1)
    q = q_ref[0]
    u2 = u_ref[...]
    zb, lbb, sfb, ab = (z0, z1), (lb0, lb1), (sf0, sf1), (a0, a1)
    first = lax.broadcasted_iota(jnp.int32, (tk, LANES), 1) < SB_HD
    row_minus_col = (lax.broadcasted_iota(jnp.int32, (tq, 2 * tk), 0)
                     - (lax.broadcasted_iota(jnp.int32, (tq, 2 * tk), 1) & (tk - 1)))
    acc_o[...] = jnp.zeros_like(acc_o)
    run_a[...] = jnp.zeros_like(run_a)
    run_b[...] = jnp.zeros_like(run_b)

    def blockdiag(t):
        zero = jnp.zeros_like(t)
        return jnp.concatenate([jnp.where(first, t, zero), jnp.where(first, zero, t)], axis=0)

    def key_start(t):
        kb = jnp.maximum(ratio * qi + ratio - 1 - t, 0)
        return pl.multiple_of(kb * tk, tk)

    def stage_q(t, par):
        kk = blockdiag(k_ref[0, pl.ds(key_start(t), tk), :])
        zb[par][...] = _dot_nt(q, kk)

    def stage_s1(par, thr):
        z = zb[par][...]
        neg_abs = lax.bitcast_convert_type(
            lax.bitcast_convert_type(z, jnp.uint32) | jnp.uint32(0x80000000), F32)
        sp = jnp.maximum(z, 0.0) + jnp.log(1.0 + jnp.exp2(neg_abs)) * LOG2E
        ra, rb = run_a[...], run_b[...]
        lb = z - sp - jnp.concatenate([ra, rb], axis=1)
        if thr is not None:
            valid = row_minus_col > thr
            sp = jnp.where(valid, sp, 0.0)
            lb = jnp.where(valid, lb, NEG_BIG)
        lbb[par][...] = lb
        sfb[par][...] = _split_dot(sp, u2, SB_SPLIT_TERMS)
        run_a[...] = ra + jnp.sum(sp[:, :tk], axis=1, keepdims=True)
        run_b[...] = rb + jnp.sum(sp[:, tk:], axis=1, keepdims=True)

    def stage_s2(par):
        ab[par][...] = jnp.exp2(lbb[par][...] - sfb[par][...]).astype(BF16)

    def stage_p(t, par):
        vv = blockdiag(v_ref[0, pl.ds(key_start(t), tk), :])
        acc_o[...] += _dot(ab[par][...], vv)

    def step(u, par, thr=None, q_=True, s1=True, s2=True, p_=True):
        if q_:
            stage_q(u, par)
        if s1:
            stage_s1(1 - par, thr)
        if s2:
            stage_s2(par)
        if p_:
            stage_p(u - 3, 1 - par)

    for u in range(ratio + 1):
        step(u, u % 2, thr=(ratio - u) * tk, s1=u >= 1, s2=u >= 2, p_=u >= 3)

    n_pairs = (n_tiles - ratio) // 2

    def body(c):
        j, _ = c
        u = ratio + 1 + 2 * j
        step(u, (ratio + 1) % 2)
        step(u + 1, ratio % 2)
        return j + 1, jnp.min(jnp.minimum(run_a[...], run_b[...]))

    def live(c):
        j, lowest = c
        return (j < n_pairs) & (lowest < SB_DEAD_BITS)

    j_end, _ = lax.while_loop(live, body, (jnp.int32(0), jnp.float32(0.0)))
    n_done = ratio + 2 * j_end
    step(n_done + 1, (ratio + 1) % 2, q_=False, s1=False)
    step(n_done + 2, ratio % 2, q_=False, s1=False, s2=False)
    o_ref[0] = acc_o[...].astype(o_ref.dtype)


def _sb_attention(qkv, u2):
    b, l, _ = qkv.shape
    tq, tk = min(SB_TQ, l), SB_TK
    assert (tq // tk) % 2 == 0 and tk == LANES
    npair = SB_HEADS * SB_HD // LANES
    wide = pltpu.VMEM((tq, 2 * tk), F32)
    col = pltpu.VMEM((tq, LANES), F32)
    return pl.pallas_call(
        functools.partial(_sb_kernel, tq=tq, tk=tk),
        out_shape=jax.ShapeDtypeStruct((b, l, SB_HEADS * SB_HD), BF16),
        grid=(b, npair, l // tq),
        in_specs=[
            pl.BlockSpec((1, tq, LANES), lambda bi, j, qi: (bi, qi, j)),
            pl.BlockSpec((1, l, LANES), lambda bi, j, qi: (bi, 0, npair + j)),
            pl.BlockSpec((1, l, LANES), lambda bi, j, qi: (bi, 0, 2 * npair + j)),
            pl.BlockSpec((2 * tk, 2 * tk), lambda bi, j, qi: (0, 0)),
        ],
        out_specs=pl.BlockSpec((1, tq, LANES), lambda bi, j, qi: (bi, qi, j)),
        scratch_shapes=[wide, wide, wide, wide, wide, wide,
                        pltpu.VMEM((tq, 2 * tk), BF16), pltpu.VMEM((tq, 2 * tk), BF16),
                        col, col, pltpu.VMEM((tq, LANES), F32)],
        compiler_params=_cparams("parallel", "parallel", "arbitrary"),
        name="sb_attention",
    )(qkv, qkv, qkv, u2)


def _sb_suffix_matrix(tk):
    s = np.arange(2 * tk)
    same = (s[:, None] // tk) == (s[None, :] // tk)
    return jnp.asarray((same & (s[:, None] > s[None, :])).astype(np.float32), BF16)


def _mla_proj_kernel(x_ref, g_ref, wq_ref, wkv_ref, wkr_ref, qn_ref, wuq_ref, kvn_ref, wuk_ref, wuvt_ref,
                     pos_ref, invf_ref, q_out, kn_out, kr_out, vt_out):
    h = _rms(x_ref[...], g_ref[...]).astype(BF16)
    cq = _dot(h, wq_ref[...])
    ckv = _dot(h, wkv_ref[...])
    kr = _dot(h, wkr_ref[...])
    q = _dot(_rms(cq, qn_ref[...]).astype(BF16), wuq_ref[...])
    ckv_n = _rms(ckv, kvn_ref[...]).astype(BF16)
    kn_out[...] = _dot(ckv_n, wuk_ref[...]).astype(kn_out.dtype)
    vt_out[0] = _dot_nt(wuvt_ref[...], ckv_n).astype(vt_out.dtype)

    ang = pos_ref[...].astype(F32) * invf_ref[...]
    cos, sin = jnp.cos(ang), jnp.sin(ang)
    lane = lax.broadcasted_iota(jnp.int32, ang.shape, 1)
    half = MLA_ROPE // 2
    sin_lo = jnp.where(lane < half, -sin, 0.0)
    sin_hi = jnp.where((lane >= half) & (lane < MLA_ROPE), sin, 0.0)

    def rope(t):
        return (t * cos + pltpu.roll(t, LANES - half, 1) * sin_lo + pltpu.roll(t, half, 1) * sin_hi)

    kr_out[...] = rope(kr).astype(kr_out.dtype)
    scale = (MLA_NOPE + MLA_ROPE) ** -0.5 * LOG2E
    for hh in range(MLA_HEADS):
        base = hh * 2 * LANES
        q_out[:, base:base + LANES] = (q[:, base:base + LANES] * scale).astype(q_out.dtype)
        q_out[:, base + LANES:base + 2 * LANES] = (
            rope(q[:, base + LANES:base + 2 * LANES]) * scale).astype(q_out.dtype)


def _mla_proj(x, g, wq, wkv, wkr, qn, wuq, kvn, wuk, wuvt, pos, invf):
    t, d = x.shape
    tm = min(TOK_TM, t)
    full = lambda a: pl.BlockSpec(a.shape, lambda i: (0, 0))
    nq = MLA_HEADS * 2 * LANES
    nv = MLA_HEADS * MLA_V
    return pl.pallas_call(
        _mla_proj_kernel,
        out_shape=(jax.ShapeDtypeStruct((t, nq), BF16), jax.ShapeDtypeStruct((t, MLA_HEADS * MLA_NOPE), BF16),
                   jax.ShapeDtypeStruct((t, LANES), BF16), jax.ShapeDtypeStruct((t // tm, nv, tm), BF16)),
        grid=(t // tm,),
        in_specs=[pl.BlockSpec((tm, d), lambda i: (i, 0)), full(g), full(wq), full(wkv), full(wkr),
                  full(qn), full(wuq), full(kvn), full(wuk), full(wuvt),
                  pl.BlockSpec((tm, 1), lambda i: (i, 0)), full(invf)],
        out_specs=(pl.BlockSpec((tm, nq), lambda i: (i, 0)),
                   pl.BlockSpec((tm, MLA_HEADS * MLA_NOPE), lambda i: (i, 0)),
                   pl.BlockSpec((tm, LANES), lambda i: (i, 0)),
                   pl.BlockSpec((1, nv, tm), lambda i: (i, 0, 0))),
        compiler_params=_cparams("parallel"),
        name="mla_proj",
    )(x, g, wq, wkv, wkr, qn, wuq, kvn, wuk, wuvt, pos, invf)


def _mla_attn_kernel(q_ref, kn_ref, kr_ref, vt_ref, o_ref, s0, s1, p0, p1, c0, c1, x0, x1, m_s, acc_s,
                     *, tq, tk):
    qi = pl.program_id(2)
    sb, pb, cb, xb = (s0, s1), (p0, p1), (c0, c1), (x0, x1)
    m_s[...] = jnp.full_like(m_s, -jnp.inf)
    acc_s[...] = jnp.zeros_like(acc_s)
    p1[...] = jnp.zeros_like(p1)
    c1[...] = jnp.ones_like(c1)
    query_minus_key = (lax.broadcasted_iota(jnp.int32, (tk, tq), 1)
                       - lax.broadcasted_iota(jnp.int32, (tk, tq), 0))

    def tile_index(t):
        return jnp.maximum(t, 0)

    def stage_q(hd, t, par):
        start = pl.multiple_of(tile_index(t) * tk, tk)
        kc = jnp.concatenate([kn_ref[0, pl.ds(start, tk), hd * LANES:(hd + 1) * LANES],
                              kr_ref[0, pl.ds(start, tk), :]], axis=1)
        q = q_ref[0, :, hd * 2 * LANES:(hd + 1) * 2 * LANES]
        s = _dot_nt(kc, q)
        sb[par][hd] = s
        xb[par][hd] = jnp.max(s, axis=0, keepdims=True)

    def stage_s(hd, par, thr):
        s = sb[par][hd]
        if thr is None:
            tile_max = xb[par][hd]
        else:
            s = jnp.where(query_minus_key >= thr, s, -jnp.inf)
            tile_max = jnp.max(s, axis=0, keepdims=True)
        m_prev = m_s[hd]
        m_new = jnp.maximum(m_prev, tile_max)
        p = jnp.exp2(s - m_new)
        corr = jnp.exp2(m_prev - m_new)
        pb[par][hd] = p.astype(BF16)
        cb[par][hd] = corr
        m_s[hd] = m_new

    ones_rows = jnp.ones((MLA_LROWS, tk), BF16)

    def stage_p(hd, t, par):
        v_t = jnp.concatenate([vt_ref[0, tile_index(t), hd * MLA_V:(hd + 1) * MLA_V, :], ones_rows], axis=0)
        acc_s[hd] = acc_s[hd] * cb[par][hd] + _dot(v_t, pb[par][hd])

    def step(t, par, thr=None, q_=True, s_=True, p_=True):
        for hd in range(MLA_PAIR):
            if q_:
                stage_q(hd, t + 1, 1 - par)
            if s_:
                stage_s(hd, par, thr)
            if p_:
                stage_p(hd, t - 1, 1 - par)

    for hd in range(MLA_PAIR):
        stage_q(hd, 0, 0)

    def body(j, c):
        step(2 * j, 0)
        step(2 * j + 1, 1)
        return c

    lax.fori_loop(0, qi, body, 0)
    d0 = 2 * qi
    step(d0, 0, thr=0)
    step(d0 + 1, 1, thr=tk, q_=False)
    step(d0 + 2, 0, q_=False, s_=False)
    for hd in range(MLA_PAIR):
        acc = acc_s[hd]
        o_ref[0, :, hd * MLA_V:(hd + 1) * MLA_V] = (acc[:MLA_V] / acc[MLA_V:MLA_V + 1]).T.astype(o_ref.dtype)


def _mla_attention(qc, kn, kr, vt):
    b, l, _ = qc.shape
    tq, tk = min(MLA_TQ, l), vt.shape[-1]
    assert tq == 2 * tk and tk % LANES == 0 and MLA_V == LANES and MLA_NOPE == LANES
    pair = MLA_PAIR
    row = pltpu.VMEM((pair, 1, tq), F32)
    once = dict(pipeline_mode=pl.Buffered(1))
    return pl.pallas_call(
        functools.partial(_mla_attn_kernel, tq=tq, tk=tk),
        out_shape=jax.ShapeDtypeStruct((b, l, MLA_HEADS * MLA_V), BF16),
        grid=(b, MLA_HEADS // pair, l // tq),
        in_specs=[
            pl.BlockSpec((1, tq, pair * 2 * LANES), lambda bi, hh, qi: (bi, qi, hh)),
            pl.BlockSpec((1, l, pair * LANES), lambda bi, hh, qi: (bi, 0, hh), **once),
            pl.BlockSpec((1, l, LANES), lambda bi, hh, qi: (bi, 0, 0), **once),
            pl.BlockSpec((1, l // tk, pair * MLA_V, tk), lambda bi, hh, qi: (bi, 0, hh, 0), **once),
        ],
        out_specs=pl.BlockSpec((1, tq, pair * LANES), lambda bi, hh, qi: (bi, qi, hh)),
        scratch_shapes=[pltpu.VMEM((pair, tk, tq), F32), pltpu.VMEM((pair, tk, tq), F32),
                        pltpu.VMEM((pair, tk, tq), BF16), pltpu.VMEM((pair, tk, tq), BF16),
                        row, row, row, row, row, pltpu.VMEM((pair, MLA_V + MLA_LROWS, tq), F32)],
        compiler_params=_cparams("parallel", "parallel", "arbitrary"),
        name="mla_attention",
    )(qc, kn, kr, vt)


GDN_TM = 256
GDN_GROUP = 4
GDN_STEP_CHUNKS = 2


def _gdn_in_kernel(x_ref, g_ref, wqkv_ref, wz_ref, wba_ref, cw_ref, alog_ref, dtb_ref, tri_ref,
                   q_out, k_out, v_out, z_out, bw_out, gw_out, bg_out, pre_s, *, tm, tiles_per_seq):
    i = pl.program_id(0)
    h = _rms(x_ref[...], g_ref[...]).astype(BF16)

    @pl.when(i % tiles_per_seq == 0)
    def _():
        pre_s[0:8, :] = jnp.zeros((8, GDN_CONV_DIM), F32)

    ba = _dot(h, wba_ref[...])
    lane = lax.broadcasted_iota(jnp.int32, ba.shape, 1)
    beta = _sigmoid(ba)
    gval = -jnp.exp(alog_ref[...]) * _softplus(ba + dtb_ref[...])
    gcum = _split_dot_left(tri_ref[...], gval, 3)
    bg = jnp.where(lane < GDN_HEADS, beta, gcum)
    bg_out[...] = bg
    for hh in range(GDN_HEADS):
        sl = slice(hh * GDN_DK, (hh + 1) * GDN_DK)
        bw_out[:, sl] = jnp.broadcast_to(bg[:, hh:hh + 1], (tm, GDN_DK))
        gw_out[:, sl] = jnp.broadcast_to(bg[:, GDN_HEADS + hh:GDN_HEADS + hh + 1], (tm, GDN_DK))

    cw = cw_ref[...]
    width = 2 * GDN_DK
    for blk in range(GDN_CONV_DIM // width):
        cs = slice(blk * width, (blk + 1) * width)
        pre = _dot(h, wqkv_ref[:, cs])
        pre_s[8:8 + tm, cs] = pre
        conv = pre * cw[3:4, cs]
        for j in range(1, GDN_CONV):
            conv = conv + pre_s[8 - j:8 - j + tm, cs] * cw[3 - j:4 - j, cs]
        pre_s[0:8, cs] = pre_s[tm:tm + 8, cs]
        y = conv * _sigmoid(conv)
        if blk * width >= 2 * GDN_QK:
            v_out[:, blk * width - 2 * GDN_QK:(blk + 1) * width - 2 * GDN_QK] = y
            continue
        is_q = blk * width < GDN_QK
        out = q_out if is_q else k_out
        for half in range(2):
            yh = y[:, half * GDN_DK:(half + 1) * GDN_DK]
            yn = yh * lax.rsqrt(jnp.sum(yh * yh, axis=1, keepdims=True) + EPS)
            col = (blk * width) % GDN_QK + half * GDN_DK
            out[:, col:col + GDN_DK] = yn * GDN_DK ** -0.5 if is_q else yn

    for blk in range(GDN_V // width):
        cs = slice(blk * width, (blk + 1) * width)
        z = _dot(h, wz_ref[:, cs])
        z_out[:, cs] = z * _sigmoid(z)


def _split_dot_left(m_bf16, x, terms):
    out = None
    r = x
    for i in range(terms):
        p = r.astype(BF16)
        d = _dot(m_bf16, p)
        out = d if out is None else out + d
        if i + 1 < terms:
            r = r - p.astype(F32)
    return out


def _gdn_in(x, g, wqkv, wz, wba, cw, alog, dtb, tri, seq):
    t, d = x.shape
    tm = min(GDN_TM, seq)
    full = lambda a: pl.BlockSpec(a.shape, lambda i: (0, 0))
    row = lambda n: pl.BlockSpec((tm, n), lambda i: (i, 0))
    wide = jax.ShapeDtypeStruct((t, GDN_QK), F32)
    return pl.pallas_call(
        functools.partial(_gdn_in_kernel, tm=tm, tiles_per_seq=seq // tm),
        out_shape=(wide, wide, wide, wide, wide, wide, jax.ShapeDtypeStruct((t, LANES), F32)),
        grid=(t // tm,),
        in_specs=[row(d), full(g), full(wqkv), full(wz), full(wba), full(cw), full(alog), full(dtb), full(tri)],
        out_specs=(row(GDN_QK),) * 6 + (row(LANES),),
        scratch_shapes=[pltpu.VMEM((tm + 8, GDN_CONV_DIM), F32)],
        compiler_params=_cparams("arbitrary"),
        name="gdn_in",
    )(x, g, wqkv, wz, wba, cw, alog, dtb, tri)


def _gdn_chunk_kernel(q_ref, k_ref, v_ref, bw_ref, gw_ref, gr_ref, o_ref, s_ref, *, nchunk):
    c, grp = GDN_CHUNK, GDN_GROUP
    rows = grp * c

    @pl.when(pl.program_id(1) == 0)
    def _():
        s_ref[...] = jnp.zeros_like(s_ref)

    r = lax.broadcasted_iota(jnp.int32, (rows, rows), 0)
    cc = lax.broadcasted_iota(jnp.int32, (rows, rows), 1)
    same = lax.shift_right_logical(r, 6) == lax.shift_right_logical(cc, 6)
    incl = same & (r >= cc)
    strict = same & (r > cc)
    own_block = (lax.shift_right_logical(lax.broadcasted_iota(jnp.int32, (rows, grp * GDN_DK), 0), 6)
                 == lax.shift_right_logical(lax.broadcasted_iota(jnp.int32, (rows, grp * GDN_DK), 1), 7))

    def spread(t):
        t16 = t.astype(BF16)
        return jnp.where(own_block, jnp.concatenate([t16] * grp, axis=1), jnp.zeros((), BF16))

    ngrp = GDN_HEADS // grp
    units = [(ci, g) for ci in range(nchunk) for g in range(ngrp)]

    def stack(ref, ci, g):
        return jnp.concatenate(
            [ref[0, ci * c:(ci + 1) * c, (g * grp + j) * GDN_DK:(g * grp + j + 1) * GDN_DK] for j in range(grp)],
            axis=0)

    ks = [stack(k_ref, ci, g) for ci, g in units]
    gw = [stack(gw_ref, ci, g) for ci, g in units]
    bw = [stack(bw_ref, ci, g) for ci, g in units]
    k16 = [t.astype(BF16) for t in ks]
    kb = [ks[i] * bw[i] for i in range(len(units))]
    eg = [jnp.exp(t) for t in gw]
    decay = []
    for i, (ci, g) in enumerate(units):
        gdiff = jnp.concatenate([gw[i], gw[i]], axis=1) - gr_ref[0, ci, g:g + 1, :]
        decay.append(jnp.where(incl, jnp.exp(jnp.where(incl, gdiff, 0.0)), 0.0))
    p = [jnp.where(strict, _dot_nt(kb[i].astype(BF16), k16[i]) * decay[i], 0.0).astype(BF16)
         for i in range(len(units))]
    y = [jnp.concatenate([stack(v_ref, ci, g) * bw[i], kb[i] * eg[i]], axis=1)
         for i, (ci, g) in enumerate(units)]
    y = [y[i] - _dot(p[i], y[i].astype(BF16)) for i in range(len(units))]
    step = 2
    while step < c:
        p = [_dot(t, t).astype(BF16) for t in p]
        y = [y[i] + _dot(p[i], y[i].astype(BF16)) for i in range(len(units))]
        step *= 2
    qs = [stack(q_ref, ci, g) for ci, g in units]
    attn = [(_dot_nt(qs[i].astype(BF16), k16[i]) * decay[i]).astype(BF16) for i in range(len(units))]

    for ci in range(nchunk):
        idx = [ci * ngrp + g for g in range(ngrp)]
        s = [s_ref[g] for g in range(ngrp)]
        ws, qs_s = [], []
        for g, i in enumerate(idx):
            qd = qs[i] * eg[i]
            prod = [_dot(jnp.concatenate([y[i][j * c:(j + 1) * c, GDN_DV:], qd[j * c:(j + 1) * c]],
                                         axis=0).astype(BF16),
                         s[g][j * GDN_DK:(j + 1) * GDN_DK].astype(BF16)) for j in range(grp)]
            ws.append(jnp.concatenate([t[:c] for t in prod], axis=0))
            qs_s.append(jnp.concatenate([t[c:] for t in prod], axis=0))
        v16 = [(y[i][:, :GDN_DV] - ws[g]).astype(BF16) for g, i in enumerate(idx)]
        o = [qs_s[g] + _dot(attn[i], v16[g]) for g, i in enumerate(idx)]
        for g, i in enumerate(idx):
            last = [gw[i][j * c + c - 1:j * c + c, :] for j in range(grp)]
            g_last = jnp.concatenate([jnp.broadcast_to(t, (c, GDN_DK)) for t in last], axis=0)
            s_decay = jnp.concatenate([jnp.broadcast_to(jnp.exp(t), (GDN_DK, GDN_DV)) for t in last], axis=0)
            kd = ks[i] * jnp.exp(g_last - gw[i])
            s_ref[g] = s[g] * s_decay + _dot_tn(spread(kd), v16[g])
            for j in range(grp):
                o_ref[0, ci * c:(ci + 1) * c, (g * grp + j) * GDN_DV:(g * grp + j + 1) * GDN_DV] = (
                    o[g][j * c:(j + 1) * c])


def _gdn_chunks(q, k, v, bw, gw, grow):
    b, l, n = q.shape
    nchunk = GDN_STEP_CHUNKS
    tl = nchunk * GDN_CHUNK
    ngrp = GDN_HEADS // GDN_GROUP
    blk = pl.BlockSpec((1, tl, n), lambda bi, ci: (bi, ci, 0))
    return pl.pallas_call(
        functools.partial(_gdn_chunk_kernel, nchunk=nchunk),
        out_shape=jax.ShapeDtypeStruct((b, l, n), F32),
        grid=(b, l // tl),
        in_specs=[blk, blk, blk, blk, blk,
                  pl.BlockSpec((1, nchunk, ngrp, GDN_GROUP * GDN_CHUNK), lambda bi, ci: (bi, ci, 0, 0))],
        out_specs=blk,
        scratch_shapes=[pltpu.VMEM((ngrp, GDN_GROUP * GDN_DK, GDN_DV), F32)],
        compiler_params=_cparams("parallel", "arbitrary"),
        name="gdn_chunks",
    )(q, k, v, bw, gw, grow)


def _gdn_out_kernel(o_ref, z_ref, gn_ref, w_ref, x_ref, out_ref, a_s):
    gn = gn_ref[...]
    for hh in range(GDN_HEADS):
        sl = slice(hh * GDN_DV, (hh + 1) * GDN_DV)
        a_s[:, sl] = (_rms(o_ref[:, sl], gn) * z_ref[:, sl]).astype(BF16)
    out_ref[...] = x_ref[...] + _dot(a_s[...], w_ref[...])


def _gdn_out(o, zs, gn, w, x):
    t, d = x.shape
    tm = min(FFN_TM, t)
    row = lambda n: pl.BlockSpec((tm, n), lambda i: (i, 0))
    full = lambda a: pl.BlockSpec(a.shape, lambda i: (0, 0))
    return pl.pallas_call(
        _gdn_out_kernel,
        out_shape=jax.ShapeDtypeStruct((t, d), F32),
        grid=(t // tm,),
        in_specs=[row(GDN_V), row(GDN_V), full(gn), full(w), row(d)],
        out_specs=row(d),
        scratch_shapes=[pltpu.VMEM((tm, GDN_V), BF16)],
        compiler_params=_cparams("parallel"),
        name="gdn_out",
    )(o, zs, gn, w, x)


def _chunk_tri(tm):
    t = np.arange(tm)
    same = (t[:, None] // GDN_CHUNK) == (t[None, :] // GDN_CHUNK)
    return jnp.asarray((same & (t[:, None] >= t[None, :])).astype(np.float32), BF16)


def _row(v):
    return v.reshape(1, -1).astype(F32)


def _pad_lanes(v, offset):
    out = jnp.zeros((1, LANES), F32)
    return out.at[0, offset:offset + v.shape[0]].set(v.astype(F32))


def _gated_deltanet(x, norm_g, w_in, conv_w, a_log, dt_bias, out_norm, w_out, batch, seq):
    t = x.shape[0]
    wqkv = w_in[:, :GDN_CONV_DIM].astype(BF16)
    wz = w_in[:, GDN_CONV_DIM:GDN_CONV_DIM + GDN_V].astype(BF16)
    wba = jnp.pad(w_in[:, GDN_CONV_DIM + GDN_V:], ((0, 0), (0, LANES - 2 * GDN_HEADS))).astype(BF16)
    q, k, v, zs, bw, gw, bg = _gdn_in(x, _row(norm_g), wqkv, wz, wba, conv_w.astype(F32),
                                      _pad_lanes(a_log, GDN_HEADS), _pad_lanes(dt_bias, GDN_HEADS),
                                      _chunk_tri(min(GDN_TM, seq)), seq)
    nc = seq // GDN_CHUNK
    grow = bg[:, GDN_HEADS:2 * GDN_HEADS].reshape(batch, nc, GDN_CHUNK, GDN_HEADS).transpose(0, 1, 3, 2)
    grow = grow.reshape(batch, nc, GDN_HEADS // GDN_GROUP, GDN_GROUP * GDN_CHUNK)
    shp = (batch, seq, GDN_QK)
    o = _gdn_chunks(q.reshape(shp), k.reshape(shp), v.reshape(shp), bw.reshape(shp), gw.reshape(shp), grow)
    return _gdn_out(o.reshape(t, GDN_V), zs, _row(out_norm), w_out.astype(BF16), x)


def _stick_breaking(x, norm_g, w_qkv, w_out, batch, seq):
    t = x.shape[0]
    n = SB_HEADS * SB_HD
    w = jnp.concatenate([w_qkv[:, :n] * (SB_HD ** -0.5 * LOG2E), w_qkv[:, n:]], axis=1).astype(BF16)
    qkv = _norm_matmul(x, _row(norm_g), w, 1024, BF16)
    o = _sb_attention(qkv.reshape(batch, seq, 3 * n), _sb_suffix_matrix(SB_TK))
    return _matmul_res(o.reshape(t, n), w_out.astype(BF16), x)


def _mla(x, norm_g, positions, w_in, q_norm, w_uq, kv_norm, w_ukv, w_out, batch, seq):
    t = x.shape[0]
    wq = w_in[:, :MLA_Q_RANK].astype(BF16)
    wkv = w_in[:, MLA_Q_RANK:MLA_Q_RANK + MLA_KV_RANK].astype(BF16)
    wkr = jnp.pad(w_in[:, MLA_Q_RANK + MLA_KV_RANK:], ((0, 0), (0, LANES - MLA_ROPE))).astype(BF16)
    qd = MLA_NOPE + MLA_ROPE
    wuq = jnp.pad(w_uq.reshape(MLA_Q_RANK, MLA_HEADS, qd), ((0, 0), (0, 0), (0, 2 * LANES - qd)))
    wuq = wuq.reshape(MLA_Q_RANK, MLA_HEADS * 2 * LANES).astype(BF16)
    half = MLA_ROPE // 2
    inv_freq = ROPE_THETA ** (-jnp.arange(half, dtype=F32) / half)
    invf = jnp.zeros((1, LANES), F32).at[0, :MLA_ROPE].set(jnp.concatenate([inv_freq, inv_freq]))
    wukv = w_ukv.reshape(MLA_KV_RANK, MLA_HEADS, MLA_NOPE + MLA_V)
    wuk = wukv[:, :, :MLA_NOPE].reshape(MLA_KV_RANK, MLA_HEADS * MLA_NOPE).astype(BF16)
    wuvt = wukv[:, :, MLA_NOPE:].reshape(MLA_KV_RANK, MLA_HEADS * MLA_V).T.astype(BF16)
    qc, kn, kr, vt = _mla_proj(x, _row(norm_g), wq, wkv, wkr, _row(q_norm), wuq, _row(kv_norm),
                               wuk, wuvt, positions.reshape(t, 1), invf)
    tk = vt.shape[-1]
    o = _mla_attention(qc.reshape(batch, seq, MLA_HEADS * 2 * LANES), kn.reshape(batch, seq, MLA_HEADS * MLA_NOPE),
                       kr.reshape(batch, seq, LANES), vt.reshape(batch, seq // tk, MLA_HEADS * MLA_V, tk))
    return _matmul_res(o.reshape(t, MLA_HEADS * MLA_V), w_out.astype(BF16), x)


def kernel(x, p, positions, ffn1_norm, ffn1_w_gate, ffn1_w_up, ffn1_w_down, mix_norm, ffn2_norm, ffn2_w_gate, ffn2_w_up, ffn2_w_down, ple_norm, ple_w_gate, ple_w_proj, gdn_w_in, gdn_conv_w, gdn_a_log, gdn_dt_bias, gdn_out_norm, gdn_w_out, sb_w_qkv, sb_w_out, mla_w_in, mla_q_norm, mla_w_uq, mla_kv_norm, mla_w_ukv, mla_w_out, final_norm):
    batch, seq, d = x.shape
    t = batch * seq
    depth = ffn1_norm.shape[0]
    x = x.reshape(t, d)
    p = p.reshape(depth, t, PLE_DIM)
    for i in range(depth):
        mixer, slot = i % N_MIXERS, i // N_MIXERS
        x = _ffn(x, _row(ffn1_norm[i]), ffn1_w_gate[i].astype(BF16), ffn1_w_up[i].astype(BF16),
                 ffn1_w_down[i].astype(BF16))
        if mixer == 0:
            x = _gated_deltanet(x, mix_norm[i], gdn_w_in[slot], gdn_conv_w[slot], gdn_a_log[slot],
                                gdn_dt_bias[slot], gdn_out_norm[slot], gdn_w_out[slot], batch, seq)
        elif mixer == 1:
            x = _stick_breaking(x, mix_norm[i], sb_w_qkv[slot], sb_w_out[slot], batch, seq)
        else:
            x = _mla(x, mix_norm[i], positions, mla_w_in[slot], mla_q_norm[slot], mla_w_uq[slot],
                     mla_kv_norm[slot], mla_w_ukv[slot], mla_w_out[slot], batch, seq)
        x = _ffn(x, _row(ffn2_norm[i]), ffn2_w_gate[i].astype(BF16), ffn2_w_up[i].astype(BF16),
                 ffn2_w_down[i].astype(BF16))
        x = _ple(x, _row(ple_norm[i]), ple_w_gate[i].astype(BF16), p[i], ple_w_proj[i].astype(BF16),
                 _row(final_norm), final=(i == depth - 1))
    return x.reshape(batch, seq, d)
```

```python
import functools

import numpy as np
import jax
import jax.numpy as jnp
from jax import lax
from jax.experimental import pallas as pl
from jax.experimental.pallas import tpu as pltpu

F32 = jnp.float32
BF16 = jnp.bfloat16

D_MODEL = 1024
DEPTH = 4
N_MIXERS = 3
PLE_DIM = 256
D_FF = 2816
EPS = 1e-6

GDN_HEADS = 8
GDN_DK = 128
GDN_DV = 128
GDN_CONV = 4
GDN_CHUNK = 64
GDN_QK = GDN_HEADS * GDN_DK
GDN_V = GDN_HEADS * GDN_DV
GDN_CONV_DIM = 2 * GDN_QK + GDN_V

SB_HEADS = 16
SB_HD = 64

MLA_HEADS = 8
MLA_Q_RANK = 384
MLA_KV_RANK = 256
MLA_NOPE = 128
MLA_ROPE = 64
MLA_V = 128
ROPE_THETA = 10000.0

LANES = 128
VMEM_LIMIT = 56 * 1024 * 1024

FFN_TM = 1024
FFN_CHUNK = 512
TOK_TM = 512
SB_TQ = 512
SB_TK = 128
MLA_TQ = 2 * TOK_TM
MLA_PAIR = 2
MLA_LROWS = 16

LOG2E = 1.4426950408889634
NEG_BIG = -1e30
SB_SPLIT_TERMS = 1
SB_DEAD_BITS = 150.0


def _cparams(*sem):
    return pltpu.CompilerParams(dimension_semantics=sem, vmem_limit_bytes=VMEM_LIMIT)


def _dot(a, b):
    return jnp.dot(a, b, preferred_element_type=F32)


def _dot_nt(a, b):
    return lax.dot_general(a, b, (((1,), (1,)), ((), ())), preferred_element_type=F32)


def _dot_tn(a, b):
    return lax.dot_general(a, b, (((0,), (0,)), ((), ())), preferred_element_type=F32)


def _rms(xf, g):
    return xf * lax.rsqrt(jnp.mean(xf * xf, axis=-1, keepdims=True) + EPS) * g


def _sigmoid(x):
    return 1.0 / (1.0 + jnp.exp2(x * -LOG2E))


def _softplus(x):
    return jnp.maximum(x, 0.0) + jnp.log(1.0 + jnp.exp(-jnp.abs(x)))


def _split_dot(x, m_bf16, terms):
    out = None
    r = x
    for i in range(terms):
        p = r.astype(BF16)
        d = _dot(p, m_bf16)
        out = d if out is None else out + d
        if i + 1 < terms:
            r = r - p.astype(F32)
    return out


def _ffn_kernel(x_ref, g_ref, wg_ref, wu_ref, wd_ref, o_ref):
    x = x_ref[...]
    h = _rms(x, g_ref[...]).astype(BF16)
    acc = None
    for lo in range(0, D_FF, FFN_CHUNK):
        hi = min(lo + FFN_CHUNK, D_FF)
        gate = _dot(h, wg_ref[:, lo:hi])
        up = _dot(h, wu_ref[:, lo:hi])
        act = (gate * _sigmoid(gate) * up).astype(BF16)
        part = _dot(act, wd_ref[lo:hi, :])
        acc = part if acc is None else acc + part
    o_ref[...] = x + 0.5 * acc


def _ffn(x, g, wg, wu, wd):
    t, d = x.shape
    tm = min(FFN_TM, t)
    once = lambda a: pl.BlockSpec(a.shape, lambda i: (0, 0), pipeline_mode=pl.Buffered(1))
    return pl.pallas_call(
        _ffn_kernel,
        out_shape=jax.ShapeDtypeStruct((t, d), F32),
        grid=(t // tm,),
        in_specs=[pl.BlockSpec((tm, d), lambda i: (i, 0)), once(g), once(wg), once(wu), once(wd)],
        out_specs=pl.BlockSpec((tm, d), lambda i: (i, 0)),
        compiler_params=_cparams("parallel"),
        name="ffn",
    )(x, g, wg, wu, wd)


def _ple_kernel(x_ref, g_ref, wgate_ref, p_ref, wp_ref, fin_ref, o_ref, *, final):
    x = x_ref[...]
    h = _rms(x, g_ref[...]).astype(BF16)
    gate = _sigmoid(_dot(h, wgate_ref[...]))
    y = x + gate * _dot(p_ref[...].astype(BF16), wp_ref[...])
    if final:
        y = _rms(y, fin_ref[...])
    o_ref[...] = y


def _ple(x, g, wgate, p, wp, fin, final):
    t, d = x.shape
    tm = min(FFN_TM, t)
    return pl.pallas_call(
        functools.partial(_ple_kernel, final=final),
        out_shape=jax.ShapeDtypeStruct((t, d), F32),
        grid=(t // tm,),
        in_specs=[
            pl.BlockSpec((tm, d), lambda i: (i, 0)),
            pl.BlockSpec((1, d), lambda i: (0, 0)),
            pl.BlockSpec((d, d), lambda i: (0, 0)),
            pl.BlockSpec((tm, PLE_DIM), lambda i: (i, 0)),
            pl.BlockSpec((PLE_DIM, d), lambda i: (0, 0)),
            pl.BlockSpec((1, d), lambda i: (0, 0)),
        ],
        out_specs=pl.BlockSpec((tm, d), lambda i: (i, 0)),
        compiler_params=_cparams("parallel"),
        name="ple",
    )(x, g, wgate, p, wp, fin)


def _norm_matmul_kernel(x_ref, g_ref, w_ref, o_ref, *, tn):
    h = _rms(x_ref[...], g_ref[...]).astype(BF16)
    for lo in range(0, w_ref.shape[1], tn):
        o_ref[:, lo:lo + tn] = _dot(h, w_ref[:, lo:lo + tn]).astype(o_ref.dtype)


def _norm_matmul(x, g, w, tn, out_dtype):
    t, d = x.shape
    n = w.shape[1]
    tm = min(FFN_TM, t)
    once = lambda a: pl.BlockSpec(a.shape, lambda i: (0, 0), pipeline_mode=pl.Buffered(1))
    return pl.pallas_call(
        functools.partial(_norm_matmul_kernel, tn=tn),
        out_shape=jax.ShapeDtypeStruct((t, n), out_dtype),
        grid=(t // tm,),
        in_specs=[pl.BlockSpec((tm, d), lambda i: (i, 0)), once(g), once(w)],
        out_specs=pl.BlockSpec((tm, n), lambda i: (i, 0)),
        compiler_params=_cparams("parallel"),
        name="norm_matmul",
    )(x, g, w)


def _matmul_res_kernel(a_ref, w_ref, x_ref, o_ref):
    o_ref[...] = x_ref[...] + _dot(a_ref[...].astype(BF16), w_ref[...])


def _matmul_res(a, w, x):
    t, k = a.shape
    d = w.shape[1]
    tm = min(FFN_TM, t)
    return pl.pallas_call(
        _matmul_res_kernel,
        out_shape=jax.ShapeDtypeStruct((t, d), F32),
        grid=(t // tm,),
        in_specs=[
            pl.BlockSpec((tm, k), lambda i: (i, 0)),
            pl.BlockSpec((k, d), lambda i: (0, 0)),
            pl.BlockSpec((tm, d), lambda i: (i, 0)),
        ],
        out_specs=pl.BlockSpec((tm, d), lambda i: (i, 0)),
        compiler_params=_cparams("parallel"),
        name="matmul_res",
    )(a, w, x)


def _sb_kernel(q_ref, k_ref, v_ref, u_ref, o_ref,
               z0, z1, lb0, lb1, sf0, sf1, a0, a1, run_a, run_b, acc_o, *, tq, tk):
    qi = pl.program_id(2)
    ratio = tq // tk
    n_tiles = ratio * (qi + 1)
    q = q_ref[0]
    u2 = u_ref[...]
    zb, lbb, sfb, ab = (z0, z1), (lb0, lb1), (sf0, sf1), (a0, a1)
    first = lax.broadcasted_iota(jnp.int32, (tk, LANES), 1) < SB_HD
    row_minus_col = (lax.broadcasted_iota(jnp.int32, (tq, 2 * tk), 0)
                     - (lax.broadcasted_iota(jnp.int32, (tq, 2 * tk), 1) & (tk - 1)))
    acc_o[...] = jnp.zeros_like(acc_o)
    run_a[...] = jnp.zeros_like(run_a)
    run_b[...] = jnp.zeros_like(run_b)

    def blockdiag(t):
        zero = jnp.zeros_like(t)
        return jnp.concatenate([jnp.where(first, t, zero), jnp.where(first, zero, t)], axis=0)

    def key_start(t):
        kb = jnp.maximum(ratio * qi + ratio - 1 - t, 0)
        return pl.multiple_of(kb * tk, tk)

    def stage_q(t, par):
        kk = blockdiag(k_ref[0, pl.ds(key_start(t), tk), :])
        zb[par][...] = _dot_nt(q, kk)

    def stage_s1(par, thr):
        z = zb[par][...]
        neg_abs = lax.bitcast_convert_type(
            lax.bitcast_convert_type(z, jnp.uint32) | jnp.uint32(0x80000000), F32)
        sp = jnp.maximum(z, 0.0) + jnp.log(1.0 + jnp.exp2(neg_abs)) * LOG2E
        ra, rb = run_a[...], run_b[...]
        lb = z - sp - jnp.concatenate([ra, rb], axis=1)
        if thr is not None:
            valid = row_minus_col > thr
            sp = jnp.where(valid, sp, 0.0)
            lb = jnp.where(valid, lb, NEG_BIG)
        lbb[par][...] = lb
        sfb[par][...] = _split_dot(sp, u2, SB_SPLIT_TERMS)
        run_a[...] = ra + jnp.sum(sp[:, :tk], axis=1, keepdims=True)
        run_b[...] = rb + jnp.sum(sp[:, tk:], axis=1, keepdims=True)

    def stage_s2(par):
        ab[par][...] = jnp.exp2(lbb[par][...] - sfb[par][...]).astype(BF16)

    def stage_p(t, par):
        vv = blockdiag(v_ref[0, pl.ds(key_start(t), tk), :])
        acc_o[...] += _dot(ab[par][...], vv)

    def step(u, par, thr=None, q_=True, s1=True, s2=True, p_=True):
        if q_:
            stage_q(u, par)
        if s1:
            stage_s1(1 - par, thr)
        if s2:
            stage_s2(par)
        if p_:
            stage_p(u - 3, 1 - par)

    for u in range(ratio + 1):
        step(u, u % 2, thr=(ratio - u) * tk, s1=u >= 1, s2=u >= 2, p_=u >= 3)

    n_pairs = (n_tiles - ratio) // 2

    def body(c):
        j, _ = c
        u = ratio + 1 + 2 * j
        step(u, (ratio + 1) % 2)
        step(u + 1, ratio % 2)
        return j + 1, jnp.min(jnp.minimum(run_a[...], run_b[...]))

    def live(c):
        j, lowest = c
        return (j < n_pairs) & (lowest < SB_DEAD_BITS)

    j_end, _ = lax.while_loop(live, body, (jnp.int32(0), jnp.float32(0.0)))
    n_done = ratio + 2 * j_end
    step(n_done + 1, (ratio + 1) % 2, q_=False, s1=False)
    step(n_done + 2, ratio % 2, q_=False, s1=False, s2=False)
    o_ref[0] = acc_o[...].astype(o_ref.dtype)


def _sb_attention(qkv, u2):
    b, l, _ = qkv.shape
    tq, tk = min(SB_TQ, l), SB_TK
    assert (tq // tk) % 2 == 0 and tk == LANES
    npair = SB_HEADS * SB_HD // LANES
    wide = pltpu.VMEM((tq, 2 * tk), F32)
    col = pltpu.VMEM((tq, LANES), F32)
    return pl.pallas_call(
        functools.partial(_sb_kernel, tq=tq, tk=tk),
        out_shape=jax.ShapeDtypeStruct((b, l, SB_HEADS * SB_HD), BF16),
        grid=(b, npair, l // tq),
        in_specs=[
            pl.BlockSpec((1, tq, LANES), lambda bi, j, qi: (bi, qi, j)),
            pl.BlockSpec((1, l, LANES), lambda bi, j, qi: (bi, 0, npair + j)),
            pl.BlockSpec((1, l, LANES), lambda bi, j, qi: (bi, 0, 2 * npair + j)),
            pl.BlockSpec((2 * tk, 2 * tk), lambda bi, j, qi: (0, 0)),
        ],
        out_specs=pl.BlockSpec((1, tq, LANES), lambda bi, j, qi: (bi, qi, j)),
        scratch_shapes=[wide, wide, wide, wide, wide, wide,
                        pltpu.VMEM((tq, 2 * tk), BF16), pltpu.VMEM((tq, 2 * tk), BF16),
                        col, col, pltpu.VMEM((tq, LANES), F32)],
        compiler_params=_cparams("parallel", "parallel", "arbitrary"),
        name="sb_attention",
    )(qkv, qkv, qkv, u2)


def _sb_suffix_matrix(tk):
    s = np.arange(2 * tk)
    same = (s[:, None] // tk) == (s[None, :] // tk)
    return jnp.asarray((same & (s[:, None] > s[None, :])).astype(np.float32), BF16)


def _mla_proj_kernel(x_ref, g_ref, wq_ref, wkv_ref, wkr_ref, qn_ref, wuq_ref, kvn_ref, wuk_ref, wuvt_ref,
                     pos_ref, invf_ref, q_out, kn_out, kr_out, vt_out):
    h = _rms(x_ref[...], g_ref[...]).astype(BF16)
    cq = _dot(h, wq_ref[...])
    ckv = _dot(h, wkv_ref[...])
    kr = _dot(h, wkr_ref[...])
    q = _dot(_rms(cq, qn_ref[...]).astype(BF16), wuq_ref[...])
    ckv_n = _rms(ckv, kvn_ref[...]).astype(BF16)
    kn_out[...] = _dot(ckv_n, wuk_ref[...]).astype(kn_out.dtype)
    vt_out[0] = _dot_nt(wuvt_ref[...], ckv_n).astype(vt_out.dtype)

    ang = pos_ref[...].astype(F32) * invf_ref[...]
    cos, sin = jnp.cos(ang), jnp.sin(ang)
    lane = lax.broadcasted_iota(jnp.int32, ang.shape, 1)
    half = MLA_ROPE // 2
    sin_lo = jnp.where(lane < half, -sin, 0.0)
    sin_hi = jnp.where((lane >= half) & (lane < MLA_ROPE), sin, 0.0)

    def rope(t):
        return (t * cos + pltpu.roll(t, LANES - half, 1) * sin_lo + pltpu.roll(t, half, 1) * sin_hi)

    kr_out[...] = rope(kr).astype(kr_out.dtype)
    scale = (MLA_NOPE + MLA_ROPE) ** -0.5 * LOG2E
    for hh in range(MLA_HEADS):
        base = hh * 2 * LANES
        q_out[:, base:base + LANES] = (q[:, base:base + LANES] * scale).astype(q_out.dtype)
        q_out[:, base + LANES:base + 2 * LANES] = (
            rope(q[:, base + LANES:base + 2 * LANES]) * scale).astype(q_out.dtype)


def _mla_proj(x, g, wq, wkv, wkr, qn, wuq, kvn, wuk, wuvt, pos, invf):
    t, d = x.shape
    tm = min(TOK_TM, t)
    full = lambda a: pl.BlockSpec(a.shape, lambda i: (0, 0))
    nq = MLA_HEADS * 2 * LANES
    nv = MLA_HEADS * MLA_V
    return pl.pallas_call(
        _mla_proj_kernel,
        out_shape=(jax.ShapeDtypeStruct((t, nq), BF16), jax.ShapeDtypeStruct((t, MLA_HEADS * MLA_NOPE), BF16),
                   jax.ShapeDtypeStruct((t, LANES), BF16), jax.ShapeDtypeStruct((t // tm, nv, tm), BF16)),
        grid=(t // tm,),
        in_specs=[pl.BlockSpec((tm, d), lambda i: (i, 0)), full(g), full(wq), full(wkv), full(wkr),
                  full(qn), full(wuq), full(kvn), full(wuk), full(wuvt),
                  pl.BlockSpec((tm, 1), lambda i: (i, 0)), full(invf)],
        out_specs=(pl.BlockSpec((tm, nq), lambda i: (i, 0)),
                   pl.BlockSpec((tm, MLA_HEADS * MLA_NOPE), lambda i: (i, 0)),
                   pl.BlockSpec((tm, LANES), lambda i: (i, 0)),
                   pl.BlockSpec((1, nv, tm), lambda i: (i, 0, 0))),
        compiler_params=_cparams("parallel"),
        name="mla_proj",
    )(x, g, wq, wkv, wkr, qn, wuq, kvn, wuk, wuvt, pos, invf)


def _mla_attn_kernel(q_ref, kn_ref, kr_ref, vt_ref, o_ref, s0, s1, p0, p1, c0, c1, x0, x1, m_s, acc_s,
                     *, tq, tk):
    qi = pl.program_id(2)
    sb, pb, cb, xb = (s0, s1), (p0, p1), (c0, c1), (x0, x1)
    m_s[...] = jnp.full_like(m_s, -jnp.inf)
    acc_s[...] = jnp.zeros_like(acc_s)
    p1[...] = jnp.zeros_like(p1)
    c1[...] = jnp.ones_like(c1)
    query_minus_key = (lax.broadcasted_iota(jnp.int32, (tk, tq), 1)
                       - lax.broadcasted_iota(jnp.int32, (tk, tq), 0))

    def tile_index(t):
        return jnp.maximum(t, 0)

    def stage_q(hd, t, par):
        start = pl.multiple_of(tile_index(t) * tk, tk)
        kc = jnp.concatenate([kn_ref[0, pl.ds(start, tk), hd * LANES:(hd + 1) * LANES],
                              kr_ref[0, pl.ds(start, tk), :]], axis=1)
        q = q_ref[0, :, hd * 2 * LANES:(hd + 1) * 2 * LANES]
        s = _dot_nt(kc, q)
        sb[par][hd] = s
        xb[par][hd] = jnp.max(s, axis=0, keepdims=True)

    def stage_s(hd, par, thr):
        s = sb[par][hd]
        if thr is None:
            tile_max = xb[par][hd]
        else:
            s = jnp.where(query_minus_key >= thr, s, -jnp.inf)
            tile_max = jnp.max(s, axis=0, keepdims=True)
        m_prev = m_s[hd]
        m_new = jnp.maximum(m_prev, tile_max)
        p = jnp.exp2(s - m_new)
        corr = jnp.exp2(m_prev - m_new)
        pb[par][hd] = p.astype(BF16)
        cb[par][hd] = corr
        m_s[hd] = m_new

    ones_rows = jnp.ones((MLA_LROWS, tk), BF16)

    def stage_p(hd, t, par):
        v_t = jnp.concatenate([vt_ref[0, tile_index(t), hd * MLA_V:(hd + 1) * MLA_V, :], ones_rows], axis=0)
        acc_s[hd] = acc_s[hd] * cb[par][hd] + _dot(v_t, pb[par][hd])

    def step(t, par, thr=None, q_=True, s_=True, p_=True):
        for hd in range(MLA_PAIR):
            if q_:
                stage_q(hd, t + 1, 1 - par)
            if s_:
                stage_s(hd, par, thr)
            if p_:
                stage_p(hd, t - 1, 1 - par)

    for hd in range(MLA_PAIR):
        stage_q(hd, 0, 0)

    def body(j, c):
        step(2 * j, 0)
        step(2 * j + 1, 1)
        return c

    lax.fori_loop(0, qi, body, 0)
    d0 = 2 * qi
    step(d0, 0, thr=0)
    step(d0 + 1, 1, thr=tk, q_=False)
    step(d0 + 2, 0, q_=False, s_=False)
    for hd in range(MLA_PAIR):
        acc = acc_s[hd]
        o_ref[0, :, hd * MLA_V:(hd + 1) * MLA_V] = (acc[:MLA_V] / acc[MLA_V:MLA_V + 1]).T.astype(o_ref.dtype)


def _mla_attention(qc, kn, kr, vt):
    b, l, _ = qc.shape
    tq, tk = min(MLA_TQ, l), vt.shape[-1]
    assert tq == 2 * tk and tk % LANES == 0 and MLA_V == LANES and MLA_NOPE == LANES
    pair = MLA_PAIR
    row = pltpu.VMEM((pair, 1, tq), F32)
    once = dict(pipeline_mode=pl.Buffered(1))
    return pl.pallas_call(
        functools.partial(_mla_attn_kernel, tq=tq, tk=tk),
        out_shape=jax.ShapeDtypeStruct((b, l, MLA_HEADS * MLA_V), BF16),
        grid=(b, MLA_HEADS // pair, l // tq),
        in_specs=[
            pl.BlockSpec((1, tq, pair * 2 * LANES), lambda bi, hh, qi: (bi, qi, hh)),
            pl.BlockSpec((1, l, pair * LANES), lambda bi, hh, qi: (bi, 0, hh), **once),
            pl.BlockSpec((1, l, LANES), lambda bi, hh, qi: (bi, 0, 0), **once),
            pl.BlockSpec((1, l // tk, pair * MLA_V, tk), lambda bi, hh, qi: (bi, 0, hh, 0), **once),
        ],
        out_specs=pl.BlockSpec((1, tq, pair * LANES), lambda bi, hh, qi: (bi, qi, hh)),
        scratch_shapes=[pltpu.VMEM((pair, tk, tq), F32), pltpu.VMEM((pair, tk, tq), F32),
                        pltpu.VMEM((pair, tk, tq), BF16), pltpu.VMEM((pair, tk, tq), BF16),
                        row, row, row, row, row, pltpu.VMEM((pair, MLA_V + MLA_LROWS, tq), F32)],
        compiler_params=_cparams("parallel", "parallel", "arbitrary"),
        name="mla_attention",
    )(qc, kn, kr, vt)


GDN_TM = 256
GDN_GROUP = 4
GDN_STEP_CHUNKS = 2


def _gdn_in_kernel(x_ref, g_ref, wqkv_ref, wz_ref, wba_ref, cw_ref, alog_ref, dtb_ref, tri_ref,
                   q_out, k_out, v_out, z_out, bw_out, gw_out, bg_out, pre_s, *, tm, tiles_per_seq):
    i = pl.program_id(0)
    h = _rms(x_ref[...], g_ref[...]).astype(BF16)

    @pl.when(i % tiles_per_seq == 0)
    def _():
        pre_s[...] = jnp.zeros_like(pre_s)

    row8 = lax.broadcasted_iota(jnp.int32, (8, 2 * GDN_DK), 0)
    ba = _dot(h, wba_ref[...])
    lane = lax.broadcasted_iota(jnp.int32, ba.shape, 1)
    beta = _sigmoid(ba)
    gval = -jnp.exp(alog_ref[...]) * _softplus(ba + dtb_ref[...])
    gcum = _split_dot_left(tri_ref[...], gval, 3)
    bg = jnp.where(lane < GDN_HEADS, beta, gcum)
    bg_out[...] = bg
    for hh in range(GDN_HEADS):
        sl = slice(hh * GDN_DK, (hh + 1) * GDN_DK)
        bw_out[:, sl] = jnp.broadcast_to(bg[:, hh:hh + 1], (tm, GDN_DK))
        gw_out[:, sl] = jnp.broadcast_to(bg[:, GDN_HEADS + hh:GDN_HEADS + hh + 1], (tm, GDN_DK))

    cw = cw_ref[...]
    width = 2 * GDN_DK
    for blk in range(GDN_CONV_DIM // width):
        cs = slice(blk * width, (blk + 1) * width)
        pre = _dot(h, wqkv_ref[:, cs])
        tail = pre_s[:, cs]
        conv = pre * cw[3:4, cs]
        for j in range(1, GDN_CONV):
            rolled = pltpu.roll(pre, j, 0)
            first = jnp.where(row8 < j, pltpu.roll(tail, j, 0), rolled[0:8])
            conv = conv + jnp.concatenate([first, rolled[8:]], axis=0) * cw[3 - j:4 - j, cs]
        pre_s[:, cs] = pre[tm - 8:tm]
        y = conv * _sigmoid(conv)
        if blk * width >= 2 * GDN_QK:
            v_out[:, blk * width - 2 * GDN_QK:(blk + 1) * width - 2 * GDN_QK] = y
            continue
        is_q = blk * width < GDN_QK
        out = q_out if is_q else k_out
        for half in range(2):
            yh = y[:, half * GDN_DK:(half + 1) * GDN_DK]
            yn = yh * lax.rsqrt(jnp.sum(yh * yh, axis=1, keepdims=True) + EPS)
            col = (blk * width) % GDN_QK + half * GDN_DK
            out[:, col:col + GDN_DK] = yn * GDN_DK ** -0.5 if is_q else yn

    for blk in range(GDN_V // width):
        cs = slice(blk * width, (blk + 1) * width)
        z = _dot(h, wz_ref[:, cs])
        z_out[:, cs] = z * _sigmoid(z)


def _split_dot_left(m_bf16, x, terms):
    out = None
    r = x
    for i in range(terms):
        p = r.astype(BF16)
        d = _dot(m_bf16, p)
        out = d if out is None else out + d
        if i + 1 < terms:
            r = r - p.astype(F32)
    return out


def _gdn_in(x, g, wqkv, wz, wba, cw, alog, dtb, tri, seq):
    t, d = x.shape
    tm = min(GDN_TM, seq)
    full = lambda a: pl.BlockSpec(a.shape, lambda i: (0, 0))
    row = lambda n: pl.BlockSpec((tm, n), lambda i: (i, 0))
    wide = jax.ShapeDtypeStruct((t, GDN_QK), F32)
    return pl.pallas_call(
        functools.partial(_gdn_in_kernel, tm=tm, tiles_per_seq=seq // tm),
        out_shape=(wide, wide, wide, wide, wide, wide, jax.ShapeDtypeStruct((t, LANES), F32)),
        grid=(t // tm,),
        in_specs=[row(d), full(g), full(wqkv), full(wz), full(wba), full(cw), full(alog), full(dtb), full(tri)],
        out_specs=(row(GDN_QK),) * 6 + (row(LANES),),
        scratch_shapes=[pltpu.VMEM((8, GDN_CONV_DIM), F32)],
        compiler_params=_cparams("arbitrary"),
        name="gdn_in",
    )(x, g, wqkv, wz, wba, cw, alog, dtb, tri)


def _gdn_chunk_kernel(q_ref, k_ref, v_ref, bw_ref, gw_ref, gr_ref, o_ref, s_ref, *, nchunk):
    c, grp = GDN_CHUNK, GDN_GROUP
    rows = grp * c

    @pl.when(pl.program_id(1) == 0)
    def _():
        s_ref[...] = jnp.zeros_like(s_ref)

    r = lax.broadcasted_iota(jnp.int32, (rows, rows), 0)
    cc = lax.broadcasted_iota(jnp.int32, (rows, rows), 1)
    same = lax.shift_right_logical(r, 6) == lax.shift_right_logical(cc, 6)
    incl = same & (r >= cc)
    strict = same & (r > cc)
    own_block = (lax.shift_right_logical(lax.broadcasted_iota(jnp.int32, (rows, grp * GDN_DK), 0), 6)
                 == lax.shift_right_logical(lax.broadcasted_iota(jnp.int32, (rows, grp * GDN_DK), 1), 7))

    def spread(t):
        t16 = t.astype(BF16)
        return jnp.where(own_block, jnp.concatenate([t16] * grp, axis=1), jnp.zeros((), BF16))

    ngrp = GDN_HEADS // grp
    units = [(ci, g) for ci in range(nchunk) for g in range(ngrp)]

    def stack(ref, ci, g):
        return jnp.concatenate(
            [ref[0, ci * c:(ci + 1) * c, (g * grp + j) * GDN_DK:(g * grp + j + 1) * GDN_DK] for j in range(grp)],
            axis=0)

    ks = [stack(k_ref, ci, g) for ci, g in units]
    gw = [stack(gw_ref, ci, g) for ci, g in units]
    bw = [stack(bw_ref, ci, g) for ci, g in units]
    k16 = [t.astype(BF16) for t in ks]
    kb = [ks[i] * bw[i] for i in range(len(units))]
    eg = [jnp.exp(t) for t in gw]
    decay = []
    for i, (ci, g) in enumerate(units):
        gdiff = jnp.concatenate([gw[i], gw[i]], axis=1) - gr_ref[0, ci, g:g + 1, :]
        decay.append(jnp.where(incl, jnp.exp(jnp.where(incl, gdiff, 0.0)), 0.0))
    p = [jnp.where(strict, _dot_nt(kb[i].astype(BF16), k16[i]) * decay[i], 0.0).astype(BF16)
         for i in range(len(units))]
    y = [jnp.concatenate([stack(v_ref, ci, g) * bw[i], kb[i] * eg[i]], axis=1)
         for i, (ci, g) in enumerate(units)]
    y = [y[i] - _dot(p[i], y[i].astype(BF16)) for i in range(len(units))]
    step = 2
    while step < c:
        p = [_dot(t, t).astype(BF16) for t in p]
        y = [y[i] + _dot(p[i], y[i].astype(BF16)) for i in range(len(units))]
        step *= 2
    qs = [stack(q_ref, ci, g) for ci, g in units]
    attn = [(_dot_nt(qs[i].astype(BF16), k16[i]) * decay[i]).astype(BF16) for i in range(len(units))]

    for ci in range(nchunk):
        idx = [ci * ngrp + g for g in range(ngrp)]
        s = [s_ref[g] for g in range(ngrp)]
        ws, qs_s = [], []
        for g, i in enumerate(idx):
            qd = qs[i] * eg[i]
            prod = [_dot(jnp.concatenate([y[i][j * c:(j + 1) * c, GDN_DV:], qd[j * c:(j + 1) * c]],
                                         axis=0).astype(BF16),
                         s[g][j * GDN_DK:(j + 1) * GDN_DK].astype(BF16)) for j in range(grp)]
            ws.append(jnp.concatenate([t[:c] for t in prod], axis=0))
            qs_s.append(jnp.concatenate([t[c:] for t in prod], axis=0))
        v16 = [(y[i][:, :GDN_DV] - ws[g]).astype(BF16) for g, i in enumerate(idx)]
        o = [qs_s[g] + _dot(attn[i], v16[g]) for g, i in enumerate(idx)]
        for g, i in enumerate(idx):
            last = [gw[i][j * c + c - 1:j * c + c, :] for j in range(grp)]
            g_last = jnp.concatenate([jnp.broadcast_to(t, (c, GDN_DK)) for t in last], axis=0)
            s_decay = jnp.concatenate([jnp.broadcast_to(jnp.exp(t), (GDN_DK, GDN_DV)) for t in last], axis=0)
            kd = ks[i] * jnp.exp(g_last - gw[i])
            s_ref[g] = s[g] * s_decay + _dot_tn(spread(kd), v16[g])
            for j in range(grp):
                o_ref[0, ci * c:(ci + 1) * c, (g * grp + j) * GDN_DV:(g * grp + j + 1) * GDN_DV] = (
                    o[g][j * c:(j + 1) * c])


def _gdn_chunks(q, k, v, bw, gw, grow):
    b, l, n = q.shape
    nchunk = GDN_STEP_CHUNKS
    tl = nchunk * GDN_CHUNK
    ngrp = GDN_HEADS // GDN_GROUP
    blk = pl.BlockSpec((1, tl, n), lambda bi, ci: (bi, ci, 0))
    return pl.pallas_call(
        functools.partial(_gdn_chunk_kernel, nchunk=nchunk),
        out_shape=jax.ShapeDtypeStruct((b, l, n), F32),
        grid=(b, l // tl),
        in_specs=[blk, blk, blk, blk, blk,
                  pl.BlockSpec((1, nchunk, ngrp, GDN_GROUP * GDN_CHUNK), lambda bi, ci: (bi, ci, 0, 0))],
        out_specs=blk,
        scratch_shapes=[pltpu.VMEM((ngrp, GDN_GROUP * GDN_DK, GDN_DV), F32)],
        compiler_params=_cparams("parallel", "arbitrary"),
        name="gdn_chunks",
    )(q, k, v, bw, gw, grow)


def _gdn_out_kernel(o_ref, z_ref, gn_ref, w_ref, x_ref, out_ref, a_s):
    gn = gn_ref[...]
    for hh in range(GDN_HEADS):
        sl = slice(hh * GDN_DV, (hh + 1) * GDN_DV)
        a_s[:, sl] = (_rms(o_ref[:, sl], gn) * z_ref[:, sl]).astype(BF16)
    out_ref[...] = x_ref[...] + _dot(a_s[...], w_ref[...])


def _gdn_out(o, zs, gn, w, x):
    t, d = x.shape
    tm = min(FFN_TM, t)
    row = lambda n: pl.BlockSpec((tm, n), lambda i: (i, 0))
    full = lambda a: pl.BlockSpec(a.shape, lambda i: (0, 0))
    return pl.pallas_call(
        _gdn_out_kernel,
        out_shape=jax.ShapeDtypeStruct((t, d), F32),
        grid=(t // tm,),
        in_specs=[row(GDN_V), row(GDN_V), full(gn), full(w), row(d)],
        out_specs=row(d),
        scratch_shapes=[pltpu.VMEM((tm, GDN_V), BF16)],
        compiler_params=_cparams("parallel"),
        name="gdn_out",
    )(o, zs, gn, w, x)


def _chunk_tri(tm):
    t = np.arange(tm)
    same = (t[:, None] // GDN_CHUNK) == (t[None, :] // GDN_CHUNK)
    return jnp.asarray((same & (t[:, None] >= t[None, :])).astype(np.float32), BF16)


def _row(v):
    return v.reshape(1, -1).astype(F32)


def _pad_lanes(v, offset):
    out = jnp.zeros((1, LANES), F32)
    return out.at[0, offset:offset + v.shape[0]].set(v.astype(F32))


def _gated_deltanet(x, norm_g, w_in, conv_w, a_log, dt_bias, out_norm, w_out, batch, seq):
    t = x.shape[0]
    wqkv = w_in[:, :GDN_CONV_DIM].astype(BF16)
    wz = w_in[:, GDN_CONV_DIM:GDN_CONV_DIM + GDN_V].astype(BF16)
    wba = jnp.pad(w_in[:, GDN_CONV_DIM + GDN_V:], ((0, 0), (0, LANES - 2 * GDN_HEADS))).astype(BF16)
    q, k, v, zs, bw, gw, bg = _gdn_in(x, _row(norm_g), wqkv, wz, wba, conv_w.astype(F32),
                                      _pad_lanes(a_log, GDN_HEADS), _pad_lanes(dt_bias, GDN_HEADS),
                                      _chunk_tri(min(GDN_TM, seq)), seq)
    nc = seq // GDN_CHUNK
    grow = bg[:, GDN_HEADS:2 * GDN_HEADS].reshape(batch, nc, GDN_CHUNK, GDN_HEADS).transpose(0, 1, 3, 2)
    grow = grow.reshape(batch, nc, GDN_HEADS // GDN_GROUP, GDN_GROUP * GDN_CHUNK)
    shp = (batch, seq, GDN_QK)
    o = _gdn_chunks(q.reshape(shp), k.reshape(shp), v.reshape(shp), bw.reshape(shp), gw.reshape(shp), grow)
    return _gdn_out(o.reshape(t, GDN_V), zs, _row(out_norm), w_out.astype(BF16), x)


def _stick_breaking(x, norm_g, w_qkv, w_out, batch, seq):
    t = x.shape[0]
    n = SB_HEADS * SB_HD
    w = jnp.concatenate([w_qkv[:, :n] * (SB_HD ** -0.5 * LOG2E), w_qkv[:, n:]], axis=1).astype(BF16)
    qkv = _norm_matmul(x, _row(norm_g), w, 1024, BF16)
    o = _sb_attention(qkv.reshape(batch, seq, 3 * n), _sb_suffix_matrix(SB_TK))
    return _matmul_res(o.reshape(t, n), w_out.astype(BF16), x)


def _mla(x, norm_g, positions, w_in, q_norm, w_uq, kv_norm, w_ukv, w_out, batch, seq):
    t = x.shape[0]
    wq = w_in[:, :MLA_Q_RANK].astype(BF16)
    wkv = w_in[:, MLA_Q_RANK:MLA_Q_RANK + MLA_KV_RANK].astype(BF16)
    wkr = jnp.pad(w_in[:, MLA_Q_RANK + MLA_KV_RANK:], ((0, 0), (0, LANES - MLA_ROPE))).astype(BF16)
    qd = MLA_NOPE + MLA_ROPE
    wuq = jnp.pad(w_uq.reshape(MLA_Q_RANK, MLA_HEADS, qd), ((0, 0), (0, 0), (0, 2 * LANES - qd)))
    wuq = wuq.reshape(MLA_Q_RANK, MLA_HEADS * 2 * LANES).astype(BF16)
    half = MLA_ROPE // 2
    inv_freq = ROPE_THETA ** (-jnp.arange(half, dtype=F32) / half)
    invf = jnp.zeros((1, LANES), F32).at[0, :MLA_ROPE].set(jnp.concatenate([inv_freq, inv_freq]))
    wukv = w_ukv.reshape(MLA_KV_RANK, MLA_HEADS, MLA_NOPE + MLA_V)
    wuk = wukv[:, :, :MLA_NOPE].reshape(MLA_KV_RANK, MLA_HEADS * MLA_NOPE).astype(BF16)
    wuvt = wukv[:, :, MLA_NOPE:].reshape(MLA_KV_RANK, MLA_HEADS * MLA_V).T.astype(BF16)
    qc, kn, kr, vt = _mla_proj(x, _row(norm_g), wq, wkv, wkr, _row(q_norm), wuq, _row(kv_norm),
                               wuk, wuvt, positions.reshape(t, 1), invf)
    tk = vt.shape[-1]
    o = _mla_attention(qc.reshape(batch, seq, MLA_HEADS * 2 * LANES), kn.reshape(batch, seq, MLA_HEADS * MLA_NOPE),
                       kr.reshape(batch, seq, LANES), vt.reshape(batch, seq // tk, MLA_HEADS * MLA_V, tk))
    return _matmul_res(o.reshape(t, MLA_HEADS * MLA_V), w_out.astype(BF16), x)


def kernel(x, p, positions, ffn1_norm, ffn1_w_gate, ffn1_w_up, ffn1_w_down, mix_norm, ffn2_norm, ffn2_w_gate, ffn2_w_up, ffn2_w_down, ple_norm, ple_w_gate, ple_w_proj, gdn_w_in, gdn_conv_w, gdn_a_log, gdn_dt_bias, gdn_out_norm, gdn_w_out, sb_w_qkv, sb_w_out, mla_w_in, mla_q_norm, mla_w_uq, mla_kv_norm, mla_w_ukv, mla_w_out, final_norm):
    batch, seq, d = x.shape
    t = batch * seq
    depth = ffn1_norm.shape[0]
    x = x.reshape(t, d)
    p = p.reshape(depth, t, PLE_DIM)
    for i in range(depth):
        mixer, slot = i % N_MIXERS, i // N_MIXERS
        x = _ffn(x, _row(ffn1_norm[i]), ffn1_w_gate[i].astype(BF16), ffn1_w_up[i].astype(BF16),
                 ffn1_w_down[i].astype(BF16))
        if mixer == 0:
            x = _gated_deltanet(x, mix_norm[i], gdn_w_in[slot], gdn_conv_w[slot], gdn_a_log[slot],
                                gdn_dt_bias[slot], gdn_out_norm[slot], gdn_w_out[slot], batch, seq)
        elif mixer == 1:
            x = _stick_breaking(x, mix_norm[i], sb_w_qkv[slot], sb_w_out[slot], batch, seq)
        else:
            x = _mla(x, mix_norm[i], positions, mla_w_in[slot], mla_q_norm[slot], mla_w_uq[slot],
                     mla_kv_norm[slot], mla_w_ukv[slot], mla_w_out[slot], batch, seq)
        x = _ffn(x, _row(ffn2_norm[i]), ffn2_w_gate[i].astype(BF16), ffn2_w_up[i].astype(BF16),
                 ffn2_w_down[i].astype(BF16))
        x = _ple(x, _row(ple_norm[i]), ple_w_gate[i].astype(BF16), p[i], ple_w_proj[i].astype(BF16),
                 _row(final_norm), final=(i == depth - 1))
    return x.reshape(batch, seq, d)
```

```python
import functools

import numpy as np
import jax
import jax.numpy as jnp
from jax import lax
from jax.experimental import pallas as pl
from jax.experimental.pallas import tpu as pltpu

F32 = jnp.float32
BF16 = jnp.bfloat16

D_MODEL = 1024
DEPTH = 4
N_MIXERS = 3
PLE_DIM = 256
D_FF = 2816
EPS = 1e-6

GDN_HEADS = 8
GDN_DK = 128
GDN_DV = 128
GDN_CONV = 4
GDN_CHUNK = 64
GDN_QK = GDN_HEADS * GDN_DK
GDN_V = GDN_HEADS * GDN_DV
GDN_CONV_DIM = 2 * GDN_QK + GDN_V

SB_HEADS = 16
SB_HD = 64

MLA_HEADS = 8
MLA_Q_RANK = 384
MLA_KV_RANK = 256
MLA_NOPE = 128
MLA_ROPE = 64
MLA_V = 128
ROPE_THETA = 10000.0

LANES = 128
VMEM_LIMIT = 56 * 1024 * 1024

FFN_TM = 1024
FFN_CHUNK = 512
TOK_TM = 512
SB_TQ = 512
SB_TK = 128
MLA_TQ = 2 * TOK_TM
MLA_PAIR = 2
MLA_LROWS = 16

LOG2E = 1.4426950408889634
NEG_BIG = -1e30
SB_SPLIT_TERMS = 1
SB_DEAD_BITS = 150.0


def _cparams(*sem):
    return pltpu.CompilerParams(dimension_semantics=sem, vmem_limit_bytes=VMEM_LIMIT)


def _dot(a, b):
    return jnp.dot(a, b, preferred_element_type=F32)


def _dot_nt(a, b):
    return lax.dot_general(a, b, (((1,), (1,)), ((), ())), preferred_element_type=F32)


def _dot_tn(a, b):
    return lax.dot_general(a, b, (((0,), (0,)), ((), ())), preferred_element_type=F32)


def _rms(xf, g):
    return xf * lax.rsqrt(jnp.mean(xf * xf, axis=-1, keepdims=True) + EPS) * g


def _sigmoid(x):
    return 1.0 / (1.0 + jnp.exp2(x * -LOG2E))


def _softplus(x):
    return jnp.maximum(x, 0.0) + jnp.log(1.0 + jnp.exp(-jnp.abs(x)))


def _split_dot(x, m_bf16, terms):
    out = None
    r = x
    for i in range(terms):
        p = r.astype(BF16)
        d = _dot(p, m_bf16)
        out = d if out is None else out + d
        if i + 1 < terms:
            r = r - p.astype(F32)
    return out


def _ffn_kernel(x_ref, g_ref, wg_ref, wu_ref, wd_ref, o_ref):
    x = x_ref[...]
    h = _rms(x, g_ref[...]).astype(BF16)
    acc = None
    for lo in range(0, D_FF, FFN_CHUNK):
        hi = min(lo + FFN_CHUNK, D_FF)
        gate = _dot(h, wg_ref[:, lo:hi])
        up = _dot(h, wu_ref[:, lo:hi])
        act = (gate * _sigmoid(gate) * up).astype(BF16)
        part = _dot(act, wd_ref[lo:hi, :])
        acc = part if acc is None else acc + part
    o_ref[...] = x + 0.5 * acc


def _ffn(x, g, wg, wu, wd):
    t, d = x.shape
    tm = min(FFN_TM, t)
    once = lambda a: pl.BlockSpec(a.shape, lambda i: (0, 0), pipeline_mode=pl.Buffered(1))
    return pl.pallas_call(
        _ffn_kernel,
        out_shape=jax.ShapeDtypeStruct((t, d), F32),
        grid=(t // tm,),
        in_specs=[pl.BlockSpec((tm, d), lambda i: (i, 0)), once(g), once(wg), once(wu), once(wd)],
        out_specs=pl.BlockSpec((tm, d), lambda i: (i, 0)),
        compiler_params=_cparams("parallel"),
        name="ffn",
    )(x, g, wg, wu, wd)


def _ple_kernel(x_ref, g_ref, wgate_ref, p_ref, wp_ref, fin_ref, o_ref, *, final):
    x = x_ref[...]
    h = _rms(x, g_ref[...]).astype(BF16)
    gate = _sigmoid(_dot(h, wgate_ref[...]))
    y = x + gate * _dot(p_ref[0].astype(BF16), wp_ref[...])
    if final:
        y = _rms(y, fin_ref[...])
    o_ref[...] = y


def _ple(x, g, wgate, p, layer, wp, fin, final):
    t, d = x.shape
    tm = min(FFN_TM, t)
    return pl.pallas_call(
        functools.partial(_ple_kernel, final=final),
        out_shape=jax.ShapeDtypeStruct((t, d), F32),
        grid=(t // tm,),
        in_specs=[
            pl.BlockSpec((tm, d), lambda i: (i, 0)),
            pl.BlockSpec((1, d), lambda i: (0, 0)),
            pl.BlockSpec((d, d), lambda i: (0, 0)),
            pl.BlockSpec((1, tm, PLE_DIM), lambda i: (layer, i, 0)),
            pl.BlockSpec((PLE_DIM, d), lambda i: (0, 0)),
            pl.BlockSpec((1, d), lambda i: (0, 0)),
        ],
        out_specs=pl.BlockSpec((tm, d), lambda i: (i, 0)),
        compiler_params=_cparams("parallel"),
        name="ple",
    )(x, g, wgate, p, wp, fin)


def _norm_matmul_kernel(x_ref, g_ref, w_ref, o_ref, *, tn):
    h = _rms(x_ref[...], g_ref[...]).astype(BF16)
    for lo in range(0, w_ref.shape[1], tn):
        o_ref[:, lo:lo + tn] = _dot(h, w_ref[:, lo:lo + tn]).astype(o_ref.dtype)


def _norm_matmul(x, g, w, tn, out_dtype):
    t, d = x.shape
    n = w.shape[1]
    tm = min(FFN_TM, t)
    once = lambda a: pl.BlockSpec(a.shape, lambda i: (0, 0), pipeline_mode=pl.Buffered(1))
    return pl.pallas_call(
        functools.partial(_norm_matmul_kernel, tn=tn),
        out_shape=jax.ShapeDtypeStruct((t, n), out_dtype),
        grid=(t // tm,),
        in_specs=[pl.BlockSpec((tm, d), lambda i: (i, 0)), once(g), once(w)],
        out_specs=pl.BlockSpec((tm, n), lambda i: (i, 0)),
        compiler_params=_cparams("parallel"),
        name="norm_matmul",
    )(x, g, w)


def _matmul_res_kernel(a_ref, w_ref, x_ref, o_ref):
    o_ref[...] = x_ref[...] + _dot(a_ref[...].astype(BF16), w_ref[...])


def _matmul_res(a, w, x):
    t, k = a.shape
    d = w.shape[1]
    tm = min(FFN_TM, t)
    return pl.pallas_call(
        _matmul_res_kernel,
        out_shape=jax.ShapeDtypeStruct((t, d), F32),
        grid=(t // tm,),
        in_specs=[
            pl.BlockSpec((tm, k), lambda i: (i, 0)),
            pl.BlockSpec((k, d), lambda i: (0, 0)),
            pl.BlockSpec((tm, d), lambda i: (i, 0)),
        ],
        out_specs=pl.BlockSpec((tm, d), lambda i: (i, 0)),
        compiler_params=_cparams("parallel"),
        name="matmul_res",
    )(a, w, x)


def _sb_kernel(q_ref, k_ref, v_ref, u_ref, o_ref,
               z0, z1, lb0, lb1, sf0, sf1, a0, a1, run_a, run_b, acc_o, *, tq, tk):
    qi = pl.program_id(2)
    ratio = tq // tk
    n_tiles = ratio * (qi + 1)
    q = q_ref[0]
    u2 = u_ref[...]
    zb, lbb, sfb, ab = (z0, z1), (lb0, lb1), (sf0, sf1), (a0, a1)
    first = lax.broadcasted_iota(jnp.int32, (tk, LANES), 1) < SB_HD
    row_minus_col = (lax.broadcasted_iota(jnp.int32, (tq, 2 * tk), 0)
                     - (lax.broadcasted_iota(jnp.int32, (tq, 2 * tk), 1) & (tk - 1)))
    acc_o[...] = jnp.zeros_like(acc_o)
    run_a[...] = jnp.zeros_like(run_a)
    run_b[...] = jnp.zeros_like(run_b)

    def blockdiag(t):
        zero = jnp.zeros_like(t)
        return jnp.concatenate([jnp.where(first, t, zero), jnp.where(first, zero, t)], axis=0)

    def key_start(t):
        kb = jnp.maximum(ratio * qi + ratio - 1 - t, 0)
        return pl.multiple_of(kb * tk, tk)

    def stage_q(t, par, r0=0):
        kk = blockdiag(k_ref[0, pl.ds(key_start(t), tk), :])
        zb[par][r0:, :] = _dot_nt(q[r0:], kk)

    def stage_s1(par, thr, r0=0):
        z = zb[par][r0:, :]
        neg_abs = lax.bitcast_convert_type(
            lax.bitcast_convert_type(z, jnp.uint32) | jnp.uint32(0x80000000), F32)
        sp = jnp.maximum(z, 0.0) + jnp.log(1.0 + jnp.exp2(neg_abs)) * LOG2E
        ra, rb = run_a[r0:, :], run_b[r0:, :]
        lb = z - sp - jnp.concatenate([ra, rb], axis=1)
        if thr is not None:
            valid = row_minus_col[r0:] > thr
            sp = jnp.where(valid, sp, 0.0)
            lb = jnp.where(valid, lb, NEG_BIG)
        lbb[par][r0:, :] = lb
        sfb[par][r0:, :] = _split_dot(sp, u2, SB_SPLIT_TERMS)
        if r0:
            lbb[par][:r0, :] = jnp.full((r0, 2 * tk), NEG_BIG, F32)
            sfb[par][:r0, :] = jnp.zeros((r0, 2 * tk), F32)
        run_a[r0:, :] = ra + jnp.sum(sp[:, :tk], axis=1, keepdims=True)
        run_b[r0:, :] = rb + jnp.sum(sp[:, tk:], axis=1, keepdims=True)

    def stage_s2(par):
        ab[par][...] = jnp.exp2(lbb[par][...] - sfb[par][...]).astype(BF16)

    def stage_p(t, par):
        vv = blockdiag(v_ref[0, pl.ds(key_start(t), tk), :])
        acc_o[...] += _dot(ab[par][...], vv)

    def step(u, par, thr=None, q_=True, s1=True, s2=True, p_=True, q_r0=0):
        if q_:
            stage_q(u, par, q_r0)
        if s1:
            stage_s1(1 - par, thr, 0 if thr is None else thr)
        if s2:
            stage_s2(par)
        if p_:
            stage_p(u - 3, 1 - par)

    for u in range(ratio + 1):
        step(u, u % 2, thr=(ratio - u) * tk, s1=u >= 1, s2=u >= 2, p_=u >= 3,
             q_r0=max(ratio - 1 - u, 0) * tk)

    n_pairs = (n_tiles - ratio) // 2

    def body(c):
        j, _ = c
        u = ratio + 1 + 2 * j
        step(u, (ratio + 1) % 2)
        step(u + 1, ratio % 2)
        return j + 1, jnp.min(jnp.minimum(run_a[...], run_b[...]))

    def live(c):
        j, lowest = c
        return (j < n_pairs) & (lowest < SB_DEAD_BITS)

    j_end, _ = lax.while_loop(live, body, (jnp.int32(0), jnp.float32(0.0)))
    n_done = ratio + 2 * j_end
    step(n_done + 1, (ratio + 1) % 2, q_=False, s1=False)
    step(n_done + 2, ratio % 2, q_=False, s1=False, s2=False)
    o_ref[0] = acc_o[...].astype(o_ref.dtype)


def _sb_attention(qkv, u2):
    b, l, _ = qkv.shape
    tq, tk = min(SB_TQ, l), SB_TK
    assert (tq // tk) % 2 == 0 and tk == LANES
    npair = SB_HEADS * SB_HD // LANES
    wide = pltpu.VMEM((tq, 2 * tk), F32)
    col = pltpu.VMEM((tq, LANES), F32)
    return pl.pallas_call(
        functools.partial(_sb_kernel, tq=tq, tk=tk),
        out_shape=jax.ShapeDtypeStruct((b, l, SB_HEADS * SB_HD), BF16),
        grid=(b, npair, l // tq),
        in_specs=[
            pl.BlockSpec((1, tq, LANES), lambda bi, j, qi: (bi, qi, j)),
            pl.BlockSpec((1, l, LANES), lambda bi, j, qi: (bi, 0, npair + j)),
            pl.BlockSpec((1, l, LANES), lambda bi, j, qi: (bi, 0, 2 * npair + j)),
            pl.BlockSpec((2 * tk, 2 * tk), lambda bi, j, qi: (0, 0)),
        ],
        out_specs=pl.BlockSpec((1, tq, LANES), lambda bi, j, qi: (bi, qi, j)),
        scratch_shapes=[wide, wide, wide, wide, wide, wide,
                        pltpu.VMEM((tq, 2 * tk), BF16), pltpu.VMEM((tq, 2 * tk), BF16),
                        col, col, pltpu.VMEM((tq, LANES), F32)],
        compiler_params=_cparams("parallel", "parallel", "arbitrary"),
        name="sb_attention",
    )(qkv, qkv, qkv, u2)


def _sb_suffix_matrix(tk):
    s = np.arange(2 * tk)
    same = (s[:, None] // tk) == (s[None, :] // tk)
    return jnp.asarray((same & (s[:, None] > s[None, :])).astype(np.float32), BF16)


def _mla_proj_kernel(x_ref, g_ref, wq_ref, wkv_ref, wkr_ref, qn_ref, wuq_ref, kvn_ref, wuk_ref, wuvt_ref,
                     pos_ref, invf_ref, q_out, kn_out, kr_out, vt_out):
    h = _rms(x_ref[...], g_ref[...]).astype(BF16)
    cq = _dot(h, wq_ref[...])
    ckv = _dot(h, wkv_ref[...])
    kr = _dot(h, wkr_ref[...])
    q = _dot(_rms(cq, qn_ref[...]).astype(BF16), wuq_ref[...])
    ckv_n = _rms(ckv, kvn_ref[...]).astype(BF16)
    kn_out[...] = _dot(ckv_n, wuk_ref[...]).astype(kn_out.dtype)
    vt_out[0] = _dot_nt(wuvt_ref[...], ckv_n).astype(vt_out.dtype)

    ang = pos_ref[...].astype(F32) * invf_ref[...]
    cos, sin = jnp.cos(ang), jnp.sin(ang)
    lane = lax.broadcasted_iota(jnp.int32, ang.shape, 1)
    half = MLA_ROPE // 2
    sin_lo = jnp.where(lane < half, -sin, 0.0)
    sin_hi = jnp.where((lane >= half) & (lane < MLA_ROPE), sin, 0.0)

    def rope(t):
        return (t * cos + pltpu.roll(t, LANES - half, 1) * sin_lo + pltpu.roll(t, half, 1) * sin_hi)

    kr_out[...] = rope(kr).astype(kr_out.dtype)
    scale = (MLA_NOPE + MLA_ROPE) ** -0.5 * LOG2E
    for hh in range(MLA_HEADS):
        base = hh * 2 * LANES
        q_out[:, base:base + LANES] = (q[:, base:base + LANES] * scale).astype(q_out.dtype)
        q_out[:, base + LANES:base + 2 * LANES] = (
            rope(q[:, base + LANES:base + 2 * LANES]) * scale).astype(q_out.dtype)


def _mla_proj(x, g, wq, wkv, wkr, qn, wuq, kvn, wuk, wuvt, pos, invf):
    t, d = x.shape
    tm = min(TOK_TM, t)
    full = lambda a: pl.BlockSpec(a.shape, lambda i: (0, 0))
    nq = MLA_HEADS * 2 * LANES
    nv = MLA_HEADS * MLA_V
    return pl.pallas_call(
        _mla_proj_kernel,
        out_shape=(jax.ShapeDtypeStruct((t, nq), BF16), jax.ShapeDtypeStruct((t, MLA_HEADS * MLA_NOPE), BF16),
                   jax.ShapeDtypeStruct((t, LANES), BF16), jax.ShapeDtypeStruct((t // tm, nv, tm), BF16)),
        grid=(t // tm,),
        in_specs=[pl.BlockSpec((tm, d), lambda i: (i, 0)), full(g), full(wq), full(wkv), full(wkr),
                  full(qn), full(wuq), full(kvn), full(wuk), full(wuvt),
                  pl.BlockSpec((tm, 1), lambda i: (i, 0)), full(invf)],
        out_specs=(pl.BlockSpec((tm, nq), lambda i: (i, 0)),
                   pl.BlockSpec((tm, MLA_HEADS * MLA_NOPE), lambda i: (i, 0)),
                   pl.BlockSpec((tm, LANES), lambda i: (i, 0)),
                   pl.BlockSpec((1, nv, tm), lambda i: (i, 0, 0))),
        compiler_params=_cparams("parallel"),
        name="mla_proj",
    )(x, g, wq, wkv, wkr, qn, wuq, kvn, wuk, wuvt, pos, invf)


def _mla_attn_kernel(q_ref, kn_ref, kr_ref, vt_ref, o_ref, s0, s1, p0, p1, c0, c1, x0, x1, m_s, acc_s,
                     *, tq, tk):
    qi = pl.program_id(2)
    sb, pb, cb, xb = (s0, s1), (p0, p1), (c0, c1), (x0, x1)
    m_s[...] = jnp.full_like(m_s, -jnp.inf)
    acc_s[...] = jnp.zeros_like(acc_s)
    p1[...] = jnp.zeros_like(p1)
    c1[...] = jnp.ones_like(c1)
    query_minus_key = (lax.broadcasted_iota(jnp.int32, (tk, tq), 1)
                       - lax.broadcasted_iota(jnp.int32, (tk, tq), 0))

    def tile_index(t):
        return jnp.maximum(t, 0)

    def stage_q(hd, t, par):
        start = pl.multiple_of(tile_index(t) * tk, tk)
        kc = jnp.concatenate([kn_ref[0, pl.ds(start, tk), hd * LANES:(hd + 1) * LANES],
                              kr_ref[0, pl.ds(start, tk), :]], axis=1)
        q = q_ref[0, :, hd * 2 * LANES:(hd + 1) * 2 * LANES]
        s = _dot_nt(kc, q)
        sb[par][hd] = s
        xb[par][hd] = jnp.max(s, axis=0, keepdims=True)

    def stage_s(hd, par, thr):
        s = sb[par][hd]
        if thr is None:
            tile_max = xb[par][hd]
        else:
            s = jnp.where(query_minus_key >= thr, s, -jnp.inf)
            tile_max = jnp.max(s, axis=0, keepdims=True)
        m_prev = m_s[hd]
        m_new = jnp.maximum(m_prev, tile_max)
        p = jnp.exp2(s - m_new)
        corr = jnp.exp2(m_prev - m_new)
        pb[par][hd] = p.astype(BF16)
        cb[par][hd] = corr
        m_s[hd] = m_new

    ones_rows = jnp.ones((MLA_LROWS, tk), BF16)

    def stage_p(hd, t, par):
        v_t = jnp.concatenate([vt_ref[0, tile_index(t), hd * MLA_V:(hd + 1) * MLA_V, :], ones_rows], axis=0)
        acc_s[hd] = acc_s[hd] * cb[par][hd] + _dot(v_t, pb[par][hd])

    def step(t, par, thr=None, q_=True, s_=True, p_=True):
        for hd in range(MLA_PAIR):
            if q_:
                stage_q(hd, t + 1, 1 - par)
            if s_:
                stage_s(hd, par, thr)
            if p_:
                stage_p(hd, t - 1, 1 - par)

    for hd in range(MLA_PAIR):
        stage_q(hd, 0, 0)

    def body(j, c):
        step(2 * j, 0)
        step(2 * j + 1, 1)
        return c

    lax.fori_loop(0, qi, body, 0)
    d0 = 2 * qi
    step(d0, 0, thr=0)
    step(d0 + 1, 1, thr=tk, q_=False)
    step(d0 + 2, 0, q_=False, s_=False)
    for hd in range(MLA_PAIR):
        acc = acc_s[hd]
        o_ref[0, :, hd * MLA_V:(hd + 1) * MLA_V] = (acc[:MLA_V] / acc[MLA_V:MLA_V + 1]).T.astype(o_ref.dtype)


def _mla_attention(qc, kn, kr, vt):
    b, l, _ = qc.shape
    tq, tk = min(MLA_TQ, l), vt.shape[-1]
    assert tq == 2 * tk and tk % LANES == 0 and MLA_V == LANES and MLA_NOPE == LANES
    pair = MLA_PAIR
    row = pltpu.VMEM((pair, 1, tq), F32)
    once = dict(pipeline_mode=pl.Buffered(1))
    return pl.pallas_call(
        functools.partial(_mla_attn_kernel, tq=tq, tk=tk),
        out_shape=jax.ShapeDtypeStruct((b, l, MLA_HEADS * MLA_V), BF16),
        grid=(b, MLA_HEADS // pair, l // tq),
        in_specs=[
            pl.BlockSpec((1, tq, pair * 2 * LANES), lambda bi, hh, qi: (bi, qi, hh)),
            pl.BlockSpec((1, l, pair * LANES), lambda bi, hh, qi: (bi, 0, hh), **once),
            pl.BlockSpec((1, l, LANES), lambda bi, hh, qi: (bi, 0, 0), **once),
            pl.BlockSpec((1, l // tk, pair * MLA_V, tk), lambda bi, hh, qi: (bi, 0, hh, 0), **once),
        ],
        out_specs=pl.BlockSpec((1, tq, pair * LANES), lambda bi, hh, qi: (bi, qi, hh)),
        scratch_shapes=[pltpu.VMEM((pair, tk, tq), F32), pltpu.VMEM((pair, tk, tq), F32),
                        pltpu.VMEM((pair, tk, tq), BF16), pltpu.VMEM((pair, tk, tq), BF16),
                        row, row, row, row, row, pltpu.VMEM((pair, MLA_V + MLA_LROWS, tq), F32)],
        compiler_params=_cparams("parallel", "parallel", "arbitrary"),
        name="mla_attention",
    )(qc, kn, kr, vt)


GDN_TM = 256
GDN_GROUP = 4
GDN_STEP_CHUNKS = 2


def _gdn_in_kernel(x_ref, g_ref, wqkv_ref, wz_ref, wba_ref, cw_ref, alog_ref, dtb_ref, tri_ref,
                   q_out, k_out, v_out, z_out, bw_out, gw_out, bg_out, pre_s, *, tm, tiles_per_seq):
    i = pl.program_id(0)
    h = _rms(x_ref[...], g_ref[...]).astype(BF16)

    @pl.when(i % tiles_per_seq == 0)
    def _():
        pre_s[...] = jnp.zeros_like(pre_s)

    row8 = lax.broadcasted_iota(jnp.int32, (8, 2 * GDN_DK), 0)
    ba = _dot(h, wba_ref[...])
    lane = lax.broadcasted_iota(jnp.int32, ba.shape, 1)
    beta = _sigmoid(ba)
    gval = -jnp.exp(alog_ref[...]) * _softplus(ba + dtb_ref[...])
    gcum = _split_dot_left(tri_ref[...], gval, 3)
    bg = jnp.where(lane < GDN_HEADS, beta, gcum)
    bg_out[...] = bg
    for hh in range(GDN_HEADS):
        sl = slice(hh * GDN_DK, (hh + 1) * GDN_DK)
        bw_out[:, sl] = jnp.broadcast_to(bg[:, hh:hh + 1], (tm, GDN_DK))
        gw_out[:, sl] = jnp.broadcast_to(bg[:, GDN_HEADS + hh:GDN_HEADS + hh + 1], (tm, GDN_DK))

    cw = cw_ref[...]
    width = 2 * GDN_DK
    for blk in range(GDN_CONV_DIM // width):
        cs = slice(blk * width, (blk + 1) * width)
        pre = _dot(h, wqkv_ref[:, cs])
        tail = pre_s[:, cs]
        conv = pre * cw[3:4, cs]
        for j in range(1, GDN_CONV):
            rolled = pltpu.roll(pre, j, 0)
            first = jnp.where(row8 < j, pltpu.roll(tail, j, 0), rolled[0:8])
            conv = conv + jnp.concatenate([first, rolled[8:]], axis=0) * cw[3 - j:4 - j, cs]
        pre_s[:, cs] = pre[tm - 8:tm]
        y = conv * _sigmoid(conv)
        if blk * width >= 2 * GDN_QK:
            v_out[:, blk * width - 2 * GDN_QK:(blk + 1) * width - 2 * GDN_QK] = y
            continue
        is_q = blk * width < GDN_QK
        out = q_out if is_q else k_out
        for half in range(2):
            yh = y[:, half * GDN_DK:(half + 1) * GDN_DK]
            yn = yh * lax.rsqrt(jnp.sum(yh * yh, axis=1, keepdims=True) + EPS)
            col = (blk * width) % GDN_QK + half * GDN_DK
            out[:, col:col + GDN_DK] = yn * GDN_DK ** -0.5 if is_q else yn

    for blk in range(GDN_V // width):
        cs = slice(blk * width, (blk + 1) * width)
        z = _dot(h, wz_ref[:, cs])
        z_out[:, cs] = z * _sigmoid(z)


def _split_dot_left(m_bf16, x, terms):
    out = None
    r = x
    for i in range(terms):
        p = r.astype(BF16)
        d = _dot(m_bf16, p)
        out = d if out is None else out + d
        if i + 1 < terms:
            r = r - p.astype(F32)
    return out


def _gdn_in(x, g, wqkv, wz, wba, cw, alog, dtb, tri, seq):
    t, d = x.shape
    tm = min(GDN_TM, seq)
    full = lambda a: pl.BlockSpec(a.shape, lambda i: (0, 0))
    row = lambda n: pl.BlockSpec((tm, n), lambda i: (i, 0))
    wide = jax.ShapeDtypeStruct((t, GDN_QK), F32)
    return pl.pallas_call(
        functools.partial(_gdn_in_kernel, tm=tm, tiles_per_seq=seq // tm),
        out_shape=(wide, wide, wide, wide, wide, wide, jax.ShapeDtypeStruct((t, LANES), F32)),
        grid=(t // tm,),
        in_specs=[row(d), full(g), full(wqkv), full(wz), full(wba), full(cw), full(alog), full(dtb), full(tri)],
        out_specs=(row(GDN_QK),) * 6 + (row(LANES),),
        scratch_shapes=[pltpu.VMEM((8, GDN_CONV_DIM), F32)],
        compiler_params=_cparams("arbitrary"),
        name="gdn_in",
    )(x, g, wqkv, wz, wba, cw, alog, dtb, tri)


def _gdn_chunk_kernel(q_ref, k_ref, v_ref, bw_ref, gw_ref, gr_ref, o_ref, s_ref, *, nchunk):
    c, grp = GDN_CHUNK, GDN_GROUP
    rows = grp * c

    @pl.when(pl.program_id(1) == 0)
    def _():
        s_ref[...] = jnp.zeros_like(s_ref)

    r = lax.broadcasted_iota(jnp.int32, (rows, rows), 0)
    cc = lax.broadcasted_iota(jnp.int32, (rows, rows), 1)
    same = lax.shift_right_logical(r, 6) == lax.shift_right_logical(cc, 6)
    incl = same & (r >= cc)
    strict = same & (r > cc)
    own_block = (lax.shift_right_logical(lax.broadcasted_iota(jnp.int32, (rows, grp * GDN_DK), 0), 6)
                 == lax.shift_right_logical(lax.broadcasted_iota(jnp.int32, (rows, grp * GDN_DK), 1), 7))

    def spread(t):
        t16 = t.astype(BF16)
        return jnp.where(own_block, jnp.concatenate([t16] * grp, axis=1), jnp.zeros((), BF16))

    ngrp = GDN_HEADS // grp
    units = [(ci, g) for ci in range(nchunk) for g in range(ngrp)]

    def stack(ref, ci, g):
        return jnp.concatenate(
            [ref[0, ci * c:(ci + 1) * c, (g * grp + j) * GDN_DK:(g * grp + j + 1) * GDN_DK] for j in range(grp)],
            axis=0)

    ks = [stack(k_ref, ci, g) for ci, g in units]
    gw = [stack(gw_ref, ci, g) for ci, g in units]
    bw = [stack(bw_ref, ci, g) for ci, g in units]
    k16 = [t.astype(BF16) for t in ks]
    kb = [ks[i] * bw[i] for i in range(len(units))]
    eg = [jnp.exp(t) for t in gw]
    decay = []
    for i, (ci, g) in enumerate(units):
        gdiff = jnp.concatenate([gw[i], gw[i]], axis=1) - gr_ref[0, ci, g:g + 1, :]
        decay.append(jnp.where(incl, jnp.exp(jnp.where(incl, gdiff, 0.0)), 0.0))
    p = [jnp.where(strict, _dot_nt(kb[i].astype(BF16), k16[i]) * decay[i], 0.0).astype(BF16)
         for i in range(len(units))]
    y = [jnp.concatenate([stack(v_ref, ci, g) * bw[i], kb[i] * eg[i]], axis=1)
         for i, (ci, g) in enumerate(units)]
    y = [y[i] - _dot(p[i], y[i].astype(BF16)) for i in range(len(units))]
    step = 2
    while step < c:
        p = [_dot(t, t).astype(BF16) for t in p]
        y = [y[i] + _dot(p[i], y[i].astype(BF16)) for i in range(len(units))]
        step *= 2
    qs = [stack(q_ref, ci, g) for ci, g in units]
    attn = [(_dot_nt(qs[i].astype(BF16), k16[i]) * decay[i]).astype(BF16) for i in range(len(units))]

    for ci in range(nchunk):
        idx = [ci * ngrp + g for g in range(ngrp)]
        s = [s_ref[g] for g in range(ngrp)]
        ws, qs_s = [], []
        for g, i in enumerate(idx):
            qd = qs[i] * eg[i]
            prod = [_dot(jnp.concatenate([y[i][j * c:(j + 1) * c, GDN_DV:], qd[j * c:(j + 1) * c]],
                                         axis=0).astype(BF16),
                         s[g][j * GDN_DK:(j + 1) * GDN_DK].astype(BF16)) for j in range(grp)]
            ws.append(jnp.concatenate([t[:c] for t in prod], axis=0))
            qs_s.append(jnp.concatenate([t[c:] for t in prod], axis=0))
        v16 = [(y[i][:, :GDN_DV] - ws[g]).astype(BF16) for g, i in enumerate(idx)]
        o = [qs_s[g] + _dot(attn[i], v16[g]) for g, i in enumerate(idx)]
        for g, i in enumerate(idx):
            last = [gw[i][j * c + c - 1:j * c + c, :] for j in range(grp)]
            g_last = jnp.concatenate([jnp.broadcast_to(t, (c, GDN_DK)) for t in last], axis=0)
            s_decay = jnp.concatenate([jnp.broadcast_to(jnp.exp(t), (GDN_DK, GDN_DV)) for t in last], axis=0)
            kd = ks[i] * jnp.exp(g_last - gw[i])
            s_ref[g] = s[g] * s_decay + _dot_tn(spread(kd), v16[g])
            for j in range(grp):
                o_ref[0, ci * c:(ci + 1) * c, (g * grp + j) * GDN_DV:(g * grp + j + 1) * GDN_DV] = (
                    o[g][j * c:(j + 1) * c])


def _gdn_chunks(q, k, v, bw, gw, grow):
    b, l, n = q.shape
    nchunk = GDN_STEP_CHUNKS
    tl = nchunk * GDN_CHUNK
    ngrp = GDN_HEADS // GDN_GROUP
    blk = pl.BlockSpec((1, tl, n), lambda bi, ci: (bi, ci, 0))
    return pl.pallas_call(
        functools.partial(_gdn_chunk_kernel, nchunk=nchunk),
        out_shape=jax.ShapeDtypeStruct((b, l, n), F32),
        grid=(b, l // tl),
        in_specs=[blk, blk, blk, blk, blk,
                  pl.BlockSpec((1, nchunk, ngrp, GDN_GROUP * GDN_CHUNK), lambda bi, ci: (bi, ci, 0, 0))],
        out_specs=blk,
        scratch_shapes=[pltpu.VMEM((ngrp, GDN_GROUP * GDN_DK, GDN_DV), F32)],
        compiler_params=_cparams("parallel", "arbitrary"),
        name="gdn_chunks",
    )(q, k, v, bw, gw, grow)


def _gdn_out_kernel(o_ref, z_ref, gn_ref, w_ref, x_ref, out_ref, a_s):
    gn = gn_ref[...]
    for hh in range(GDN_HEADS):
        sl = slice(hh * GDN_DV, (hh + 1) * GDN_DV)
        a_s[:, sl] = (_rms(o_ref[:, sl], gn) * z_ref[:, sl]).astype(BF16)
    out_ref[...] = x_ref[...] + _dot(a_s[...], w_ref[...])


def _gdn_out(o, zs, gn, w, x):
    t, d = x.shape
    tm = min(FFN_TM, t)
    row = lambda n: pl.BlockSpec((tm, n), lambda i: (i, 0))
    full = lambda a: pl.BlockSpec(a.shape, lambda i: (0, 0))
    return pl.pallas_call(
        _gdn_out_kernel,
        out_shape=jax.ShapeDtypeStruct((t, d), F32),
        grid=(t // tm,),
        in_specs=[row(GDN_V), row(GDN_V), full(gn), full(w), row(d)],
        out_specs=row(d),
        scratch_shapes=[pltpu.VMEM((tm, GDN_V), BF16)],
        compiler_params=_cparams("parallel"),
        name="gdn_out",
    )(o, zs, gn, w, x)


def _chunk_tri(tm):
    t = np.arange(tm)
    same = (t[:, None] // GDN_CHUNK) == (t[None, :] // GDN_CHUNK)
    return jnp.asarray((same & (t[:, None] >= t[None, :])).astype(np.float32), BF16)


def _row(v):
    return v.reshape(1, -1).astype(F32)


def _pad_lanes(v, offset):
    out = jnp.zeros((1, LANES), F32)
    return out.at[0, offset:offset + v.shape[0]].set(v.astype(F32))


def _gated_deltanet(x, norm_g, w_in, conv_w, a_log, dt_bias, out_norm, w_out, batch, seq):
    t = x.shape[0]
    wqkv = w_in[:, :GDN_CONV_DIM].astype(BF16)
    wz = w_in[:, GDN_CONV_DIM:GDN_CONV_DIM + GDN_V].astype(BF16)
    wba = jnp.pad(w_in[:, GDN_CONV_DIM + GDN_V:], ((0, 0), (0, LANES - 2 * GDN_HEADS))).astype(BF16)
    q, k, v, zs, bw, gw, bg = _gdn_in(x, _row(norm_g), wqkv, wz, wba, conv_w.astype(F32),
                                      _pad_lanes(a_log, GDN_HEADS), _pad_lanes(dt_bias, GDN_HEADS),
                                      _chunk_tri(min(GDN_TM, seq)), seq)
    nc = seq // GDN_CHUNK
    grow = bg[:, GDN_HEADS:2 * GDN_HEADS].reshape(batch, nc, GDN_CHUNK, GDN_HEADS).transpose(0, 1, 3, 2)
    grow = grow.reshape(batch, nc, GDN_HEADS // GDN_GROUP, GDN_GROUP * GDN_CHUNK)
    shp = (batch, seq, GDN_QK)
    o = _gdn_chunks(q.reshape(shp), k.reshape(shp), v.reshape(shp), bw.reshape(shp), gw.reshape(shp), grow)
    return _gdn_out(o.reshape(t, GDN_V), zs, _row(out_norm), w_out.astype(BF16), x)


def _stick_breaking(x, norm_g, w_qkv, w_out, batch, seq):
    t = x.shape[0]
    n = SB_HEADS * SB_HD
    w = jnp.concatenate([w_qkv[:, :n] * (SB_HD ** -0.5 * LOG2E), w_qkv[:, n:]], axis=1).astype(BF16)
    qkv = _norm_matmul(x, _row(norm_g), w, 1024, BF16)
    o = _sb_attention(qkv.reshape(batch, seq, 3 * n), _sb_suffix_matrix(SB_TK))
    return _matmul_res(o.reshape(t, n), w_out.astype(BF16), x)


def _mla(x, norm_g, positions, w_in, q_norm, w_uq, kv_norm, w_ukv, w_out, batch, seq):
    t = x.shape[0]
    wq = w_in[:, :MLA_Q_RANK].astype(BF16)
    wkv = w_in[:, MLA_Q_RANK:MLA_Q_RANK + MLA_KV_RANK].astype(BF16)
    wkr = jnp.pad(w_in[:, MLA_Q_RANK + MLA_KV_RANK:], ((0, 0), (0, LANES - MLA_ROPE))).astype(BF16)
    qd = MLA_NOPE + MLA_ROPE
    wuq = jnp.pad(w_uq.reshape(MLA_Q_RANK, MLA_HEADS, qd), ((0, 0), (0, 0), (0, 2 * LANES - qd)))
    wuq = wuq.reshape(MLA_Q_RANK, MLA_HEADS * 2 * LANES).astype(BF16)
    half = MLA_ROPE // 2
    inv_freq = ROPE_THETA ** (-jnp.arange(half, dtype=F32) / half)
    invf = jnp.zeros((1, LANES), F32).at[0, :MLA_ROPE].set(jnp.concatenate([inv_freq, inv_freq]))
    wukv = w_ukv.reshape(MLA_KV_RANK, MLA_HEADS, MLA_NOPE + MLA_V)
    wuk = wukv[:, :, :MLA_NOPE].reshape(MLA_KV_RANK, MLA_HEADS * MLA_NOPE).astype(BF16)
    wuvt = wukv[:, :, MLA_NOPE:].reshape(MLA_KV_RANK, MLA_HEADS * MLA_V).T.astype(BF16)
    qc, kn, kr, vt = _mla_proj(x, _row(norm_g), wq, wkv, wkr, _row(q_norm), wuq, _row(kv_norm),
                               wuk, wuvt, positions.reshape(t, 1), invf)
    tk = vt.shape[-1]
    o = _mla_attention(qc.reshape(batch, seq, MLA_HEADS * 2 * LANES), kn.reshape(batch, seq, MLA_HEADS * MLA_NOPE),
                       kr.reshape(batch, seq, LANES), vt.reshape(batch, seq // tk, MLA_HEADS * MLA_V, tk))
    return _matmul_res(o.reshape(t, MLA_HEADS * MLA_V), w_out.astype(BF16), x)


def kernel(x, p, positions, ffn1_norm, ffn1_w_gate, ffn1_w_up, ffn1_w_down, mix_norm, ffn2_norm, ffn2_w_gate, ffn2_w_up, ffn2_w_down, ple_norm, ple_w_gate, ple_w_proj, gdn_w_in, gdn_conv_w, gdn_a_log, gdn_dt_bias, gdn_out_norm, gdn_w_out, sb_w_qkv, sb_w_out, mla_w_in, mla_q_norm, mla_w_uq, mla_kv_norm, mla_w_ukv, mla_w_out, final_norm):
    batch, seq, d = x.shape
    t = batch * seq
    depth = ffn1_norm.shape[0]
    x = x.reshape(t, d)
    p = p.reshape(depth, t, PLE_DIM)
    for i in range(depth):
        mixer, slot = i % N_MIXERS, i // N_MIXERS
        x = _ffn(x, _row(ffn1_norm[i]), ffn1_w_gate[i].astype(BF16), ffn1_w_up[i].astype(BF16),
                 ffn1_w_down[i].astype(BF16))
        if mixer == 0:
            x = _gated_deltanet(x, mix_norm[i], gdn_w_in[slot], gdn_conv_w[slot], gdn_a_log[slot],
                                gdn_dt_bias[slot], gdn_out_norm[slot], gdn_w_out[slot], batch, seq)
        elif mixer == 1:
            x = _stick_breaking(x, mix_norm[i], sb_w_qkv[slot], sb_w_out[slot], batch, seq)
        else:
            x = _mla(x, mix_norm[i], positions, mla_w_in[slot], mla_q_norm[slot], mla_w_uq[slot],
                     mla_kv_norm[slot], mla_w_ukv[slot], mla_w_out[slot], batch, seq)
        x = _ffn(x, _row(ffn2_norm[i]), ffn2_w_gate[i].astype(BF16), ffn2_w_up[i].astype(BF16),
                 ffn2_w_down[i].astype(BF16))
        x = _ple(x, _row(ple_norm[i]), ple_w_gate[i].astype(BF16), p, i, ple_w_proj[i].astype(BF16),
                 _row(final_norm), final=(i == depth - 1))
    return x.reshape(batch, seq, d)
```

```python
import functools

import numpy as np
import jax
import jax.numpy as jnp
from jax import lax
from jax.experimental import pallas as pl
from jax.experimental.pallas import tpu as pltpu

F32 = jnp.float32
BF16 = jnp.bfloat16

D_MODEL = 1024
DEPTH = 4
N_MIXERS = 3
PLE_DIM = 256
D_FF = 2816
EPS = 1e-6

GDN_HEADS = 8
GDN_DK = 128
GDN_DV = 128
GDN_CONV = 4
GDN_CHUNK = 64
GDN_QK = GDN_HEADS * GDN_DK
GDN_V = GDN_HEADS * GDN_DV
GDN_CONV_DIM = 2 * GDN_QK + GDN_V

SB_HEADS = 16
SB_HD = 64

MLA_HEADS = 8
MLA_Q_RANK = 384
MLA_KV_RANK = 256
MLA_NOPE = 128
MLA_ROPE = 64
MLA_V = 128
ROPE_THETA = 10000.0

LANES = 128
VMEM_LIMIT = 56 * 1024 * 1024

FFN_TM = 1024
FFN_CHUNK = 512
TOK_TM = 512
SB_TQ = 512
SB_TK = 128
MLA_TQ = 2 * TOK_TM
MLA_PAIR = 2
MLA_LROWS = 16

LOG2E = 1.4426950408889634
NEG_BIG = -1e30
SB_SPLIT_TERMS = 1
SB_DEAD_BITS = 150.0


def _cparams(*sem):
    return pltpu.CompilerParams(dimension_semantics=sem, vmem_limit_bytes=VMEM_LIMIT)


def _dot(a, b):
    return jnp.dot(a, b, preferred_element_type=F32)


def _dot_nt(a, b):
    return lax.dot_general(a, b, (((1,), (1,)), ((), ())), preferred_element_type=F32)


def _dot_tn(a, b):
    return lax.dot_general(a, b, (((0,), (0,)), ((), ())), preferred_element_type=F32)


def _rms(xf, g):
    return xf * lax.rsqrt(jnp.mean(xf * xf, axis=-1, keepdims=True) + EPS) * g


def _sigmoid(x):
    return 1.0 / (1.0 + jnp.exp2(x * -LOG2E))


def _softplus(x):
    return jnp.maximum(x, 0.0) + jnp.log(1.0 + jnp.exp(-jnp.abs(x)))


def _split_dot(x, m_bf16, terms):
    out = None
    r = x
    for i in range(terms):
        p = r.astype(BF16)
        d = _dot(p, m_bf16)
        out = d if out is None else out + d
        if i + 1 < terms:
            r = r - p.astype(F32)
    return out


def _ffn_kernel(x_ref, g_ref, wg_ref, wu_ref, wd_ref, o_ref):
    x = x_ref[...]
    h = _rms(x, g_ref[...]).astype(BF16)
    acc = None
    for lo in range(0, D_FF, FFN_CHUNK):
        hi = min(lo + FFN_CHUNK, D_FF)
        gate = _dot(h, wg_ref[:, lo:hi])
        up = _dot(h, wu_ref[:, lo:hi])
        act = (gate * _sigmoid(gate) * up).astype(BF16)
        part = _dot(act, wd_ref[lo:hi, :])
        acc = part if acc is None else acc + part
    o_ref[...] = x + 0.5 * acc


def _ffn(x, g, wg, wu, wd):
    t, d = x.shape
    tm = min(FFN_TM, t)
    once = lambda a: pl.BlockSpec(a.shape, lambda i: (0, 0), pipeline_mode=pl.Buffered(1))
    return pl.pallas_call(
        _ffn_kernel,
        out_shape=jax.ShapeDtypeStruct((t, d), F32),
        grid=(t // tm,),
        in_specs=[pl.BlockSpec((tm, d), lambda i: (i, 0)), once(g), once(wg), once(wu), once(wd)],
        out_specs=pl.BlockSpec((tm, d), lambda i: (i, 0)),
        compiler_params=_cparams("parallel"),
        name="ffn",
    )(x, g, wg, wu, wd)


def _ple_kernel(x_ref, g_ref, wgate_ref, p_ref, wp_ref, fin_ref, o_ref, *, final):
    x = x_ref[...]
    h = _rms(x, g_ref[...]).astype(BF16)
    gate = _sigmoid(_dot(h, wgate_ref[...]))
    y = x + gate * _dot(p_ref[0].astype(BF16), wp_ref[...])
    if final:
        y = _rms(y, fin_ref[...])
    o_ref[...] = y


def _ple(x, g, wgate, p, layer, wp, fin, final):
    t, d = x.shape
    tm = min(FFN_TM, t)
    return pl.pallas_call(
        functools.partial(_ple_kernel, final=final),
        out_shape=jax.ShapeDtypeStruct((t, d), F32),
        grid=(t // tm,),
        in_specs=[
            pl.BlockSpec((tm, d), lambda i: (i, 0)),
            pl.BlockSpec((1, d), lambda i: (0, 0)),
            pl.BlockSpec((d, d), lambda i: (0, 0)),
            pl.BlockSpec((1, tm, PLE_DIM), lambda i: (layer, i, 0)),
            pl.BlockSpec((PLE_DIM, d), lambda i: (0, 0)),
            pl.BlockSpec((1, d), lambda i: (0, 0)),
        ],
        out_specs=pl.BlockSpec((tm, d), lambda i: (i, 0)),
        compiler_params=_cparams("parallel"),
        name="ple",
    )(x, g, wgate, p, wp, fin)


def _norm_matmul_kernel(x_ref, g_ref, w_ref, o_ref, *, tn):
    h = _rms(x_ref[...], g_ref[...]).astype(BF16)
    for lo in range(0, w_ref.shape[1], tn):
        o_ref[:, lo:lo + tn] = _dot(h, w_ref[:, lo:lo + tn]).astype(o_ref.dtype)


def _norm_matmul(x, g, w, tn, out_dtype):
    t, d = x.shape
    n = w.shape[1]
    tm = min(FFN_TM, t)
    once = lambda a: pl.BlockSpec(a.shape, lambda i: (0, 0), pipeline_mode=pl.Buffered(1))
    return pl.pallas_call(
        functools.partial(_norm_matmul_kernel, tn=tn),
        out_shape=jax.ShapeDtypeStruct((t, n), out_dtype),
        grid=(t // tm,),
        in_specs=[pl.BlockSpec((tm, d), lambda i: (i, 0)), once(g), once(w)],
        out_specs=pl.BlockSpec((tm, n), lambda i: (i, 0)),
        compiler_params=_cparams("parallel"),
        name="norm_matmul",
    )(x, g, w)


def _matmul_res_kernel(a_ref, w_ref, x_ref, o_ref):
    o_ref[...] = x_ref[...] + _dot(a_ref[...].astype(BF16), w_ref[...])


def _matmul_res(a, w, x):
    t, k = a.shape
    d = w.shape[1]
    tm = min(FFN_TM, t)
    return pl.pallas_call(
        _matmul_res_kernel,
        out_shape=jax.ShapeDtypeStruct((t, d), F32),
        grid=(t // tm,),
        in_specs=[
            pl.BlockSpec((tm, k), lambda i: (i, 0)),
            pl.BlockSpec((k, d), lambda i: (0, 0)),
            pl.BlockSpec((tm, d), lambda i: (i, 0)),
        ],
        out_specs=pl.BlockSpec((tm, d), lambda i: (i, 0)),
        compiler_params=_cparams("parallel"),
        name="matmul_res",
    )(a, w, x)


def _sb_kernel(q_ref, k_ref, v_ref, u_ref, o_ref,
               z0, z1, lb0, lb1, sf0, sf1, a0, a1, run_a, run_b, acc_o, *, tq, tk):
    qi = pl.program_id(2)
    ratio = tq // tk
    n_tiles = ratio * (qi + 1)
    q = q_ref[0]
    u2 = u_ref[...]
    zb, lbb, sfb, ab = (z0, z1), (lb0, lb1), (sf0, sf1), (a0, a1)
    first = lax.broadcasted_iota(jnp.int32, (tk, LANES), 1) < SB_HD
    row_minus_col = (lax.broadcasted_iota(jnp.int32, (tq, 2 * tk), 0)
                     - (lax.broadcasted_iota(jnp.int32, (tq, 2 * tk), 1) & (tk - 1)))
    acc_o[...] = jnp.zeros_like(acc_o)
    run_a[...] = jnp.zeros_like(run_a)
    run_b[...] = jnp.zeros_like(run_b)

    def blockdiag(t):
        zero = jnp.zeros_like(t)
        return jnp.concatenate([jnp.where(first, t, zero), jnp.where(first, zero, t)], axis=0)

    def key_start(t):
        kb = jnp.maximum(ratio * qi + ratio - 1 - t, 0)
        return pl.multiple_of(kb * tk, tk)

    def stage_q(t, par, r0=0):
        kk = blockdiag(k_ref[0, pl.ds(key_start(t), tk), :])
        zb[par][r0:, :] = _dot_nt(q[r0:], kk)

    def stage_s1(par, thr, r0=0):
        z = zb[par][r0:, :]
        neg_abs = lax.bitcast_convert_type(
            lax.bitcast_convert_type(z, jnp.uint32) | jnp.uint32(0x80000000), F32)
        sp = jnp.maximum(z, 0.0) + jnp.log(1.0 + jnp.exp2(neg_abs)) * LOG2E
        ra, rb = run_a[r0:, :], run_b[r0:, :]
        lb = z - sp - jnp.concatenate([ra, rb], axis=1)
        if thr is not None:
            valid = row_minus_col[r0:] > thr
            sp = jnp.where(valid, sp, 0.0)
            lb = jnp.where(valid, lb, NEG_BIG)
        lbb[par][r0:, :] = lb
        sfb[par][r0:, :] = _split_dot(sp, u2, SB_SPLIT_TERMS)
        if r0:
            lbb[par][:r0, :] = jnp.full((r0, 2 * tk), NEG_BIG, F32)
            sfb[par][:r0, :] = jnp.zeros((r0, 2 * tk), F32)
        run_a[r0:, :] = ra + jnp.sum(sp[:, :tk], axis=1, keepdims=True)
        run_b[r0:, :] = rb + jnp.sum(sp[:, tk:], axis=1, keepdims=True)

    def stage_s2(par):
        ab[par][...] = jnp.exp2(lbb[par][...] - sfb[par][...]).astype(BF16)

    def stage_p(t, par):
        vv = blockdiag(v_ref[0, pl.ds(key_start(t), tk), :])
        acc_o[...] += _dot(ab[par][...], vv)

    def step(u, par, thr=None, q_=True, s1=True, s2=True, p_=True, q_r0=0):
        if q_:
            stage_q(u, par, q_r0)
        if s1:
            stage_s1(1 - par, thr, 0 if thr is None else thr)
        if s2:
            stage_s2(par)
        if p_:
            stage_p(u - 3, 1 - par)

    for u in range(ratio + 1):
        step(u, u % 2, thr=(ratio - u) * tk, s1=u >= 1, s2=u >= 2, p_=u >= 3,
             q_r0=max(ratio - 1 - u, 0) * tk)

    n_pairs = (n_tiles - ratio) // 2

    def body(c):
        j, _ = c
        u = ratio + 1 + 2 * j
        step(u, (ratio + 1) % 2)
        step(u + 1, ratio % 2)
        return j + 1, jnp.min(jnp.minimum(run_a[...], run_b[...]))

    def live(c):
        j, lowest = c
        return (j < n_pairs) & (lowest < SB_DEAD_BITS)

    j_end, _ = lax.while_loop(live, body, (jnp.int32(0), jnp.float32(0.0)))
    n_done = ratio + 2 * j_end
    step(n_done + 1, (ratio + 1) % 2, q_=False, s1=False)
    step(n_done + 2, ratio % 2, q_=False, s1=False, s2=False)
    o_ref[0] = acc_o[...].astype(o_ref.dtype)


def _sb_attention(qkv, u2):
    b, l, _ = qkv.shape
    tq, tk = min(SB_TQ, l), SB_TK
    assert (tq // tk) % 2 == 0 and tk == LANES
    npair = SB_HEADS * SB_HD // LANES
    wide = pltpu.VMEM((tq, 2 * tk), F32)
    col = pltpu.VMEM((tq, LANES), F32)
    return pl.pallas_call(
        functools.partial(_sb_kernel, tq=tq, tk=tk),
        out_shape=jax.ShapeDtypeStruct((b, l, SB_HEADS * SB_HD), BF16),
        grid=(b, npair, l // tq),
        in_specs=[
            pl.BlockSpec((1, tq, LANES), lambda bi, j, qi: (bi, qi, j)),
            pl.BlockSpec((1, l, LANES), lambda bi, j, qi: (bi, 0, npair + j)),
            pl.BlockSpec((1, l, LANES), lambda bi, j, qi: (bi, 0, 2 * npair + j)),
            pl.BlockSpec((2 * tk, 2 * tk), lambda bi, j, qi: (0, 0)),
        ],
        out_specs=pl.BlockSpec((1, tq, LANES), lambda bi, j, qi: (bi, qi, j)),
        scratch_shapes=[wide, wide, wide, wide, wide, wide,
                        pltpu.VMEM((tq, 2 * tk), BF16), pltpu.VMEM((tq, 2 * tk), BF16),
                        col, col, pltpu.VMEM((tq, LANES), F32)],
        compiler_params=_cparams("parallel", "parallel", "arbitrary"),
        name="sb_attention",
    )(qkv, qkv, qkv, u2)


def _sb_suffix_matrix(tk):
    s = np.arange(2 * tk)
    same = (s[:, None] // tk) == (s[None, :] // tk)
    return jnp.asarray((same & (s[:, None] > s[None, :])).astype(np.float32), BF16)


def _mla_proj_kernel(x_ref, g_ref, wq_ref, wkv_ref, wkr_ref, qn_ref, wuq_ref, kvn_ref, wuk_ref, wuvt_ref,
                     pos_ref, invf_ref, q_out, kn_out, kr_out, vt_out):
    h = _rms(x_ref[...], g_ref[...]).astype(BF16)
    cq = _dot(h, wq_ref[...])
    ckv = _dot(h, wkv_ref[...])
    kr = _dot(h, wkr_ref[...])
    q = _dot(_rms(cq, qn_ref[...]).astype(BF16), wuq_ref[...])
    ckv_n = _rms(ckv, kvn_ref[...]).astype(BF16)
    kn_out[...] = _dot(ckv_n, wuk_ref[...]).astype(kn_out.dtype)
    vt_out[0] = _dot_nt(wuvt_ref[...], ckv_n).astype(vt_out.dtype)

    pos = pos_ref[...].astype(F32)
    hm = pos.shape[0] // 2
    ang = pos[:hm] * invf_ref[0:1, :] + pos[hm:] * invf_ref[1:2, :]
    cos_p, sin_p = jnp.cos(ang), jnp.sin(ang)
    cos = jnp.concatenate([cos_p, pltpu.roll(cos_p, LANES // 2, 1)], axis=0)
    sin = jnp.concatenate([sin_p, pltpu.roll(sin_p, LANES // 2, 1)], axis=0)
    lane = lax.broadcasted_iota(jnp.int32, cos.shape, 1)
    half = MLA_ROPE // 2
    sin_lo = jnp.where(lane < half, -sin, 0.0)
    sin_hi = jnp.where((lane >= half) & (lane < MLA_ROPE), sin, 0.0)

    def rope(t):
        return (t * cos + pltpu.roll(t, LANES - half, 1) * sin_lo + pltpu.roll(t, half, 1) * sin_hi)

    kr_out[...] = rope(kr).astype(kr_out.dtype)
    scale = (MLA_NOPE + MLA_ROPE) ** -0.5 * LOG2E
    for hh in range(MLA_HEADS):
        base = hh * 2 * LANES
        q_out[:, base:base + LANES] = (q[:, base:base + LANES] * scale).astype(q_out.dtype)
        q_out[:, base + LANES:base + 2 * LANES] = (
            rope(q[:, base + LANES:base + 2 * LANES]) * scale).astype(q_out.dtype)


def _mla_proj(x, g, wq, wkv, wkr, qn, wuq, kvn, wuk, wuvt, pos, invf):
    t, d = x.shape
    tm = min(TOK_TM, t)
    full = lambda a: pl.BlockSpec(a.shape, lambda i: (0, 0))
    nq = MLA_HEADS * 2 * LANES
    nv = MLA_HEADS * MLA_V
    return pl.pallas_call(
        _mla_proj_kernel,
        out_shape=(jax.ShapeDtypeStruct((t, nq), BF16), jax.ShapeDtypeStruct((t, MLA_HEADS * MLA_NOPE), BF16),
                   jax.ShapeDtypeStruct((t, LANES), BF16), jax.ShapeDtypeStruct((t // tm, nv, tm), BF16)),
        grid=(t // tm,),
        in_specs=[pl.BlockSpec((tm, d), lambda i: (i, 0)), full(g), full(wq), full(wkv), full(wkr),
                  full(qn), full(wuq), full(kvn), full(wuk), full(wuvt),
                  pl.BlockSpec((tm, 1), lambda i: (i, 0)), full(invf)],
        out_specs=(pl.BlockSpec((tm, nq), lambda i: (i, 0)),
                   pl.BlockSpec((tm, MLA_HEADS * MLA_NOPE), lambda i: (i, 0)),
                   pl.BlockSpec((tm, LANES), lambda i: (i, 0)),
                   pl.BlockSpec((1, nv, tm), lambda i: (i, 0, 0))),
        compiler_params=_cparams("parallel"),
        name="mla_proj",
    )(x, g, wq, wkv, wkr, qn, wuq, kvn, wuk, wuvt, pos, invf)


def _mla_attn_kernel(q_ref, kn_ref, kr_ref, vt_ref, o_ref, s0, s1, p0, p1, c0, c1, x0, x1, m_s, acc_s,
                     *, tq, tk):
    qi = pl.program_id(2)
    sb, pb, cb, xb = (s0, s1), (p0, p1), (c0, c1), (x0, x1)
    m_s[...] = jnp.full_like(m_s, -jnp.inf)
    acc_s[...] = jnp.zeros_like(acc_s)
    p1[...] = jnp.zeros_like(p1)
    c1[...] = jnp.ones_like(c1)
    query_minus_key = (lax.broadcasted_iota(jnp.int32, (tk, tq), 1)
                       - lax.broadcasted_iota(jnp.int32, (tk, tq), 0))

    def tile_index(t):
        return jnp.maximum(t, 0)

    def stage_q(hd, t, par):
        start = pl.multiple_of(tile_index(t) * tk, tk)
        kc = jnp.concatenate([kn_ref[0, pl.ds(start, tk), hd * LANES:(hd + 1) * LANES],
                              kr_ref[0, pl.ds(start, tk), :]], axis=1)
        q = q_ref[0, :, hd * 2 * LANES:(hd + 1) * 2 * LANES]
        s = _dot_nt(kc, q)
        sb[par][hd] = s
        xb[par][hd] = jnp.max(s, axis=0, keepdims=True)

    def stage_s(hd, par, thr):
        s = sb[par][hd]
        if thr is None:
            tile_max = xb[par][hd]
        else:
            s = jnp.where(query_minus_key >= thr, s, -jnp.inf)
            tile_max = jnp.max(s, axis=0, keepdims=True)
        m_prev = m_s[hd]
        m_new = jnp.maximum(m_prev, tile_max)
        p = jnp.exp2(s - m_new)
        corr = jnp.exp2(m_prev - m_new)
        pb[par][hd] = p.astype(BF16)
        cb[par][hd] = corr
        m_s[hd] = m_new

    ones_rows = jnp.ones((MLA_LROWS, tk), BF16)

    def stage_p(hd, t, par):
        v_t = jnp.concatenate([vt_ref[0, tile_index(t), hd * MLA_V:(hd + 1) * MLA_V, :], ones_rows], axis=0)
        acc_s[hd] = acc_s[hd] * cb[par][hd] + _dot(v_t, pb[par][hd])

    def step(t, par, thr=None, q_=True, s_=True, p_=True):
        for hd in range(MLA_PAIR):
            if q_:
                stage_q(hd, t + 1, 1 - par)
            if s_:
                stage_s(hd, par, thr)
            if p_:
                stage_p(hd, t - 1, 1 - par)

    for hd in range(MLA_PAIR):
        stage_q(hd, 0, 0)

    def body(j, c):
        step(2 * j, 0)
        step(2 * j + 1, 1)
        return c

    lax.fori_loop(0, qi, body, 0)
    d0 = 2 * qi
    step(d0, 0, thr=0)
    step(d0 + 1, 1, thr=tk, q_=False)
    step(d0 + 2, 0, q_=False, s_=False)
    for hd in range(MLA_PAIR):
        acc = acc_s[hd]
        o_ref[0, :, hd * MLA_V:(hd + 1) * MLA_V] = (acc[:MLA_V] / acc[MLA_V:MLA_V + 1]).T.astype(o_ref.dtype)


def _mla_attention(qc, kn, kr, vt):
    b, l, _ = qc.shape
    tq, tk = min(MLA_TQ, l), vt.shape[-1]
    assert tq == 2 * tk and tk % LANES == 0 and MLA_V == LANES and MLA_NOPE == LANES
    pair = MLA_PAIR
    row = pltpu.VMEM((pair, 1, tq), F32)
    once = dict(pipeline_mode=pl.Buffered(1))
    return pl.pallas_call(
        functools.partial(_mla_attn_kernel, tq=tq, tk=tk),
        out_shape=jax.ShapeDtypeStruct((b, l, MLA_HEADS * MLA_V), BF16),
        grid=(b, MLA_HEADS // pair, l // tq),
        in_specs=[
            pl.BlockSpec((1, tq, pair * 2 * LANES), lambda bi, hh, qi: (bi, qi, hh)),
            pl.BlockSpec((1, l, pair * LANES), lambda bi, hh, qi: (bi, 0, hh), **once),
            pl.BlockSpec((1, l, LANES), lambda bi, hh, qi: (bi, 0, 0), **once),
            pl.BlockSpec((1, l // tk, pair * MLA_V, tk), lambda bi, hh, qi: (bi, 0, hh, 0), **once),
        ],
        out_specs=pl.BlockSpec((1, tq, pair * LANES), lambda bi, hh, qi: (bi, qi, hh)),
        scratch_shapes=[pltpu.VMEM((pair, tk, tq), F32), pltpu.VMEM((pair, tk, tq), F32),
                        pltpu.VMEM((pair, tk, tq), BF16), pltpu.VMEM((pair, tk, tq), BF16),
                        row, row, row, row, row, pltpu.VMEM((pair, MLA_V + MLA_LROWS, tq), F32)],
        compiler_params=_cparams("parallel", "parallel", "arbitrary"),
        name="mla_attention",
    )(qc, kn, kr, vt)


GDN_TM = 256
GDN_GROUP = 4
GDN_STEP_CHUNKS = 2


def _gdn_in_kernel(x_ref, g_ref, wqkv_ref, wz_ref, wba_ref, cw_ref, alog_ref, dtb_ref, tri_ref,
                   q_out, k_out, v_out, z_out, bw_out, gw_out, bg_out, pre_s, *, tm, tiles_per_seq):
    i = pl.program_id(0)
    h = _rms(x_ref[...], g_ref[...]).astype(BF16)

    @pl.when(i % tiles_per_seq == 0)
    def _():
        pre_s[...] = jnp.zeros_like(pre_s)

    row8 = lax.broadcasted_iota(jnp.int32, (8, 2 * GDN_DK), 0)
    ba = _dot(h, wba_ref[...])
    lane = lax.broadcasted_iota(jnp.int32, ba.shape, 1)
    beta = _sigmoid(ba)
    gval = -jnp.exp(alog_ref[...]) * _softplus(ba + dtb_ref[...])
    gcum = _split_dot_left(tri_ref[...], gval, 3)
    bg = jnp.where(lane < GDN_HEADS, beta, gcum)
    bg_out[...] = bg
    for hh in range(GDN_HEADS):
        sl = slice(hh * GDN_DK, (hh + 1) * GDN_DK)
        bw_out[:, sl] = jnp.broadcast_to(bg[:, hh:hh + 1], (tm, GDN_DK))
        gw_out[:, sl] = jnp.broadcast_to(bg[:, GDN_HEADS + hh:GDN_HEADS + hh + 1], (tm, GDN_DK))

    cw = cw_ref[...]
    width = 2 * GDN_DK
    for blk in range(GDN_CONV_DIM // width):
        cs = slice(blk * width, (blk + 1) * width)
        pre = _dot(h, wqkv_ref[:, cs])
        tail = pre_s[:, cs]
        conv = pre * cw[3:4, cs]
        for j in range(1, GDN_CONV):
            rolled = pltpu.roll(pre, j, 0)
            first = jnp.where(row8 < j, pltpu.roll(tail, j, 0), rolled[0:8])
            conv = conv + jnp.concatenate([first, rolled[8:]], axis=0) * cw[3 - j:4 - j, cs]
        pre_s[:, cs] = pre[tm - 8:tm]
        y = conv * _sigmoid(conv)
        if blk * width >= 2 * GDN_QK:
            v_out[:, blk * width - 2 * GDN_QK:(blk + 1) * width - 2 * GDN_QK] = y
            continue
        is_q = blk * width < GDN_QK
        out = q_out if is_q else k_out
        for half in range(2):
            yh = y[:, half * GDN_DK:(half + 1) * GDN_DK]
            yn = yh * lax.rsqrt(jnp.sum(yh * yh, axis=1, keepdims=True) + EPS)
            col = (blk * width) % GDN_QK + half * GDN_DK
            out[:, col:col + GDN_DK] = yn * GDN_DK ** -0.5 if is_q else yn

    for blk in range(GDN_V // width):
        cs = slice(blk * width, (blk + 1) * width)
        z = _dot(h, wz_ref[:, cs])
        z_out[:, cs] = (z * _sigmoid(z)).astype(z_out.dtype)


def _split_dot_left(m_bf16, x, terms):
    out = None
    r = x
    for i in range(terms):
        p = r.astype(BF16)
        d = _dot(m_bf16, p)
        out = d if out is None else out + d
        if i + 1 < terms:
            r = r - p.astype(F32)
    return out


def _gdn_in(x, g, wqkv, wz, wba, cw, alog, dtb, tri, seq):
    t, d = x.shape
    tm = min(GDN_TM, seq)
    full = lambda a: pl.BlockSpec(a.shape, lambda i: (0, 0))
    row = lambda n: pl.BlockSpec((tm, n), lambda i: (i, 0))
    wide = jax.ShapeDtypeStruct((t, GDN_QK), F32)
    return pl.pallas_call(
        functools.partial(_gdn_in_kernel, tm=tm, tiles_per_seq=seq // tm),
        out_shape=(wide, wide, wide, jax.ShapeDtypeStruct((t, GDN_V), BF16), wide, wide,
                   jax.ShapeDtypeStruct((t, LANES), F32)),
        grid=(t // tm,),
        in_specs=[row(d), full(g), full(wqkv), full(wz), full(wba), full(cw), full(alog), full(dtb), full(tri)],
        out_specs=(row(GDN_QK),) * 6 + (row(LANES),),
        scratch_shapes=[pltpu.VMEM((8, GDN_CONV_DIM), F32)],
        compiler_params=_cparams("arbitrary"),
        name="gdn_in",
    )(x, g, wqkv, wz, wba, cw, alog, dtb, tri)


def _gdn_chunk_kernel(q_ref, k_ref, v_ref, bw_ref, gw_ref, gr_ref, o_ref, s_ref, *, nchunk):
    c, grp = GDN_CHUNK, GDN_GROUP
    rows = grp * c

    @pl.when(pl.program_id(1) == 0)
    def _():
        s_ref[...] = jnp.zeros_like(s_ref)

    r = lax.broadcasted_iota(jnp.int32, (rows, rows), 0)
    cc = lax.broadcasted_iota(jnp.int32, (rows, rows), 1)
    same = lax.shift_right_logical(r, 6) == lax.shift_right_logical(cc, 6)
    incl = same & (r >= cc)
    strict = same & (r > cc)
    own_block = (lax.shift_right_logical(lax.broadcasted_iota(jnp.int32, (rows, grp * GDN_DK), 0), 6)
                 == lax.shift_right_logical(lax.broadcasted_iota(jnp.int32, (rows, grp * GDN_DK), 1), 7))

    def spread(t):
        t16 = t.astype(BF16)
        return jnp.where(own_block, jnp.concatenate([t16] * grp, axis=1), jnp.zeros((), BF16))

    ngrp = GDN_HEADS // grp
    units = [(ci, g) for ci in range(nchunk) for g in range(ngrp)]

    def stack(ref, ci, g):
        return jnp.concatenate(
            [ref[0, ci * c:(ci + 1) * c, (g * grp + j) * GDN_DK:(g * grp + j + 1) * GDN_DK] for j in range(grp)],
            axis=0)

    ks = [stack(k_ref, ci, g) for ci, g in units]
    gw = [stack(gw_ref, ci, g) for ci, g in units]
    bw = [stack(bw_ref, ci, g) for ci, g in units]
    k16 = [t.astype(BF16) for t in ks]
    kb = [ks[i] * bw[i] for i in range(len(units))]
    eg = [jnp.exp(t) for t in gw]
    decay = []
    for i, (ci, g) in enumerate(units):
        gdiff = jnp.concatenate([gw[i], gw[i]], axis=1) - gr_ref[0, ci, g:g + 1, :]
        decay.append(jnp.where(incl, jnp.exp(jnp.where(incl, gdiff, 0.0)), 0.0))
    p = [jnp.where(strict, _dot_nt(kb[i].astype(BF16), k16[i]) * decay[i], 0.0).astype(BF16)
         for i in range(len(units))]
    y = [jnp.concatenate([stack(v_ref, ci, g) * bw[i], kb[i] * eg[i]], axis=1)
         for i, (ci, g) in enumerate(units)]
    y = [y[i] - _dot(p[i], y[i].astype(BF16)) for i in range(len(units))]
    step = 2
    while step < c:
        p = [_dot(t, t).astype(BF16) for t in p]
        y = [y[i] + _dot(p[i], y[i].astype(BF16)) for i in range(len(units))]
        step *= 2
    qs = [stack(q_ref, ci, g) for ci, g in units]
    attn = [(_dot_nt(qs[i].astype(BF16), k16[i]) * decay[i]).astype(BF16) for i in range(len(units))]

    for ci in range(nchunk):
        idx = [ci * ngrp + g for g in range(ngrp)]
        s = [s_ref[g] for g in range(ngrp)]
        ws, qs_s = [], []
        for g, i in enumerate(idx):
            qd = qs[i] * eg[i]
            prod = [_dot(jnp.concatenate([y[i][j * c:(j + 1) * c, GDN_DV:], qd[j * c:(j + 1) * c]],
                                         axis=0).astype(BF16),
                         s[g][j * GDN_DK:(j + 1) * GDN_DK].astype(BF16)) for j in range(grp)]
            ws.append(jnp.concatenate([t[:c] for t in prod], axis=0))
            qs_s.append(jnp.concatenate([t[c:] for t in prod], axis=0))
        v16 = [(y[i][:, :GDN_DV] - ws[g]).astype(BF16) for g, i in enumerate(idx)]
        o = [qs_s[g] + _dot(attn[i], v16[g]) for g, i in enumerate(idx)]
        for g, i in enumerate(idx):
            last = [gw[i][j * c + c - 1:j * c + c, :] for j in range(grp)]
            g_last = jnp.concatenate([jnp.broadcast_to(t, (c, GDN_DK)) for t in last], axis=0)
            s_decay = jnp.concatenate([jnp.broadcast_to(jnp.exp(t), (GDN_DK, GDN_DV)) for t in last], axis=0)
            kd = ks[i] * jnp.exp(g_last - gw[i])
            s_ref[g] = s[g] * s_decay + _dot_tn(spread(kd), v16[g])
            for j in range(grp):
                o_ref[0, ci * c:(ci + 1) * c, (g * grp + j) * GDN_DV:(g * grp + j + 1) * GDN_DV] = (
                    o[g][j * c:(j + 1) * c].astype(o_ref.dtype))


def _gdn_chunks(q, k, v, bw, gw, grow):
    b, l, n = q.shape
    nchunk = GDN_STEP_CHUNKS
    tl = nchunk * GDN_CHUNK
    ngrp = GDN_HEADS // GDN_GROUP
    blk = pl.BlockSpec((1, tl, n), lambda bi, ci: (bi, ci, 0))
    return pl.pallas_call(
        functools.partial(_gdn_chunk_kernel, nchunk=nchunk),
        out_shape=jax.ShapeDtypeStruct((b, l, n), BF16),
        grid=(b, l // tl),
        in_specs=[blk, blk, blk, blk, blk,
                  pl.BlockSpec((1, nchunk, ngrp, GDN_GROUP * GDN_CHUNK), lambda bi, ci: (bi, ci, 0, 0))],
        out_specs=blk,
        scratch_shapes=[pltpu.VMEM((ngrp, GDN_GROUP * GDN_DK, GDN_DV), F32)],
        compiler_params=_cparams("parallel", "arbitrary"),
        name="gdn_chunks",
    )(q, k, v, bw, gw, grow)


def _gdn_out_kernel(o_ref, z_ref, gn_ref, w_ref, x_ref, out_ref, a_s):
    gn = gn_ref[...]
    for hh in range(GDN_HEADS):
        sl = slice(hh * GDN_DV, (hh + 1) * GDN_DV)
        a_s[:, sl] = (_rms(o_ref[:, sl].astype(F32), gn) * z_ref[:, sl].astype(F32)).astype(BF16)
    out_ref[...] = x_ref[...] + _dot(a_s[...], w_ref[...])


def _gdn_out(o, zs, gn, w, x):
    t, d = x.shape
    tm = min(FFN_TM, t)
    row = lambda n: pl.BlockSpec((tm, n), lambda i: (i, 0))
    full = lambda a: pl.BlockSpec(a.shape, lambda i: (0, 0))
    return pl.pallas_call(
        _gdn_out_kernel,
        out_shape=jax.ShapeDtypeStruct((t, d), F32),
        grid=(t // tm,),
        in_specs=[row(GDN_V), row(GDN_V), full(gn), full(w), row(d)],
        out_specs=row(d),
        scratch_shapes=[pltpu.VMEM((tm, GDN_V), BF16)],
        compiler_params=_cparams("parallel"),
        name="gdn_out",
    )(o, zs, gn, w, x)


def _chunk_tri(tm):
    t = np.arange(tm)
    same = (t[:, None] // GDN_CHUNK) == (t[None, :] // GDN_CHUNK)
    return jnp.asarray((same & (t[:, None] >= t[None, :])).astype(np.float32), BF16)


def _row(v):
    return v.reshape(1, -1).astype(F32)


def _pad_lanes(v, offset):
    out = jnp.zeros((1, LANES), F32)
    return out.at[0, offset:offset + v.shape[0]].set(v.astype(F32))


def _gated_deltanet(x, norm_g, w_in, conv_w, a_log, dt_bias, out_norm, w_out, batch, seq):
    t = x.shape[0]
    wqkv = w_in[:, :GDN_CONV_DIM].astype(BF16)
    wz = w_in[:, GDN_CONV_DIM:GDN_CONV_DIM + GDN_V].astype(BF16)
    wba = jnp.pad(w_in[:, GDN_CONV_DIM + GDN_V:], ((0, 0), (0, LANES - 2 * GDN_HEADS))).astype(BF16)
    q, k, v, zs, bw, gw, bg = _gdn_in(x, _row(norm_g), wqkv, wz, wba, conv_w.astype(F32),
                                      _pad_lanes(a_log, GDN_HEADS), _pad_lanes(dt_bias, GDN_HEADS),
                                      _chunk_tri(min(GDN_TM, seq)), seq)
    nc = seq // GDN_CHUNK
    grow = bg[:, GDN_HEADS:2 * GDN_HEADS].reshape(batch, nc, GDN_CHUNK, GDN_HEADS).transpose(0, 1, 3, 2)
    grow = grow.reshape(batch, nc, GDN_HEADS // GDN_GROUP, GDN_GROUP * GDN_CHUNK)
    shp = (batch, seq, GDN_QK)
    o = _gdn_chunks(q.reshape(shp), k.reshape(shp), v.reshape(shp), bw.reshape(shp), gw.reshape(shp), grow)
    return _gdn_out(o.reshape(t, GDN_V), zs, _row(out_norm), w_out.astype(BF16), x)


def _stick_breaking(x, norm_g, w_qkv, w_out, batch, seq):
    t = x.shape[0]
    n = SB_HEADS * SB_HD
    w = jnp.concatenate([w_qkv[:, :n] * (SB_HD ** -0.5 * LOG2E), w_qkv[:, n:]], axis=1).astype(BF16)
    qkv = _norm_matmul(x, _row(norm_g), w, 1024, BF16)
    o = _sb_attention(qkv.reshape(batch, seq, 3 * n), _sb_suffix_matrix(SB_TK))
    return _matmul_res(o.reshape(t, n), w_out.astype(BF16), x)


def _mla(x, norm_g, positions, w_in, q_norm, w_uq, kv_norm, w_ukv, w_out, batch, seq):
    t = x.shape[0]
    wq = w_in[:, :MLA_Q_RANK].astype(BF16)
    wkv = w_in[:, MLA_Q_RANK:MLA_Q_RANK + MLA_KV_RANK].astype(BF16)
    wkr = jnp.pad(w_in[:, MLA_Q_RANK + MLA_KV_RANK:], ((0, 0), (0, LANES - MLA_ROPE))).astype(BF16)
    qd = MLA_NOPE + MLA_ROPE
    wuq = jnp.pad(w_uq.reshape(MLA_Q_RANK, MLA_HEADS, qd), ((0, 0), (0, 0), (0, 2 * LANES - qd)))
    wuq = wuq.reshape(MLA_Q_RANK, MLA_HEADS * 2 * LANES).astype(BF16)
    half = MLA_ROPE // 2
    inv_freq = ROPE_THETA ** (-jnp.arange(half, dtype=F32) / half)
    freqs = jnp.concatenate([inv_freq, inv_freq])
    invf = jnp.zeros((2, LANES), F32).at[0, :MLA_ROPE].set(freqs).at[1, LANES // 2:LANES // 2 + MLA_ROPE].set(freqs)
    wukv = w_ukv.reshape(MLA_KV_RANK, MLA_HEADS, MLA_NOPE + MLA_V)
    wuk = wukv[:, :, :MLA_NOPE].reshape(MLA_KV_RANK, MLA_HEADS * MLA_NOPE).astype(BF16)
    wuvt = wukv[:, :, MLA_NOPE:].reshape(MLA_KV_RANK, MLA_HEADS * MLA_V).T.astype(BF16)
    qc, kn, kr, vt = _mla_proj(x, _row(norm_g), wq, wkv, wkr, _row(q_norm), wuq, _row(kv_norm),
                               wuk, wuvt, positions.reshape(t, 1), invf)
    tk = vt.shape[-1]
    o = _mla_attention(qc.reshape(batch, seq, MLA_HEADS * 2 * LANES), kn.reshape(batch, seq, MLA_HEADS * MLA_NOPE),
                       kr.reshape(batch, seq, LANES), vt.reshape(batch, seq // tk, MLA_HEADS * MLA_V, tk))
    return _matmul_res(o.reshape(t, MLA_HEADS * MLA_V), w_out.astype(BF16), x)


def kernel(x, p, positions, ffn1_norm, ffn1_w_gate, ffn1_w_up, ffn1_w_down, mix_norm, ffn2_norm, ffn2_w_gate, ffn2_w_up, ffn2_w_down, ple_norm, ple_w_gate, ple_w_proj, gdn_w_in, gdn_conv_w, gdn_a_log, gdn_dt_bias, gdn_out_norm, gdn_w_out, sb_w_qkv, sb_w_out, mla_w_in, mla_q_norm, mla_w_uq, mla_kv_norm, mla_w_ukv, mla_w_out, final_norm):
    batch, seq, d = x.shape
    t = batch * seq
    depth = ffn1_norm.shape[0]
    x = x.reshape(t, d)
    p = p.reshape(depth, t, PLE_DIM)
    for i in range(depth):
        mixer, slot = i % N_MIXERS, i // N_MIXERS
        x = _ffn(x, _row(ffn1_norm[i]), ffn1_w_gate[i].astype(BF16), ffn1_w_up[i].astype(BF16),
                 ffn1_w_down[i].astype(BF16))
        if mixer == 0:
            x = _gated_deltanet(x, mix_norm[i], gdn_w_in[slot], gdn_conv_w[slot], gdn_a_log[slot],
                                gdn_dt_bias[slot], gdn_out_norm[slot], gdn_w_out[slot], batch, seq)
        elif mixer == 1:
            x = _stick_breaking(x, mix_norm[i], sb_w_qkv[slot], sb_w_out[slot], batch, seq)
        else:
            x = _mla(x, mix_norm[i], positions, mla_w_in[slot], mla_q_norm[slot], mla_w_uq[slot],
                     mla_kv_norm[slot], mla_w_ukv[slot], mla_w_out[slot], batch, seq)
        x = _ffn(x, _row(ffn2_norm[i]), ffn2_w_gate[i].astype(BF16), ffn2_w_up[i].astype(BF16),
                 ffn2_w_down[i].astype(BF16))
        x = _ple(x, _row(ple_norm[i]), ple_w_gate[i].astype(BF16), p, i, ple_w_proj[i].astype(BF16),
                 _row(final_norm), final=(i == depth - 1))
    return x.reshape(batch, seq, d)
```

```python
import functools

import numpy as np
import jax
import jax.numpy as jnp
from jax import lax
from jax.experimental import pallas as pl
from jax.experimental.pallas import tpu as pltpu

F32 = jnp.float32
BF16 = jnp.bfloat16

D_MODEL = 1024
DEPTH = 4
N_MIXERS = 3
PLE_DIM = 256
D_FF = 2816
EPS = 1e-6

GDN_HEADS = 8
GDN_DK = 128
GDN_DV = 128
GDN_CONV = 4
GDN_CHUNK = 64
GDN_QK = GDN_HEADS * GDN_DK
GDN_V = GDN_HEADS * GDN_DV
GDN_CONV_DIM = 2 * GDN_QK + GDN_V

SB_HEADS = 16
SB_HD = 64

MLA_HEADS = 8
MLA_Q_RANK = 384
MLA_KV_RANK = 256
MLA_NOPE = 128
MLA_ROPE = 64
MLA_V = 128
ROPE_THETA = 10000.0

LANES = 128
VMEM_LIMIT = 56 * 1024 * 1024

FFN_TM = 1024
FFN_CHUNK = 512
TOK_TM = 512
SB_TQ = 512
SB_TK = 128
SB_PAIRS = 2
MLA_TQ = 2 * TOK_TM
MLA_PAIR = 2
MLA_LROWS = 16

LOG2E = 1.4426950408889634
NEG_BIG = -1e30
SB_SPLIT_TERMS = 1
SB_DEAD_BITS = 150.0


def _cparams(*sem):
    return pltpu.CompilerParams(dimension_semantics=sem, vmem_limit_bytes=VMEM_LIMIT)


def _dot(a, b):
    return jnp.dot(a, b, preferred_element_type=F32)


def _dot_nt(a, b):
    return lax.dot_general(a, b, (((1,), (1,)), ((), ())), preferred_element_type=F32)


def _dot_tn(a, b):
    return lax.dot_general(a, b, (((0,), (0,)), ((), ())), preferred_element_type=F32)


def _rms(xf, g):
    return xf * lax.rsqrt(jnp.mean(xf * xf, axis=-1, keepdims=True) + EPS) * g


def _sigmoid(x):
    return 1.0 / (1.0 + jnp.exp2(x * -LOG2E))


def _softplus(x):
    return jnp.maximum(x, 0.0) + jnp.log(1.0 + jnp.exp(-jnp.abs(x)))


def _split_dot(x, m_bf16, terms):
    out = None
    r = x
    for i in range(terms):
        p = r.astype(BF16)
        d = _dot(p, m_bf16)
        out = d if out is None else out + d
        if i + 1 < terms:
            r = r - p.astype(F32)
    return out


def _ffn_kernel(x_ref, g_ref, wg_ref, wu_ref, wd_ref, o_ref):
    x = x_ref[...]
    h = _rms(x, g_ref[...]).astype(BF16)
    acc = None
    for lo in range(0, D_FF, FFN_CHUNK):
        hi = min(lo + FFN_CHUNK, D_FF)
        gate = _dot(h, wg_ref[:, lo:hi])
        up = _dot(h, wu_ref[:, lo:hi])
        act = (gate * _sigmoid(gate) * up).astype(BF16)
        part = _dot(act, wd_ref[lo:hi, :])
        acc = part if acc is None else acc + part
    o_ref[...] = x + 0.5 * acc


def _ffn(x, g, wg, wu, wd):
    t, d = x.shape
    tm = min(FFN_TM, t)
    once = lambda a: pl.BlockSpec(a.shape, lambda i: (0, 0), pipeline_mode=pl.Buffered(1))
    return pl.pallas_call(
        _ffn_kernel,
        out_shape=jax.ShapeDtypeStruct((t, d), F32),
        grid=(t // tm,),
        in_specs=[pl.BlockSpec((tm, d), lambda i: (i, 0)), once(g), once(wg), once(wu), once(wd)],
        out_specs=pl.BlockSpec((tm, d), lambda i: (i, 0)),
        compiler_params=_cparams("parallel"),
        name="ffn",
    )(x, g, wg, wu, wd)


def _ple_kernel(x_ref, g_ref, wgate_ref, p_ref, wp_ref, fin_ref, o_ref, *, final):
    x = x_ref[...]
    h = _rms(x, g_ref[...]).astype(BF16)
    gate = _sigmoid(_dot(h, wgate_ref[...]))
    y = x + gate * _dot(p_ref[0].astype(BF16), wp_ref[...])
    if final:
        y = _rms(y, fin_ref[...])
    o_ref[...] = y


def _ple(x, g, wgate, p, layer, wp, fin, final):
    t, d = x.shape
    tm = min(FFN_TM, t)
    return pl.pallas_call(
        functools.partial(_ple_kernel, final=final),
        out_shape=jax.ShapeDtypeStruct((t, d), F32),
        grid=(t // tm,),
        in_specs=[
            pl.BlockSpec((tm, d), lambda i: (i, 0)),
            pl.BlockSpec((1, d), lambda i: (0, 0)),
            pl.BlockSpec((d, d), lambda i: (0, 0)),
            pl.BlockSpec((1, tm, PLE_DIM), lambda i: (layer, i, 0)),
            pl.BlockSpec((PLE_DIM, d), lambda i: (0, 0)),
            pl.BlockSpec((1, d), lambda i: (0, 0)),
        ],
        out_specs=pl.BlockSpec((tm, d), lambda i: (i, 0)),
        compiler_params=_cparams("parallel"),
        name="ple",
    )(x, g, wgate, p, wp, fin)


def _norm_matmul_kernel(x_ref, g_ref, w_ref, o_ref, *, tn):
    h = _rms(x_ref[...], g_ref[...]).astype(BF16)
    for lo in range(0, w_ref.shape[1], tn):
        o_ref[:, lo:lo + tn] = _dot(h, w_ref[:, lo:lo + tn]).astype(o_ref.dtype)


def _norm_matmul(x, g, w, tn, out_dtype):
    t, d = x.shape
    n = w.shape[1]
    tm = min(FFN_TM, t)
    once = lambda a: pl.BlockSpec(a.shape, lambda i: (0, 0), pipeline_mode=pl.Buffered(1))
    return pl.pallas_call(
        functools.partial(_norm_matmul_kernel, tn=tn),
        out_shape=jax.ShapeDtypeStruct((t, n), out_dtype),
        grid=(t // tm,),
        in_specs=[pl.BlockSpec((tm, d), lambda i: (i, 0)), once(g), once(w)],
        out_specs=pl.BlockSpec((tm, n), lambda i: (i, 0)),
        compiler_params=_cparams("parallel"),
        name="norm_matmul",
    )(x, g, w)


def _matmul_res_kernel(a_ref, w_ref, x_ref, o_ref):
    o_ref[...] = x_ref[...] + _dot(a_ref[...].astype(BF16), w_ref[...])


def _matmul_res(a, w, x):
    t, k = a.shape
    d = w.shape[1]
    tm = min(FFN_TM, t)
    return pl.pallas_call(
        _matmul_res_kernel,
        out_shape=jax.ShapeDtypeStruct((t, d), F32),
        grid=(t // tm,),
        in_specs=[
            pl.BlockSpec((tm, k), lambda i: (i, 0)),
            pl.BlockSpec((k, d), lambda i: (0, 0)),
            pl.BlockSpec((tm, d), lambda i: (i, 0)),
        ],
        out_specs=pl.BlockSpec((tm, d), lambda i: (i, 0)),
        compiler_params=_cparams("parallel"),
        name="matmul_res",
    )(a, w, x)


def _sb_kernel(q_ref, k_ref, v_ref, u_ref, o_ref,
               z0, z1, lb0, lb1, sf0, sf1, a0, a1, run_a, run_b, acc_o, *, tq, tk):
    qi = pl.program_id(2)
    ratio = tq // tk
    n_tiles = ratio * (qi + 1)
    u2 = u_ref[...]
    zb, lbb, sfb, ab = (z0, z1), (lb0, lb1), (sf0, sf1), (a0, a1)
    first = lax.broadcasted_iota(jnp.int32, (tk, LANES), 1) < SB_HD
    row_minus_col = (lax.broadcasted_iota(jnp.int32, (tq, 2 * tk), 0)
                     - (lax.broadcasted_iota(jnp.int32, (tq, 2 * tk), 1) & (tk - 1)))
    acc_o[...] = jnp.zeros_like(acc_o)
    run_a[...] = jnp.zeros_like(run_a)
    run_b[...] = jnp.zeros_like(run_b)

    def key_tile(ref, hp, t):
        kb = jnp.maximum(ratio * qi + ratio - 1 - t, 0)
        x = ref[0, pl.ds(pl.multiple_of(kb * tk, tk), tk), hp * LANES:(hp + 1) * LANES]
        zero = jnp.zeros_like(x)
        return jnp.concatenate([jnp.where(first, x, zero), jnp.where(first, zero, x)], axis=0)

    def stage_q(hp, t, par, r0=0):
        q = q_ref[0, r0:, hp * LANES:(hp + 1) * LANES]
        zb[par][hp, r0:, :] = _dot_nt(q, key_tile(k_ref, hp, t))

    def stage_s1(hp, par, thr, r0=0):
        z = zb[par][hp, r0:, :]
        neg_abs = lax.bitcast_convert_type(
            lax.bitcast_convert_type(z, jnp.uint32) | jnp.uint32(0x80000000), F32)
        sp = jnp.maximum(z, 0.0) + jnp.log(1.0 + jnp.exp2(neg_abs)) * LOG2E
        ra, rb = run_a[hp, r0:, :], run_b[hp, r0:, :]
        lb = z - sp - jnp.concatenate([ra, rb], axis=1)
        if thr is not None:
            valid = row_minus_col[r0:] > thr
            sp = jnp.where(valid, sp, 0.0)
            lb = jnp.where(valid, lb, NEG_BIG)
        lbb[par][hp, r0:, :] = lb
        sfb[par][hp, r0:, :] = _split_dot(sp, u2, SB_SPLIT_TERMS)
        if r0:
            lbb[par][hp, :r0, :] = jnp.full((r0, 2 * tk), NEG_BIG, F32)
            sfb[par][hp, :r0, :] = jnp.zeros((r0, 2 * tk), F32)
        run_a[hp, r0:, :] = ra + jnp.sum(sp[:, :tk], axis=1, keepdims=True)
        run_b[hp, r0:, :] = rb + jnp.sum(sp[:, tk:], axis=1, keepdims=True)

    def stage_s2(hp, par):
        ab[par][hp] = jnp.exp2(lbb[par][hp] - sfb[par][hp]).astype(BF16)

    def stage_p(hp, t, par):
        acc_o[hp] += _dot(ab[par][hp], key_tile(v_ref, hp, t))

    def step(u, par, thr=None, q_=True, s1=True, s2=True, p_=True, q_r0=0):
        for hp in range(SB_PAIRS):
            if q_:
                stage_q(hp, u, par, q_r0)
            if s1:
                stage_s1(hp, 1 - par, thr, 0 if thr is None else thr)
            if s2:
                stage_s2(hp, par)
            if p_:
                stage_p(hp, u - 3, 1 - par)

    for u in range(ratio + 1):
        step(u, u % 2, thr=(ratio - u) * tk, s1=u >= 1, s2=u >= 2, p_=u >= 3,
             q_r0=max(ratio - 1 - u, 0) * tk)

    n_pairs = (n_tiles - ratio) // 2

    def body(c):
        j, _ = c
        u = ratio + 1 + 2 * j
        step(u, (ratio + 1) % 2)
        step(u + 1, ratio % 2)
        return j + 1, jnp.min(jnp.minimum(run_a[...], run_b[...]))

    def live(c):
        j, lowest = c
        return (j < n_pairs) & (lowest < SB_DEAD_BITS)

    j_end, _ = lax.while_loop(live, body, (jnp.int32(0), jnp.float32(0.0)))
    n_done = ratio + 2 * j_end
    step(n_done + 1, (ratio + 1) % 2, q_=False, s1=False)
    step(n_done + 2, ratio % 2, q_=False, s1=False, s2=False)
    for hp in range(SB_PAIRS):
        o_ref[0, :, hp * LANES:(hp + 1) * LANES] = acc_o[hp].astype(o_ref.dtype)


def _sb_attention(qkv, u2):
    b, l, _ = qkv.shape
    tq, tk = min(SB_TQ, l), SB_TK
    assert (tq // tk) % 2 == 0 and tk == LANES
    ngrp = SB_HEADS * SB_HD // (SB_PAIRS * LANES)
    width = SB_PAIRS * LANES
    wide = pltpu.VMEM((SB_PAIRS, tq, 2 * tk), F32)
    half = pltpu.VMEM((SB_PAIRS, tq, 2 * tk), BF16)
    col = pltpu.VMEM((SB_PAIRS, tq, LANES), F32)
    return pl.pallas_call(
        functools.partial(_sb_kernel, tq=tq, tk=tk),
        out_shape=jax.ShapeDtypeStruct((b, l, SB_HEADS * SB_HD), BF16),
        grid=(b, ngrp, l // tq),
        in_specs=[
            pl.BlockSpec((1, tq, width), lambda bi, j, qi: (bi, qi, j)),
            pl.BlockSpec((1, l, width), lambda bi, j, qi: (bi, 0, ngrp + j)),
            pl.BlockSpec((1, l, width), lambda bi, j, qi: (bi, 0, 2 * ngrp + j)),
            pl.BlockSpec((2 * tk, 2 * tk), lambda bi, j, qi: (0, 0)),
        ],
        out_specs=pl.BlockSpec((1, tq, width), lambda bi, j, qi: (bi, qi, j)),
        scratch_shapes=[wide, wide, wide, wide, wide, wide, half, half, col, col, col],
        compiler_params=_cparams("parallel", "parallel", "arbitrary"),
        name="sb_attention",
    )(qkv, qkv, qkv, u2)


def _sb_suffix_matrix(tk):
    s = np.arange(2 * tk)
    same = (s[:, None] // tk) == (s[None, :] // tk)
    return jnp.asarray((same & (s[:, None] > s[None, :])).astype(np.float32), BF16)


def _mla_proj_kernel(x_ref, g_ref, wq_ref, wkv_ref, wkr_ref, qn_ref, wuq_ref, kvn_ref, wuk_ref, wuvt_ref,
                     pos_ref, invf_ref, q_out, kn_out, kr_out, vt_out):
    h = _rms(x_ref[...], g_ref[...]).astype(BF16)
    cq = _dot(h, wq_ref[...])
    ckv = _dot(h, wkv_ref[...])
    kr = _dot(h, wkr_ref[...])
    q = _dot(_rms(cq, qn_ref[...]).astype(BF16), wuq_ref[...])
    ckv_n = _rms(ckv, kvn_ref[...]).astype(BF16)
    kn_out[...] = _dot(ckv_n, wuk_ref[...]).astype(kn_out.dtype)
    vt_out[0] = _dot_nt(wuvt_ref[...], ckv_n).astype(vt_out.dtype)

    pos = pos_ref[...].astype(F32)
    hm = pos.shape[0] // 2
    ang = pos[:hm] * invf_ref[0:1, :] + pos[hm:] * invf_ref[1:2, :]
    cos_p, sin_p = jnp.cos(ang), jnp.sin(ang)
    cos = jnp.concatenate([cos_p, pltpu.roll(cos_p, LANES // 2, 1)], axis=0)
    sin = jnp.concatenate([sin_p, pltpu.roll(sin_p, LANES // 2, 1)], axis=0)
    lane = lax.broadcasted_iota(jnp.int32, cos.shape, 1)
    half = MLA_ROPE // 2
    sin_lo = jnp.where(lane < half, -sin, 0.0)
    sin_hi = jnp.where((lane >= half) & (lane < MLA_ROPE), sin, 0.0)

    def rope(t):
        return (t * cos + pltpu.roll(t, LANES - half, 1) * sin_lo + pltpu.roll(t, half, 1) * sin_hi)

    kr_out[...] = rope(kr).astype(kr_out.dtype)
    scale = (MLA_NOPE + MLA_ROPE) ** -0.5 * LOG2E
    for hh in range(MLA_HEADS):
        base = hh * 2 * LANES
        q_out[:, base:base + LANES] = (q[:, base:base + LANES] * scale).astype(q_out.dtype)
        q_out[:, base + LANES:base + 2 * LANES] = (
            rope(q[:, base + LANES:base + 2 * LANES]) * scale).astype(q_out.dtype)


def _mla_proj(x, g, wq, wkv, wkr, qn, wuq, kvn, wuk, wuvt, pos, invf):
    t, d = x.shape
    tm = min(TOK_TM, t)
    full = lambda a: pl.BlockSpec(a.shape, lambda i: (0, 0))
    nq = MLA_HEADS * 2 * LANES
    nv = MLA_HEADS * MLA_V
    return pl.pallas_call(
        _mla_proj_kernel,
        out_shape=(jax.ShapeDtypeStruct((t, nq), BF16), jax.ShapeDtypeStruct((t, MLA_HEADS * MLA_NOPE), BF16),
                   jax.ShapeDtypeStruct((t, LANES), BF16), jax.ShapeDtypeStruct((t // tm, nv, tm), BF16)),
        grid=(t // tm,),
        in_specs=[pl.BlockSpec((tm, d), lambda i: (i, 0)), full(g), full(wq), full(wkv), full(wkr),
                  full(qn), full(wuq), full(kvn), full(wuk), full(wuvt),
                  pl.BlockSpec((tm, 1), lambda i: (i, 0)), full(invf)],
        out_specs=(pl.BlockSpec((tm, nq), lambda i: (i, 0)),
                   pl.BlockSpec((tm, MLA_HEADS * MLA_NOPE), lambda i: (i, 0)),
                   pl.BlockSpec((tm, LANES), lambda i: (i, 0)),
                   pl.BlockSpec((1, nv, tm), lambda i: (i, 0, 0))),
        compiler_params=_cparams("parallel"),
        name="mla_proj",
    )(x, g, wq, wkv, wkr, qn, wuq, kvn, wuk, wuvt, pos, invf)


def _mla_attn_kernel(q_ref, kn_ref, kr_ref, vt_ref, o_ref, s0, s1, p0, p1, c0, c1, x0, x1, m_s, acc_s,
                     *, tq, tk):
    qi = pl.program_id(2)
    sb, pb, cb, xb = (s0, s1), (p0, p1), (c0, c1), (x0, x1)
    m_s[...] = jnp.full_like(m_s, -jnp.inf)
    acc_s[...] = jnp.zeros_like(acc_s)
    p1[...] = jnp.zeros_like(p1)
    c1[...] = jnp.ones_like(c1)
    query_minus_key = (lax.broadcasted_iota(jnp.int32, (tk, tq), 1)
                       - lax.broadcasted_iota(jnp.int32, (tk, tq), 0))

    def tile_index(t):
        return jnp.maximum(t, 0)

    def stage_q(hd, t, par):
        start = pl.multiple_of(tile_index(t) * tk, tk)
        kc = jnp.concatenate([kn_ref[0, pl.ds(start, tk), hd * LANES:(hd + 1) * LANES],
                              kr_ref[0, pl.ds(start, tk), :]], axis=1)
        q = q_ref[0, :, hd * 2 * LANES:(hd + 1) * 2 * LANES]
        s = _dot_nt(kc, q)
        sb[par][hd] = s
        xb[par][hd] = jnp.max(s, axis=0, keepdims=True)

    def stage_s(hd, par, thr):
        s = sb[par][hd]
        if thr is None:
            tile_max = xb[par][hd]
        else:
            s = jnp.where(query_minus_key >= thr, s, -jnp.inf)
            tile_max = jnp.max(s, axis=0, keepdims=True)
        m_prev = m_s[hd]
        m_new = jnp.maximum(m_prev, tile_max)
        p = jnp.exp2(s - m_new)
        corr = jnp.exp2(m_prev - m_new)
        pb[par][hd] = p.astype(BF16)
        cb[par][hd] = corr
        m_s[hd] = m_new

    ones_rows = jnp.ones((MLA_LROWS, tk), BF16)

    def stage_p(hd, t, par):
        v_t = jnp.concatenate([vt_ref[0, tile_index(t), hd * MLA_V:(hd + 1) * MLA_V, :], ones_rows], axis=0)
        acc_s[hd] = acc_s[hd] * cb[par][hd] + _dot(v_t, pb[par][hd])

    def step(t, par, thr=None, q_=True, s_=True, p_=True):
        for hd in range(MLA_PAIR):
            if q_:
                stage_q(hd, t + 1, 1 - par)
            if s_:
                stage_s(hd, par, thr)
            if p_:
                stage_p(hd, t - 1, 1 - par)

    for hd in range(MLA_PAIR):
        stage_q(hd, 0, 0)

    def body(j, c):
        step(2 * j, 0)
        step(2 * j + 1, 1)
        return c

    lax.fori_loop(0, qi, body, 0)
    d0 = 2 * qi
    step(d0, 0, thr=0)
    step(d0 + 1, 1, thr=tk, q_=False)
    step(d0 + 2, 0, q_=False, s_=False)
    for hd in range(MLA_PAIR):
        acc = acc_s[hd]
        o_ref[0, :, hd * MLA_V:(hd + 1) * MLA_V] = (acc[:MLA_V] / acc[MLA_V:MLA_V + 1]).T.astype(o_ref.dtype)


def _mla_attention(qc, kn, kr, vt):
    b, l, _ = qc.shape
    tq, tk = min(MLA_TQ, l), vt.shape[-1]
    assert tq == 2 * tk and tk % LANES == 0 and MLA_V == LANES and MLA_NOPE == LANES
    pair = MLA_PAIR
    row = pltpu.VMEM((pair, 1, tq), F32)
    once = dict(pipeline_mode=pl.Buffered(1))
    return pl.pallas_call(
        functools.partial(_mla_attn_kernel, tq=tq, tk=tk),
        out_shape=jax.ShapeDtypeStruct((b, l, MLA_HEADS * MLA_V), BF16),
        grid=(b, MLA_HEADS // pair, l // tq),
        in_specs=[
            pl.BlockSpec((1, tq, pair * 2 * LANES), lambda bi, hh, qi: (bi, qi, hh)),
            pl.BlockSpec((1, l, pair * LANES), lambda bi, hh, qi: (bi, 0, hh), **once),
            pl.BlockSpec((1, l, LANES), lambda bi, hh, qi: (bi, 0, 0), **once),
            pl.BlockSpec((1, l // tk, pair * MLA_V, tk), lambda bi, hh, qi: (bi, 0, hh, 0), **once),
        ],
        out_specs=pl.BlockSpec((1, tq, pair * LANES), lambda bi, hh, qi: (bi, qi, hh)),
        scratch_shapes=[pltpu.VMEM((pair, tk, tq), F32), pltpu.VMEM((pair, tk, tq), F32),
                        pltpu.VMEM((pair, tk, tq), BF16), pltpu.VMEM((pair, tk, tq), BF16),
                        row, row, row, row, row, pltpu.VMEM((pair, MLA_V + MLA_LROWS, tq), F32)],
        compiler_params=_cparams("parallel", "parallel", "arbitrary"),
        name="mla_attention",
    )(qc, kn, kr, vt)


GDN_TM = 256
GDN_GROUP = 4
GDN_STEP_CHUNKS = 2


def _gdn_in_kernel(x_ref, g_ref, wqkv_ref, wz_ref, wba_ref, cw_ref, alog_ref, dtb_ref, tri_ref,
                   q_out, k_out, v_out, z_out, bw_out, gw_out, bg_out, pre_s, *, tm, tiles_per_seq):
    i = pl.program_id(0)
    h = _rms(x_ref[...], g_ref[...]).astype(BF16)

    @pl.when(i % tiles_per_seq == 0)
    def _():
        pre_s[...] = jnp.zeros_like(pre_s)

    row8 = lax.broadcasted_iota(jnp.int32, (8, 2 * GDN_DK), 0)
    ba = _dot(h, wba_ref[...])
    lane = lax.broadcasted_iota(jnp.int32, ba.shape, 1)
    beta = _sigmoid(ba)
    gval = -jnp.exp(alog_ref[...]) * _softplus(ba + dtb_ref[...])
    gcum = _split_dot_left(tri_ref[...], gval, 3)
    bg = jnp.where(lane < GDN_HEADS, beta, gcum)
    bg_out[...] = bg
    for hh in range(GDN_HEADS):
        sl = slice(hh * GDN_DK, (hh + 1) * GDN_DK)
        bw_out[:, sl] = jnp.broadcast_to(bg[:, hh:hh + 1], (tm, GDN_DK))
        gw_out[:, sl] = jnp.broadcast_to(bg[:, GDN_HEADS + hh:GDN_HEADS + hh + 1], (tm, GDN_DK))

    cw = cw_ref[...]
    width = 2 * GDN_DK
    for blk in range(GDN_CONV_DIM // width):
        cs = slice(blk * width, (blk + 1) * width)
        pre = _dot(h, wqkv_ref[:, cs])
        tail = pre_s[:, cs]
        conv = pre * cw[3:4, cs]
        for j in range(1, GDN_CONV):
            rolled = pltpu.roll(pre, j, 0)
            first = jnp.where(row8 < j, pltpu.roll(tail, j, 0), rolled[0:8])
            conv = conv + jnp.concatenate([first, rolled[8:]], axis=0) * cw[3 - j:4 - j, cs]
        pre_s[:, cs] = pre[tm - 8:tm]
        y = conv * _sigmoid(conv)
        if blk * width >= 2 * GDN_QK:
            v_out[:, blk * width - 2 * GDN_QK:(blk + 1) * width - 2 * GDN_QK] = y
            continue
        is_q = blk * width < GDN_QK
        out = q_out if is_q else k_out
        for half in range(2):
            yh = y[:, half * GDN_DK:(half + 1) * GDN_DK]
            yn = yh * lax.rsqrt(jnp.sum(yh * yh, axis=1, keepdims=True) + EPS)
            col = (blk * width) % GDN_QK + half * GDN_DK
            out[:, col:col + GDN_DK] = yn * GDN_DK ** -0.5 if is_q else yn

    for blk in range(GDN_V // width):
        cs = slice(blk * width, (blk + 1) * width)
        z = _dot(h, wz_ref[:, cs])
        z_out[:, cs] = (z * _sigmoid(z)).astype(z_out.dtype)


def _split_dot_left(m_bf16, x, terms):
    out = None
    r = x
    for i in range(terms):
        p = r.astype(BF16)
        d = _dot(m_bf16, p)
        out = d if out is None else out + d
        if i + 1 < terms:
            r = r - p.astype(F32)
    return out


def _gdn_in(x, g, wqkv, wz, wba, cw, alog, dtb, tri, seq):
    t, d = x.shape
    tm = min(GDN_TM, seq)
    full = lambda a: pl.BlockSpec(a.shape, lambda i: (0, 0))
    row = lambda n: pl.BlockSpec((tm, n), lambda i: (i, 0))
    wide = jax.ShapeDtypeStruct((t, GDN_QK), F32)
    return pl.pallas_call(
        functools.partial(_gdn_in_kernel, tm=tm, tiles_per_seq=seq // tm),
        out_shape=(wide, wide, wide, jax.ShapeDtypeStruct((t, GDN_V), BF16), wide, wide,
                   jax.ShapeDtypeStruct((t, LANES), F32)),
        grid=(t // tm,),
        in_specs=[row(d), full(g), full(wqkv), full(wz), full(wba), full(cw), full(alog), full(dtb), full(tri)],
        out_specs=(row(GDN_QK),) * 6 + (row(LANES),),
        scratch_shapes=[pltpu.VMEM((8, GDN_CONV_DIM), F32)],
        compiler_params=_cparams("arbitrary"),
        name="gdn_in",
    )(x, g, wqkv, wz, wba, cw, alog, dtb, tri)


def _gdn_chunk_kernel(q_ref, k_ref, v_ref, bw_ref, gw_ref, gr_ref, o_ref, s_ref, *, nchunk):
    c, grp = GDN_CHUNK, GDN_GROUP
    rows = grp * c

    @pl.when(pl.program_id(1) == 0)
    def _():
        s_ref[...] = jnp.zeros_like(s_ref)

    r = lax.broadcasted_iota(jnp.int32, (rows, rows), 0)
    cc = lax.broadcasted_iota(jnp.int32, (rows, rows), 1)
    same = lax.shift_right_logical(r, 6) == lax.shift_right_logical(cc, 6)
    incl = same & (r >= cc)
    strict = same & (r > cc)
    own_block = (lax.shift_right_logical(lax.broadcasted_iota(jnp.int32, (rows, grp * GDN_DK), 0), 6)
                 == lax.shift_right_logical(lax.broadcasted_iota(jnp.int32, (rows, grp * GDN_DK), 1), 7))

    def spread(t):
        t16 = t.astype(BF16)
        return jnp.where(own_block, jnp.concatenate([t16] * grp, axis=1), jnp.zeros((), BF16))

    ngrp = GDN_HEADS // grp
    units = [(ci, g) for ci in range(nchunk) for g in range(ngrp)]

    def stack(ref, ci, g):
        return jnp.concatenate(
            [ref[0, ci * c:(ci + 1) * c, (g * grp + j) * GDN_DK:(g * grp + j + 1) * GDN_DK] for j in range(grp)],
            axis=0)

    ks = [stack(k_ref, ci, g) for ci, g in units]
    gw = [stack(gw_ref, ci, g) for ci, g in units]
    bw = [stack(bw_ref, ci, g) for ci, g in units]
    k16 = [t.astype(BF16) for t in ks]
    kb = [ks[i] * bw[i] for i in range(len(units))]
    eg = [jnp.exp(t) for t in gw]
    decay = []
    for i, (ci, g) in enumerate(units):
        gdiff = jnp.concatenate([gw[i], gw[i]], axis=1) - gr_ref[0, ci, g:g + 1, :]
        decay.append(jnp.where(incl, jnp.exp(jnp.where(incl, gdiff, 0.0)), 0.0))
    p = [jnp.where(strict, _dot_nt(kb[i].astype(BF16), k16[i]) * decay[i], 0.0).astype(BF16)
         for i in range(len(units))]
    y = [jnp.concatenate([stack(v_ref, ci, g) * bw[i], kb[i] * eg[i]], axis=1)
         for i, (ci, g) in enumerate(units)]
    y = [y[i] - _dot(p[i], y[i].astype(BF16)) for i in range(len(units))]
    step = 2
    while step < c:
        p = [_dot(t, t).astype(BF16) for t in p]
        y = [y[i] + _dot(p[i], y[i].astype(BF16)) for i in range(len(units))]
        step *= 2
    qs = [stack(q_ref, ci, g) for ci, g in units]
    attn = [(_dot_nt(qs[i].astype(BF16), k16[i]) * decay[i]).astype(BF16) for i in range(len(units))]

    for ci in range(nchunk):
        idx = [ci * ngrp + g for g in range(ngrp)]
        s = [s_ref[g] for g in range(ngrp)]
        ws, qs_s = [], []
        for g, i in enumerate(idx):
            qd = qs[i] * eg[i]
            prod = [_dot(jnp.concatenate([y[i][j * c:(j + 1) * c, GDN_DV:], qd[j * c:(j + 1) * c]],
                                         axis=0).astype(BF16),
                         s[g][j * GDN_DK:(j + 1) * GDN_DK].astype(BF16)) for j in range(grp)]
            ws.append(jnp.concatenate([t[:c] for t in prod], axis=0))
            qs_s.append(jnp.concatenate([t[c:] for t in prod], axis=0))
        v16 = [(y[i][:, :GDN_DV] - ws[g]).astype(BF16) for g, i in enumerate(idx)]
        o = [qs_s[g] + _dot(attn[i], v16[g]) for g, i in enumerate(idx)]
        for g, i in enumerate(idx):
            last = [gw[i][j * c + c - 1:j * c + c, :] for j in range(grp)]
            g_last = jnp.concatenate([jnp.broadcast_to(t, (c, GDN_DK)) for t in last], axis=0)
            s_decay = jnp.concatenate([jnp.broadcast_to(jnp.exp(t), (GDN_DK, GDN_DV)) for t in last], axis=0)
            kd = ks[i] * jnp.exp(g_last - gw[i])
            s_ref[g] = s[g] * s_decay + _dot_tn(spread(kd), v16[g])
            for j in range(grp):
                o_ref[0, ci * c:(ci + 1) * c, (g * grp + j) * GDN_DV:(g * grp + j + 1) * GDN_DV] = (
                    o[g][j * c:(j + 1) * c].astype(o_ref.dtype))


def _gdn_chunks(q, k, v, bw, gw, grow):
    b, l, n = q.shape
    nchunk = GDN_STEP_CHUNKS
    tl = nchunk * GDN_CHUNK
    ngrp = GDN_HEADS // GDN_GROUP
    blk = pl.BlockSpec((1, tl, n), lambda bi, ci: (bi, ci, 0))
    return pl.pallas_call(
        functools.partial(_gdn_chunk_kernel, nchunk=nchunk),
        out_shape=jax.ShapeDtypeStruct((b, l, n), BF16),
        grid=(b, l // tl),
        in_specs=[blk, blk, blk, blk, blk,
                  pl.BlockSpec((1, nchunk, ngrp, GDN_GROUP * GDN_CHUNK), lambda bi, ci: (bi, ci, 0, 0))],
        out_specs=blk,
        scratch_shapes=[pltpu.VMEM((ngrp, GDN_GROUP * GDN_DK, GDN_DV), F32)],
        compiler_params=_cparams("parallel", "arbitrary"),
        name="gdn_chunks",
    )(q, k, v, bw, gw, grow)


def _gdn_out_kernel(o_ref, z_ref, gn_ref, w_ref, x_ref, out_ref, a_s):
    gn = gn_ref[...]
    for hh in range(GDN_HEADS):
        sl = slice(hh * GDN_DV, (hh + 1) * GDN_DV)
        a_s[:, sl] = (_rms(o_ref[:, sl].astype(F32), gn) * z_ref[:, sl].astype(F32)).astype(BF16)
    out_ref[...] = x_ref[...] + _dot(a_s[...], w_ref[...])


def _gdn_out(o, zs, gn, w, x):
    t, d = x.shape
    tm = min(FFN_TM, t)
    row = lambda n: pl.BlockSpec((tm, n), lambda i: (i, 0))
    full = lambda a: pl.BlockSpec(a.shape, lambda i: (0, 0))
    return pl.pallas_call(
        _gdn_out_kernel,
        out_shape=jax.ShapeDtypeStruct((t, d), F32),
        grid=(t // tm,),
        in_specs=[row(GDN_V), row(GDN_V), full(gn), full(w), row(d)],
        out_specs=row(d),
        scratch_shapes=[pltpu.VMEM((tm, GDN_V), BF16)],
        compiler_params=_cparams("parallel"),
        name="gdn_out",
    )(o, zs, gn, w, x)


def _chunk_tri(tm):
    t = np.arange(tm)
    same = (t[:, None] // GDN_CHUNK) == (t[None, :] // GDN_CHUNK)
    return jnp.asarray((same & (t[:, None] >= t[None, :])).astype(np.float32), BF16)


def _row(v):
    return v.reshape(1, -1).astype(F32)


def _pad_lanes(v, offset):
    out = jnp.zeros((1, LANES), F32)
    return out.at[0, offset:offset + v.shape[0]].set(v.astype(F32))


def _gated_deltanet(x, norm_g, w_in, conv_w, a_log, dt_bias, out_norm, w_out, batch, seq):
    t = x.shape[0]
    wqkv = w_in[:, :GDN_CONV_DIM].astype(BF16)
    wz = w_in[:, GDN_CONV_DIM:GDN_CONV_DIM + GDN_V].astype(BF16)
    wba = jnp.pad(w_in[:, GDN_CONV_DIM + GDN_V:], ((0, 0), (0, LANES - 2 * GDN_HEADS))).astype(BF16)
    q, k, v, zs, bw, gw, bg = _gdn_in(x, _row(norm_g), wqkv, wz, wba, conv_w.astype(F32),
                                      _pad_lanes(a_log, GDN_HEADS), _pad_lanes(dt_bias, GDN_HEADS),
                                      _chunk_tri(min(GDN_TM, seq)), seq)
    nc = seq // GDN_CHUNK
    grow = bg[:, GDN_HEADS:2 * GDN_HEADS].reshape(batch, nc, GDN_CHUNK, GDN_HEADS).transpose(0, 1, 3, 2)
    grow = grow.reshape(batch, nc, GDN_HEADS // GDN_GROUP, GDN_GROUP * GDN_CHUNK)
    shp = (batch, seq, GDN_QK)
    o = _gdn_chunks(q.reshape(shp), k.reshape(shp), v.reshape(shp), bw.reshape(shp), gw.reshape(shp), grow)
    return _gdn_out(o.reshape(t, GDN_V), zs, _row(out_norm), w_out.astype(BF16), x)


def _stick_breaking(x, norm_g, w_qkv, w_out, batch, seq):
    t = x.shape[0]
    n = SB_HEADS * SB_HD
    w = jnp.concatenate([w_qkv[:, :n] * (SB_HD ** -0.5 * LOG2E), w_qkv[:, n:]], axis=1).astype(BF16)
    qkv = _norm_matmul(x, _row(norm_g), w, 1024, BF16)
    o = _sb_attention(qkv.reshape(batch, seq, 3 * n), _sb_suffix_matrix(SB_TK))
    return _matmul_res(o.reshape(t, n), w_out.astype(BF16), x)


def _mla(x, norm_g, positions, w_in, q_norm, w_uq, kv_norm, w_ukv, w_out, batch, seq):
    t = x.shape[0]
    wq = w_in[:, :MLA_Q_RANK].astype(BF16)
    wkv = w_in[:, MLA_Q_RANK:MLA_Q_RANK + MLA_KV_RANK].astype(BF16)
    wkr = jnp.pad(w_in[:, MLA_Q_RANK + MLA_KV_RANK:], ((0, 0), (0, LANES - MLA_ROPE))).astype(BF16)
    qd = MLA_NOPE + MLA_ROPE
    wuq = jnp.pad(w_uq.reshape(MLA_Q_RANK, MLA_HEADS, qd), ((0, 0), (0, 0), (0, 2 * LANES - qd)))
    wuq = wuq.reshape(MLA_Q_RANK, MLA_HEADS * 2 * LANES).astype(BF16)
    half = MLA_ROPE // 2
    inv_freq = ROPE_THETA ** (-jnp.arange(half, dtype=F32) / half)
    freqs = jnp.concatenate([inv_freq, inv_freq])
    invf = jnp.zeros((2, LANES), F32).at[0, :MLA_ROPE].set(freqs).at[1, LANES // 2:LANES // 2 + MLA_ROPE].set(freqs)
    wukv = w_ukv.reshape(MLA_KV_RANK, MLA_HEADS, MLA_NOPE + MLA_V)
    wuk = wukv[:, :, :MLA_NOPE].reshape(MLA_KV_RANK, MLA_HEADS * MLA_NOPE).astype(BF16)
    wuvt = wukv[:, :, MLA_NOPE:].reshape(MLA_KV_RANK, MLA_HEADS * MLA_V).T.astype(BF16)
    qc, kn, kr, vt = _mla_proj(x, _row(norm_g), wq, wkv, wkr, _row(q_norm), wuq, _row(kv_norm),
                               wuk, wuvt, positions.reshape(t, 1), invf)
    tk = vt.shape[-1]
    o = _mla_attention(qc.reshape(batch, seq, MLA_HEADS * 2 * LANES), kn.reshape(batch, seq, MLA_HEADS * MLA_NOPE),
                       kr.reshape(batch, seq, LANES), vt.reshape(batch, seq // tk, MLA_HEADS * MLA_V, tk))
    return _matmul_res(o.reshape(t, MLA_HEADS * MLA_V), w_out.astype(BF16), x)


def kernel(x, p, positions, ffn1_norm, ffn1_w_gate, ffn1_w_up, ffn1_w_down, mix_norm, ffn2_norm, ffn2_w_gate, ffn2_w_up, ffn2_w_down, ple_norm, ple_w_gate, ple_w_proj, gdn_w_in, gdn_conv_w, gdn_a_log, gdn_dt_bias, gdn_out_norm, gdn_w_out, sb_w_qkv, sb_w_out, mla_w_in, mla_q_norm, mla_w_uq, mla_kv_norm, mla_w_ukv, mla_w_out, final_norm):
    batch, seq, d = x.shape
    t = batch * seq
    depth = ffn1_norm.shape[0]
    x = x.reshape(t, d)
    p = p.reshape(depth, t, PLE_DIM)
    for i in range(depth):
        mixer, slot = i % N_MIXERS, i // N_MIXERS
        x = _ffn(x, _row(ffn1_norm[i]), ffn1_w_gate[i].astype(BF16), ffn1_w_up[i].astype(BF16),
                 ffn1_w_down[i].astype(BF16))
        if mixer == 0:
            x = _gated_deltanet(x, mix_norm[i], gdn_w_in[slot], gdn_conv_w[slot], gdn_a_log[slot],
                                gdn_dt_bias[slot], gdn_out_norm[slot], gdn_w_out[slot], batch, seq)
        elif mixer == 1:
            x = _stick_breaking(x, mix_norm[i], sb_w_qkv[slot], sb_w_out[slot], batch, seq)
        else:
            x = _mla(x, mix_norm[i], positions, mla_w_in[slot], mla_q_norm[slot], mla_w_uq[slot],
                     mla_kv_norm[slot], mla_w_ukv[slot], mla_w_out[slot], batch, seq)
        x = _ffn(x, _row(ffn2_norm[i]), ffn2_w_gate[i].astype(BF16), ffn2_w_up[i].astype(BF16),
                 ffn2_w_down[i].astype(BF16))
        x = _ple(x, _row(ple_norm[i]), ple_w_gate[i].astype(BF16), p, i, ple_w_proj[i].astype(BF16),
                 _row(final_norm), final=(i == depth - 1))
    return x.reshape(batch, seq, d)
```
